```python
import math
import jax, jax.numpy as jnp
from jax import lax
import numpy as np

D_MODEL = 1024
BATCH = 4
SEQ = 4096
DEPTH = 2

GRID_W = 64
CTX_LEN = 256
HEAD_DIM = 64
ATTN_SCALE = HEAD_DIM ** -0.5
Q_BLOCK = 128
ROPE_BASE = 10000.0
ROPE_AXIS_DIM = HEAD_DIM // 2
ROPE_PAIRS_PER_AXIS = ROPE_AXIS_DIM // 2
EPS = 1e-6

POOL_WINDOWS = (2, 4, 8, 16)
POOL_GROUPS = len(POOL_WINDOWS)
POOL_WIDTH = D_MODEL // 2
POOL_GROUP_DIM = POOL_WIDTH // POOL_GROUPS

DIFF_HEADS = D_MODEL // 256
DIFF_WIDTH = DIFF_HEADS * 2 * HEAD_DIM

GQA_HEADS = D_MODEL // 128
GQA_KV_HEADS = GQA_HEADS // 4
GQA_GROUP = GQA_HEADS // GQA_KV_HEADS
GQA_WIDTH = GQA_HEADS * HEAD_DIM
GQA_KV_WIDTH = GQA_KV_HEADS * HEAD_DIM

N_BRANCH = 3
D_FF = 4 * D_MODEL

Q_SIDE_SPLITS = (POOL_WIDTH, POOL_WIDTH + DIFF_WIDTH, POOL_WIDTH + DIFF_WIDTH + GQA_WIDTH)
KV_OFF = POOL_WIDTH + DIFF_WIDTH + GQA_WIDTH + N_BRANCH * D_MODEL
KV_SPLITS = (DIFF_WIDTH, 2 * DIFF_WIDTH, 2 * DIFF_WIDTH + GQA_KV_WIDTH)
IN_WIDTH = KV_OFF + 2 * DIFF_WIDTH + 2 * GQA_KV_WIDTH

kernel_name = 'hybrid_pool_diffattn_gqa_prefix_dit_block'


def _rms_norm(x, g):
    xf = x.astype(jnp.float32)
    y = xf * lax.rsqrt(jnp.mean(xf * xf, axis=-1, keepdims=True) + EPS)
    return (y * g.astype(jnp.float32)).astype(x.dtype)


def _modulate(h, shift, scale):
    return h * (1 + scale) + shift


def _rope_tables(rows):
    row = jnp.repeat(jnp.arange(rows), GRID_W).astype(jnp.float32)
    col = jnp.tile(jnp.arange(GRID_W), rows).astype(jnp.float32)
    inv = 1.0 / (ROPE_BASE ** (jnp.arange(ROPE_PAIRS_PER_AXIS, dtype=jnp.float32) * 2.0 / ROPE_AXIS_DIM))
    ang = jnp.concatenate([row[:, None] * inv, col[:, None] * inv], axis=-1)
    return jnp.cos(ang), jnp.sin(ang)


def _rope(x, cos, sin):
    xf = x.astype(jnp.float32).reshape(x.shape[:-1] + (HEAD_DIM // 2, 2))
    x0, x1 = xf[..., 0], xf[..., 1]
    out = jnp.stack([x0 * cos - x1 * sin, x0 * sin + x1 * cos], axis=-1)
    return out.reshape(x.shape).astype(x.dtype)


def _heads(x, n):
    b, l, w = x.shape
    return x.reshape(b, l, n, w // n).transpose(0, 2, 1, 3)


def _merge_heads(o):
    b, h, l, d = o.shape
    return o.transpose(0, 2, 1, 3).reshape(b, l, h * d)


def _probs(s):
    return jax.nn.softmax(s.astype(jnp.float32) * ATTN_SCALE, axis=-1)


def _sweep_query_blocks(block_fn, queries):
    n = queries[0].shape[-2]
    nb = n // Q_BLOCK

    def split(q):
        q = q.reshape(q.shape[:-2] + (nb, Q_BLOCK, q.shape[-1]))
        return jnp.moveaxis(q, -3, 0)

    out = lax.map(lambda blk: block_fn(*blk), tuple(split(q) for q in queries))
    out = jnp.moveaxis(out, 0, -3)
    return out.reshape(out.shape[:-3] + (n, out.shape[-1]))


def _pool_mixer(z, w_grp, scale):
    b, l, _ = z.shape
    zg = z.reshape(b, l, POOL_GROUPS, POOL_GROUP_DIM)
    zf = zg.astype(jnp.float32)
    cs = jnp.concatenate([jnp.zeros((b, 1, POOL_GROUPS, POOL_GROUP_DIM), jnp.float32),
                          jnp.cumsum(zf, axis=1)], axis=1)
    t = jnp.arange(l)
    means = []
    for g, w in enumerate(POOL_WINDOWS):
        lo = jnp.clip(t - w // 2, 0, l)
        hi = jnp.clip(t + w // 2, 0, l)
        cnt = (hi - lo).astype(jnp.float32)[None, :, None]
        means.append((cs[:, hi, g] - cs[:, lo, g]) / cnt)
    pooled = (jnp.stack(means, axis=2) - zf).astype(z.dtype)
    mixed = jnp.einsum('blgc,gcd->blgd', pooled, w_grp)
    return mixed.reshape(b, l, POOL_WIDTH) * scale


def _diff_lambda(lq1, lk1, lq2, lk2, lam_init):
    def e(a, b):
        return jnp.exp(jnp.sum(a.astype(jnp.float32) * b.astype(jnp.float32)))
    return e(lq1, lk1) - e(lq2, lk2) + lam_init


def _diff_attention(q1, q2, k1, k2, v, lam, sub_gain, lam_init):
    def block(a, b):
        p = (_probs(jnp.einsum('bhqd,bhkd->bhqk', a, k1))
             - lam * _probs(jnp.einsum('bhqd,bhkd->bhqk', b, k2)))
        return jnp.einsum('bhqk,bhkd->bhqd', p.astype(v.dtype), v)

    o = _sweep_query_blocks(block, (q1, q2))
    o = _rms_norm(o, sub_gain) * (1.0 - lam_init)
    return _merge_heads(o)


def _gq_attention(q, k, v):
    b, h, l, d = q.shape
    qg = q.reshape(b, GQA_KV_HEADS, GQA_GROUP, l, d)

    def block(a):
        p = _probs(jnp.einsum('bhgqd,bhkd->bhgqk', a, k))
        return jnp.einsum('bhgqk,bhkd->bhgqd', p.astype(v.dtype), v)

    o = _sweep_query_blocks(block, (qg,))
    return _merge_heads(o.reshape(b, h, l, d))


def _kv_heads(kv_part, k_gain, cos, sin):
    dk, dv, gk, gv = jnp.split(kv_part, KV_SPLITS, axis=-1)
    dk = _heads(dk, DIFF_HEADS)
    k1, k2 = dk[..., :HEAD_DIM], dk[..., HEAD_DIM:]
    dv = _heads(dv, DIFF_HEADS)
    gk = _rms_norm(_heads(gk, GQA_KV_HEADS), k_gain)
    gv = _heads(gv, GQA_KV_HEADS)
    if cos is not None:
        k1, k2, gk = _rope(k1, cos, sin), _rope(k2, cos, sin), _rope(gk, cos, sin)
    return (k1, k2, dv, gk, gv)


def _mix(q_part, keys, lp, lam, lam_init, cos, sin):
    pool_in, dq, gq, gates = jnp.split(q_part, Q_SIDE_SPLITS, axis=-1)
    k1, k2, dv, gk, gv = keys
    dq = _heads(dq, DIFF_HEADS)
    q1, q2 = dq[..., :HEAD_DIM], dq[..., HEAD_DIM:]
    gq = _rms_norm(_heads(gq, GQA_HEADS), lp['gqa_q_norm'])
    if cos is not None:
        q1, q2, gq = _rope(q1, cos, sin), _rope(q2, cos, sin), _rope(gq, cos, sin)
    pool_out = _pool_mixer(pool_in, lp['w_pool_grp'], lp['pool_scale'])
    diff_out = _diff_attention(q1, q2, k1, k2, dv, lam, lp['diff_subln'], lam_init)
    gqa_out = _gq_attention(gq, gk, gv)
    b, l, _ = gates.shape
    g = jax.nn.sigmoid(gates.reshape(b, l, N_BRANCH, D_MODEL))
    merged = (g[:, :, 0] * (pool_out @ lp['w_pool_out'])
              + g[:, :, 1] * (diff_out @ lp['w_diff_out'])
              + g[:, :, 2] * (gqa_out @ lp['w_gqa_out']))
    return merged @ lp['w_o']


def _mlp(h, w1, w2):
    return jnp.square(jax.nn.relu(h @ w1)) @ w2


def _layer(x, ctx, c, c_ctx, lp, lam_init, cos, sin, update_ctx):
    sh_a, sc_a, gt_a, sh_f, sc_f, gt_f = jnp.split(
        (jax.nn.silu(c) @ lp['w_mod'] + lp['b_mod'])[:, None, :], 6, axis=-1)
    csh_a, csc_a, cgt_a, csh_f, csc_f, cgt_f = jnp.split(
        jax.nn.silu(c_ctx) @ lp['w_mod'] + lp['b_mod'], 6, axis=-1)
    lam = _diff_lambda(lp['lambda_q1'], lp['lambda_k1'], lp['lambda_q2'], lp['lambda_k2'], lam_init)

    h_lat = _modulate(_rms_norm(x, lp['g_pre_mix']), sh_a, sc_a)
    h_ctx = _modulate(_rms_norm(ctx, lp['g_pre_mix']), csh_a, csc_a)

    p_lat = h_lat @ lp['w_in']
    p_ctx = h_ctx @ lp['w_in'] if update_ctx else h_ctx @ lp['w_in'][:, KV_OFF:]
    ctx_kv = _kv_heads(p_ctx[..., -(IN_WIDTH - KV_OFF):], lp['gqa_k_norm'], None, None)
    lat_kv = _kv_heads(p_lat[..., KV_OFF:], lp['gqa_k_norm'], cos, sin)
    keys = tuple(jnp.concatenate([kc, kl], axis=2) for kc, kl in zip(ctx_kv, lat_kv))

    y_lat = _mix(p_lat[..., :KV_OFF], keys, lp, lam, lam_init, cos, sin)
    x = x + gt_a * _rms_norm(y_lat, lp['g_post_mix'])
    hf = _modulate(_rms_norm(x, lp['g_pre_ffn']), sh_f, sc_f)
    x = x + gt_f * _rms_norm(_mlp(hf, lp['w_ff1'], lp['w_ff2']), lp['g_post_ffn'])

    if update_ctx:
        y_ctx = _mix(p_ctx[..., :KV_OFF], ctx_kv, lp, lam, lam_init, None, None)
        ctx = ctx + cgt_a * _rms_norm(y_ctx, lp['g_post_mix'])
        hc = _modulate(_rms_norm(ctx, lp['g_pre_ffn']), csh_f, csc_f)
        ctx = ctx + cgt_f * _rms_norm(_mlp(hc, lp['w_ff1'], lp['w_ff2']), lp['g_post_ffn'])
    return x, ctx


def setup_inputs(seed: int = 0) -> dict:
    key = jax.random.key(seed)
    ks = jax.random.split(key, 32)

    def nrm(k, shape, scale):
        return jax.random.normal(k, shape, jnp.float32) * scale

    def gain(k, shape):
        return 1.0 + 0.02 * jax.random.normal(k, shape, jnp.float32)

    return {
        'x': nrm(ks[0], (BATCH, SEQ, D_MODEL), 1.0),
        'c': nrm(ks[1], (BATCH, D_MODEL), 1.0),
        'ctx': nrm(ks[2], (BATCH, CTX_LEN, D_MODEL), 1.0),
        'c_ctx': nrm(ks[3], (D_MODEL,), 1.0),
        'w_mod': nrm(ks[4], (DEPTH, D_MODEL, 6 * D_MODEL), D_MODEL ** -0.5),
        'b_mod': nrm(ks[5], (DEPTH, 6 * D_MODEL), 0.01),
        'g_pre_mix': gain(ks[6], (DEPTH, D_MODEL)),
        'g_post_mix': gain(ks[7], (DEPTH, D_MODEL)),
        'g_pre_ffn': gain(ks[8], (DEPTH, D_MODEL)),
        'g_post_ffn': gain(ks[9], (DEPTH, D_MODEL)),
        'w_in': nrm(ks[10], (DEPTH, D_MODEL, IN_WIDTH), D_MODEL ** -0.5),
        'w_pool_grp': nrm(ks[11], (DEPTH, POOL_GROUPS, POOL_GROUP_DIM, POOL_GROUP_DIM), POOL_GROUP_DIM ** -0.5),
        'pool_scale': gain(ks[12], (DEPTH, POOL_WIDTH)),
        'lambda_q1': nrm(ks[13], (DEPTH, HEAD_DIM), 0.1),
        'lambda_k1': nrm(ks[14], (DEPTH, HEAD_DIM), 0.1),
        'lambda_q2': nrm(ks[15], (DEPTH, HEAD_DIM), 0.1),
        'lambda_k2': nrm(ks[16], (DEPTH, HEAD_DIM), 0.1),
        'diff_subln': gain(ks[17], (DEPTH, 2 * HEAD_DIM)),
        'gqa_q_norm': gain(ks[18], (DEPTH, HEAD_DIM)),
        'gqa_k_norm': gain(ks[19], (DEPTH, HEAD_DIM)),
        'w_pool_out': nrm(ks[20], (DEPTH, POOL_WIDTH, D_MODEL), POOL_WIDTH ** -0.5),
        'w_diff_out': nrm(ks[21], (DEPTH, DIFF_WIDTH, D_MODEL), DIFF_WIDTH ** -0.5),
        'w_gqa_out': nrm(ks[22], (DEPTH, GQA_WIDTH, D_MODEL), GQA_WIDTH ** -0.5),
        'w_o': nrm(ks[23], (DEPTH, D_MODEL, D_MODEL), D_MODEL ** -0.5),
        'w_ff1': nrm(ks[24], (DEPTH, D_MODEL, D_FF), D_MODEL ** -0.5),
        'w_ff2': nrm(ks[25], (DEPTH, D_FF, D_MODEL), D_FF ** -0.5),
    }


def reference(x, c, ctx, c_ctx, w_mod, b_mod, g_pre_mix, g_post_mix, g_pre_ffn, g_post_ffn,
              w_in, w_pool_grp, pool_scale, lambda_q1, lambda_k1, lambda_q2, lambda_k2,
              diff_subln, gqa_q_norm, gqa_k_norm, w_pool_out, w_diff_out, w_gqa_out, w_o,
              w_ff1, w_ff2):
    rows = x.shape[1] // GRID_W
    cos, sin = _rope_tables(rows)
    for i in range(DEPTH):
        lp = {
            'w_mod': w_mod[i], 'b_mod': b_mod[i],
            'g_pre_mix': g_pre_mix[i], 'g_post_mix': g_post_mix[i],
            'g_pre_ffn': g_pre_ffn[i], 'g_post_ffn': g_post_ffn[i],
            'w_in': w_in[i], 'w_pool_grp': w_pool_grp[i], 'pool_scale': pool_scale[i],
            'lambda_q1': lambda_q1[i], 'lambda_k1': lambda_k1[i],
            'lambda_q2': lambda_q2[i], 'lambda_k2': lambda_k2[i],
            'diff_subln': diff_subln[i], 'gqa_q_norm': gqa_q_norm[i], 'gqa_k_norm': gqa_k_norm[i],
            'w_pool_out': w_pool_out[i], 'w_diff_out': w_diff_out[i], 'w_gqa_out': w_gqa_out[i],
            'w_o': w_o[i], 'w_ff1': w_ff1[i], 'w_ff2': w_ff2[i],
        }
        lam_init = 0.8 - 0.6 * math.exp(-0.3 * i)
        x, ctx = _layer(x, ctx, c, c_ctx, lp, lam_init, cos, sin, i < DEPTH - 1)
    return x
```

```python
import functools
import math

import jax
import jax.numpy as jnp
from jax import lax
from jax.experimental import pallas as pl
from jax.experimental.pallas import tpu as pltpu

F32 = jnp.float32
BF16 = jnp.bfloat16

D_MODEL = 1024
GRID_W = 64
HEAD_DIM = 64
ATTN_SCALE = HEAD_DIM ** -0.5
ROPE_BASE = 10000.0
ROPE_AXIS_DIM = HEAD_DIM // 2
ROPE_PAIRS_PER_AXIS = ROPE_AXIS_DIM // 2
EPS = 1e-6

POOL_WINDOWS = (2, 4, 8, 16)
POOL_WIDTH = D_MODEL // 2
POOL_GROUP_DIM = POOL_WIDTH // len(POOL_WINDOWS)
POOL_HALO = max(POOL_WINDOWS) // 2

DIFF_HEADS = D_MODEL // 256
DIFF_WIDTH = DIFF_HEADS * 2 * HEAD_DIM
GQA_HEADS = D_MODEL // 128
GQA_KV_HEADS = GQA_HEADS // 4
GQA_GROUP = GQA_HEADS // GQA_KV_HEADS
GQA_WIDTH = GQA_HEADS * HEAD_DIM
GQA_KV_WIDTH = GQA_KV_HEADS * HEAD_DIM
N_BRANCH = 3
D_FF = 4 * D_MODEL

OFF_POOL = 0
OFF_DQ = OFF_POOL + POOL_WIDTH
OFF_GQ = OFF_DQ + DIFF_WIDTH
OFF_GATE = OFF_GQ + GQA_WIDTH
OFF_DK = OFF_GATE + N_BRANCH * D_MODEL
OFF_DV = OFF_DK + DIFF_WIDTH
OFF_GK = OFF_DV + DIFF_WIDTH
OFF_GV = OFF_GK + GQA_KV_WIDTH
IN_WIDTH = OFF_GV + GQA_KV_WIDTH

LANES = 128
KEY_CHUNK = 256
PROJ_CHUNK = 512
VMEM_LIMIT = 56 * 1024 * 1024

MOD_ROWS = 8


def _cparams(sem):
    return pltpu.CompilerParams(dimension_semantics=sem, vmem_limit_bytes=VMEM_LIMIT)


def _resident(shape):
    nd = len(shape)
    return pl.BlockSpec(shape, lambda *_: (0,) * nd, pipeline_mode=pl.Buffered(1))


def _rms(x, g):
    ms = jnp.mean(x * x, axis=-1, keepdims=True)
    return x * lax.rsqrt(ms + EPS) * g


def _mod_kernel(c_ref, w_ref, b_ref, o_ref):
    c = c_ref[...]
    a = c * (1.0 / (1.0 + jnp.exp(-c)))
    o_ref[0] = jnp.dot(a.astype(BF16), w_ref[0].astype(BF16),
                       preferred_element_type=F32) + b_ref[0]


def _modulation(c_all, w_mod, b_mod):
    depth = w_mod.shape[0]
    nblk = w_mod.shape[2] // D_MODEL
    out = pl.pallas_call(
        _mod_kernel,
        grid=(depth, nblk),
        in_specs=[
            pl.BlockSpec((MOD_ROWS, D_MODEL), lambda l, j: (0, 0)),
            pl.BlockSpec((1, D_MODEL, D_MODEL), lambda l, j: (l, 0, j)),
            pl.BlockSpec((1, 1, D_MODEL), lambda l, j: (l, 0, j)),
        ],
        out_specs=pl.BlockSpec((1, MOD_ROWS, D_MODEL), lambda l, j: (l, 0, j)),
        out_shape=jax.ShapeDtypeStruct((depth, MOD_ROWS, nblk * D_MODEL), F32),
        compiler_params=_cparams(("arbitrary", "arbitrary")),
        name="modulation",
    )(c_all, w_mod, b_mod.reshape(depth, 1, nblk * D_MODEL))
    return out.reshape(depth, MOD_ROWS, nblk, D_MODEL)


def _pair_swap(x):
    lane = lax.broadcasted_iota(jnp.int32, x.shape, 1)
    return jnp.where((lane & 1) == 0, pltpu.roll(x, LANES - 1, 1), pltpu.roll(x, 1, 1))


def _rope(x, cos, sin_signed):
    return x * cos + _pair_swap(x) * sin_signed


def _head_norm(x, gain2):
    lane = lax.broadcasted_iota(jnp.int32, x.shape, 1)
    lo = lane < HEAD_DIM
    x2 = x * x
    ms_lo = jnp.sum(jnp.where(lo, x2, 0.0), axis=-1, keepdims=True) * (1.0 / HEAD_DIM)
    ms_hi = jnp.sum(jnp.where(lo, 0.0, x2), axis=-1, keepdims=True) * (1.0 / HEAD_DIM)
    inv = jnp.where(lo, lax.rsqrt(ms_lo + EPS), lax.rsqrt(ms_hi + EPS))
    return x * inv * gain2


def _in_proj_kernel(*refs, rope, q_side):
    it = iter(refs)
    x_ref, mod_ref, gpre_ref, w_ref, qg_ref, kg_ref = (next(it) for _ in range(6))
    cos_ref = next(it) if rope else None
    sin_ref = next(it) if rope else None
    if q_side:
        pool_ref, dq_ref, gq_ref, gate_ref = (next(it) for _ in range(4))
    dkt_ref, dv_ref, gkt_ref, gv_ref = (next(it) for _ in range(4))

    x = x_ref[...]
    mod = mod_ref[0, 0]
    h = _rms(x, gpre_ref[...]) * (1.0 + mod[1:2]) + mod[0:1]
    hb = h.astype(BF16)
    if rope:
        cos, sin = cos_ref[...], sin_ref[...]

    def proj(off, width):
        return jnp.dot(hb, w_ref[:, off:off + width], preferred_element_type=F32)

    def blocks(off, width):
        y = proj(off, width)
        return [y[:, j * LANES:(j + 1) * LANES] for j in range(width // LANES)]

    def split_heads(dst_ref, j, y):
        dst_ref[0, 2 * j] = y[:, :HEAD_DIM].astype(BF16)
        dst_ref[0, 2 * j + 1] = y[:, HEAD_DIM:].astype(BF16)

    if q_side:
        pool_ref[...] = proj(OFF_POOL, POOL_WIDTH)
        for j in range(N_BRANCH * D_MODEL // PROJ_CHUNK):
            gate_ref[:, j * PROJ_CHUNK:(j + 1) * PROJ_CHUNK] = proj(
                OFF_GATE + j * PROJ_CHUNK, PROJ_CHUNK)
        for hh, y in enumerate(blocks(OFF_DQ, DIFF_WIDTH)):
            if rope:
                y = _rope(y, cos, sin)
            split_heads(dq_ref, hh, y * ATTN_SCALE)
        for jj, y in enumerate(blocks(OFF_GQ, GQA_WIDTH)):
            y = _head_norm(y, qg_ref[...])
            if rope:
                y = _rope(y, cos, sin)
            split_heads(gq_ref, jj, y * ATTN_SCALE)

    for hh, y in enumerate(blocks(OFF_DK, DIFF_WIDTH)):
        if rope:
            y = _rope(y, cos, sin)
        dkt_ref[0, hh] = y.T.astype(BF16)
    dv_ref[...] = proj(OFF_DV, DIFF_WIDTH).astype(BF16)
    (y,) = blocks(OFF_GK, GQA_KV_WIDTH)
    y = _head_norm(y, kg_ref[...])
    if rope:
        y = _rope(y, cos, sin)
    gkt_ref[0] = y.T.astype(BF16)
    split_heads(gv_ref, 0, proj(OFF_GV, GQA_KV_WIDTH))


def _in_proj(x, mod, layer, mod_row_fn, g_pre, w_in, q_gain2, k_gain2, rope_tabs, *,
             seg_len, tm, q_side):
    n = x.shape[0]
    n_seg = n // seg_len
    tps = seg_len // tm
    rope = rope_tabs is not None
    seg = lambda i: i // tps
    pos = lambda i: i % tps

    in_specs = [
        pl.BlockSpec((tm, D_MODEL), lambda i: (i, 0)),
        pl.BlockSpec((1, 1, 6, D_MODEL), lambda i: (layer, mod_row_fn(seg(i)), 0, 0)),
        _resident((1, D_MODEL)),
        _resident(w_in.shape),
        _resident((1, LANES)),
        _resident((1, LANES)),
    ]
    args = [x, mod, g_pre, w_in, q_gain2, k_gain2]
    if rope:
        in_specs += [pl.BlockSpec((tm, LANES), lambda i: (pos(i), 0))] * 2
        args += list(rope_tabs)

    out_shape, out_specs = [], []
    if q_side:
        out_shape += [
            jax.ShapeDtypeStruct((n, POOL_WIDTH), F32),
            jax.ShapeDtypeStruct((n_seg, 2 * DIFF_HEADS, seg_len, HEAD_DIM), BF16),
            jax.ShapeDtypeStruct((n_seg, GQA_HEADS, seg_len, HEAD_DIM), BF16),
            jax.ShapeDtypeStruct((n, N_BRANCH * D_MODEL), F32),
        ]
        out_specs += [
            pl.BlockSpec((tm, POOL_WIDTH), lambda i: (i, 0)),
            pl.BlockSpec((1, 2 * DIFF_HEADS, tm, HEAD_DIM), lambda i: (seg(i), 0, pos(i), 0)),
            pl.BlockSpec((1, GQA_HEADS, tm, HEAD_DIM), lambda i: (seg(i), 0, pos(i), 0)),
            pl.BlockSpec((tm, N_BRANCH * D_MODEL), lambda i: (i, 0)),
        ]
    out_shape += [
        jax.ShapeDtypeStruct((n_seg, DIFF_HEADS, 2 * HEAD_DIM, seg_len), BF16),
        jax.ShapeDtypeStruct((n, DIFF_WIDTH), BF16),
        jax.ShapeDtypeStruct((n_seg, GQA_KV_WIDTH, seg_len), BF16),
        jax.ShapeDtypeStruct((n_seg, GQA_KV_HEADS, seg_len, HEAD_DIM), BF16),
    ]
    out_specs += [
        pl.BlockSpec((1, DIFF_HEADS, 2 * HEAD_DIM, tm), lambda i: (seg(i), 0, 0, pos(i))),
        pl.BlockSpec((tm, DIFF_WIDTH), lambda i: (i, 0)),
        pl.BlockSpec((1, GQA_KV_WIDTH, tm), lambda i: (seg(i), 0, pos(i))),
        pl.BlockSpec((1, GQA_KV_HEADS, tm, HEAD_DIM), lambda i: (seg(i), 0, pos(i), 0)),
    ]
    return pl.pallas_call(
        functools.partial(_in_proj_kernel, rope=rope, q_side=q_side),
        grid=(n // tm,),
        in_specs=in_specs,
        out_specs=out_specs,
        out_shape=out_shape,
        compiler_params=_cparams(("arbitrary",)),
        name="in_proj" + ("_rope" if rope else "") + ("" if q_side else "_kv"),
    )(*args)


def _softmax_pv(q_rows, kt_chunks, v_scr, s_scr, m_scr, *, row_block):
    n_chunks = len(kt_chunks(q_rows[0][2]))
    for row0, q, ksel in q_rows:
        kts = kt_chunks(ksel)
        for rb in range(q.shape[0] // row_block):
            r0 = rb * row_block
            qb = q[r0:r0 + row_block]
            m_acc = None
            for c in range(n_chunks):
                s = jnp.dot(qb, kts[c](), preferred_element_type=F32)
                s_scr[row0 + r0:row0 + r0 + row_block, c * KEY_CHUNK:(c + 1) * KEY_CHUNK] = s
                mm = jnp.maximum(s[:, :LANES], s[:, LANES:])
                m_acc = mm if m_acc is None else jnp.maximum(m_acc, mm)
            m = jnp.max(m_acc, axis=-1, keepdims=True)
            m_scr[row0 + r0:row0 + r0 + row_block, :] = jnp.broadcast_to(m, (row_block, KEY_CHUNK))
    acc = None
    for c in range(n_chunks):
        p = jnp.exp(s_scr[:, c * KEY_CHUNK:(c + 1) * KEY_CHUNK] - m_scr[...]).astype(BF16)
        d = jnp.dot(p, v_scr[c * KEY_CHUNK:(c + 1) * KEY_CHUNK, :], preferred_element_type=F32)
        acc = d if acc is None else acc + d
    return acc


def _fill_values(v_scr, v_refs, width):
    r = 0
    for v_ref in v_refs:
        v = v_ref[...]
        v = v.reshape(v.shape[-2], v.shape[-1])
        v_scr[r:r + v.shape[0], 0:width] = v
        r += v.shape[0]
    v_scr[:, width:] = jnp.ones((v_scr.shape[0], v_scr.shape[1] - width), BF16)


def _diff_attn_kernel(*refs, n_seg_keys, tq, lam_init):
    it = iter(refs)
    lam_ref, subg_ref, q_ref = next(it), next(it), next(it)
    kt_refs = [next(it) for _ in n_seg_keys]
    v_refs = [next(it) for _ in n_seg_keys]
    o_ref, s_scr, m_scr, v_scr = next(it), next(it), next(it), next(it)

    @pl.when(pl.program_id(2) == 0)
    def _():
        _fill_values(v_scr, v_refs, 2 * HEAD_DIM)

    def kt_chunks(krow):
        out = []
        for kt_ref, nk in zip(kt_refs, n_seg_keys):
            for c in range(nk // KEY_CHUNK):
                out.append(functools.partial(
                    lambda r, cc: r[0, 0, krow:krow + HEAD_DIM, cc * KEY_CHUNK:(cc + 1) * KEY_CHUNK],
                    kt_ref, c))
        return out

    acc = _softmax_pv([(0, q_ref[0, 0], 0), (tq, q_ref[0, 1], HEAD_DIM)],
                      kt_chunks, v_scr, s_scr, m_scr, row_block=min(tq, 256))
    vw = 2 * HEAD_DIM
    o1 = acc[:tq, :vw] / acc[:tq, vw:]
    o2 = acc[tq:, :vw] / acc[tq:, vw:]
    lp = lam_ref[...]
    lam = (jnp.exp(jnp.sum(lp[0:1] * lp[1:2], axis=-1, keepdims=True))
           - jnp.exp(jnp.sum(lp[2:3] * lp[3:4], axis=-1, keepdims=True)) + lam_init)
    o = o1 - lam * o2
    o = _rms(o, subg_ref[...]) * (1.0 - lam_init)
    o_ref[...] = o.astype(BF16)


def _diff_attn(dq, kts, vs, lam_params, sub_gain, lam_init, *, tq):
    n_seg, _, lq, _ = dq.shape
    n_seg_keys = tuple(kt.shape[-1] for kt in kts)
    nk = sum(n_seg_keys)
    tpq = lq // tq
    in_specs = [
        _resident(lam_params.shape),
        _resident(sub_gain.shape),
        pl.BlockSpec((1, 2, tq, HEAD_DIM), lambda b, h, t: (b, h, t, 0)),
    ]
    in_specs += [pl.BlockSpec((1, 1, 2 * HEAD_DIM, k), lambda b, h, t: (b, h, 0, 0))
                 for k in n_seg_keys]
    in_specs += [pl.BlockSpec((k, 2 * HEAD_DIM), lambda b, h, t: (b, h)) for k in n_seg_keys]
    return pl.pallas_call(
        functools.partial(_diff_attn_kernel, n_seg_keys=n_seg_keys, tq=tq, lam_init=lam_init),
        grid=(n_seg, DIFF_HEADS, tpq),
        in_specs=in_specs,
        out_specs=pl.BlockSpec((tq, 2 * HEAD_DIM), lambda b, h, t: (b * tpq + t, h)),
        out_shape=jax.ShapeDtypeStruct((n_seg * lq, DIFF_WIDTH), BF16),
        scratch_shapes=[
            pltpu.VMEM((2 * tq, nk), F32),
            pltpu.VMEM((2 * tq, KEY_CHUNK), F32),
            pltpu.VMEM((nk, KEY_CHUNK), BF16),
        ],
        compiler_params=_cparams(("arbitrary", "arbitrary", "arbitrary")),
        name="diff_attn",
    )(lam_params, sub_gain, dq, *kts, *vs)


def _gqa_attn_kernel(*refs, n_seg_keys, tq):
    it = iter(refs)
    q_ref = next(it)
    kt_refs = [next(it) for _ in n_seg_keys]
    v_refs = [next(it) for _ in n_seg_keys]
    o_ref, s_scr, m_scr, v_scr = next(it), next(it), next(it), next(it)

    @pl.when(pl.program_id(2) == 0)
    def _():
        _fill_values(v_scr, v_refs, HEAD_DIM)

    def kt_chunks(_):
        out = []
        for kt_ref, nk in zip(kt_refs, n_seg_keys):
            for c in range(nk // KEY_CHUNK):
                out.append(functools.partial(
                    lambda r, cc: r[0, :, cc * KEY_CHUNK:(cc + 1) * KEY_CHUNK], kt_ref, c))
        return out

    q = q_ref[0].reshape(GQA_GROUP * tq, HEAD_DIM)
    acc = _softmax_pv([(0, q, None)], kt_chunks, v_scr, s_scr, m_scr, row_block=256)
    o = acc[:, :HEAD_DIM] / acc[:, HEAD_DIM:]
    o_ref[...] = jnp.concatenate(
        [o[g * tq:(g + 1) * tq] for g in range(GQA_GROUP)], axis=1).astype(BF16)


def _gqa_attn(gq, kts, vs, *, tq):
    n_seg, _, lq, _ = gq.shape
    n_seg_keys = tuple(kt.shape[-1] for kt in kts)
    nk = sum(n_seg_keys)
    tpq = lq // tq
    in_specs = [pl.BlockSpec((1, GQA_GROUP, tq, HEAD_DIM), lambda b, g, t: (b, g, t, 0))]
    in_specs += [pl.BlockSpec((1, HEAD_DIM, k), lambda b, g, t: (b, g, 0)) for k in n_seg_keys]
    in_specs += [pl.BlockSpec((1, 1, k, HEAD_DIM), lambda b, g, t: (b, g, 0, 0))
                 for k in n_seg_keys]
    return pl.pallas_call(
        functools.partial(_gqa_attn_kernel, n_seg_keys=n_seg_keys, tq=tq),
        grid=(n_seg, GQA_KV_HEADS, tpq),
        in_specs=in_specs,
        out_specs=pl.BlockSpec((tq, GQA_GROUP * HEAD_DIM), lambda b, g, t: (b * tpq + t, g)),
        out_shape=jax.ShapeDtypeStruct((n_seg * lq, GQA_WIDTH), BF16),
        scratch_shapes=[
            pltpu.VMEM((GQA_GROUP * tq, nk), F32),
            pltpu.VMEM((GQA_GROUP * tq, KEY_CHUNK), F32),
            pltpu.VMEM((nk, LANES), BF16),
        ],
        compiler_params=_cparams(("arbitrary", "arbitrary", "arbitrary")),
        name="gqa_attn",
    )(gq, *kts, *vs)


def _merge_kernel(x_ref, zp_ref, z_ref, zn_ref, do_ref, go_ref, gate_ref, mod_ref,
                  wgrp_ref, pscale_ref, wpo_ref, wdo_ref, wgo_ref, wo_ref,
                  gpost_ref, gpre_ref, xo_ref, hf_ref, zext_scr, *, seg_len, tm):
    i = pl.program_id(0)
    tps = seg_len // tm
    t = i % tps
    z = z_ref[...]
    zext_scr[0:POOL_HALO] = jnp.where(t > 0, zp_ref[...], 0.0)
    zext_scr[POOL_HALO:POOL_HALO + tm] = z
    zext_scr[POOL_HALO + tm:] = jnp.where(t < tps - 1, zn_ref[...], 0.0)
    pos = t * tm + lax.broadcasted_iota(jnp.int32, (tm, POOL_GROUP_DIM), 0)
    mixed = []
    for g, w in enumerate(POOL_WINDOWS):
        c0 = g * POOL_GROUP_DIM
        ssum = None
        for d in range(-(w // 2), w // 2):
            zz = zext_scr[POOL_HALO + d:POOL_HALO + d + tm, c0:c0 + POOL_GROUP_DIM]
            ssum = zz if ssum is None else ssum + zz
        cnt = (jnp.minimum(pos + w // 2, seg_len) - jnp.maximum(pos - w // 2, 0)).astype(F32)
        pooled = ssum / cnt - z[:, c0:c0 + POOL_GROUP_DIM]
        mixed.append(jnp.dot(pooled.astype(BF16), wgrp_ref[g], preferred_element_type=F32))
    pool_out = jnp.concatenate(mixed, axis=1) * pscale_ref[...]

    def gate(k):
        gk = gate_ref[:, k * D_MODEL:(k + 1) * D_MODEL]
        return 1.0 / (1.0 + jnp.exp(-gk))

    merged = gate(0) * jnp.dot(pool_out.astype(BF16), wpo_ref[...], preferred_element_type=F32)
    merged += gate(1) * jnp.dot(do_ref[...], wdo_ref[...], preferred_element_type=F32)
    merged += gate(2) * jnp.dot(go_ref[...], wgo_ref[...], preferred_element_type=F32)
    y = jnp.dot(merged.astype(BF16), wo_ref[...], preferred_element_type=F32)
    mod = mod_ref[0, 0]
    xn = x_ref[...] + mod[2:3] * _rms(y, gpost_ref[...])
    xo_ref[...] = xn
    hf_ref[...] = (_rms(xn, gpre_ref[...]) * (1.0 + mod[4:5]) + mod[3:4]).astype(BF16)


def _merge(x, pool_in, diff_out, gqa_out, gates, mod, layer, mod_row_fn, w_grp, pool_scale,
           w_pool_out, w_diff_out, w_gqa_out, w_o, g_post, g_pre_ffn, *, seg_len, tm):
    n = x.shape[0]
    tps = seg_len // tm
    hb = tm // POOL_HALO
    n_hblk = n // POOL_HALO
    row = lambda i: (i, 0)
    in_specs = [
        pl.BlockSpec((tm, D_MODEL), row),
        pl.BlockSpec((POOL_HALO, POOL_WIDTH), lambda i: (jnp.maximum(i * hb - 1, 0), 0)),
        pl.BlockSpec((tm, POOL_WIDTH), row),
        pl.BlockSpec((POOL_HALO, POOL_WIDTH), lambda i: (jnp.minimum((i + 1) * hb, n_hblk - 1), 0)),
        pl.BlockSpec((tm, DIFF_WIDTH), row),
        pl.BlockSpec((tm, GQA_WIDTH), row),
        pl.BlockSpec((tm, N_BRANCH * D_MODEL), row),
        pl.BlockSpec((1, 1, 6, D_MODEL), lambda i: (layer, mod_row_fn(i // tps), 0, 0)),
        _resident(w_grp.shape), _resident(pool_scale.shape), _resident(w_pool_out.shape),
        _resident(w_diff_out.shape), _resident(w_gqa_out.shape), _resident(w_o.shape),
        _resident(g_post.shape), _resident(g_pre_ffn.shape),
    ]
    return pl.pallas_call(
        functools.partial(_merge_kernel, seg_len=seg_len, tm=tm),
        grid=(n // tm,),
        in_specs=in_specs,
        out_specs=[pl.BlockSpec((tm, D_MODEL), row), pl.BlockSpec((tm, D_MODEL), row)],
        out_shape=[jax.ShapeDtypeStruct((n, D_MODEL), F32),
                   jax.ShapeDtypeStruct((n, D_MODEL), BF16)],
        scratch_shapes=[pltpu.VMEM((tm + 2 * POOL_HALO, POOL_WIDTH), F32)],
        compiler_params=_cparams(("arbitrary",)),
        name="merge",
    )(x, pool_in, pool_in, pool_in, diff_out, gqa_out, gates, mod, w_grp, pool_scale,
      w_pool_out, w_diff_out, w_gqa_out, w_o, g_post, g_pre_ffn)


def _mlp_kernel(x_ref, hf_ref, mod_ref, w1_ref, w2_ref, gpost_ref, xo_ref, h_scr):
    hf = hf_ref[...]
    for j in range(D_FF // PROJ_CHUNK):
        cols = slice(j * PROJ_CHUNK, (j + 1) * PROJ_CHUNK)
        h = jnp.maximum(jnp.dot(hf, w1_ref[:, cols], preferred_element_type=F32), 0.0)
        h_scr[:, cols] = (h * h).astype(BF16)
    y = jnp.dot(h_scr[...], w2_ref[...], preferred_element_type=F32)
    xo_ref[...] = x_ref[...] + mod_ref[0, 0][5:6] * _rms(y, gpost_ref[...])


def _mlp(x, hf, mod, layer, mod_row_fn, w1, w2, g_post, *, seg_len, tm):
    n = x.shape[0]
    tps = seg_len // tm
    row = lambda i: (i, 0)
    return pl.pallas_call(
        _mlp_kernel,
        grid=(n // tm,),
        in_specs=[
            pl.BlockSpec((tm, D_MODEL), row),
            pl.BlockSpec((tm, D_MODEL), row),
            pl.BlockSpec((1, 1, 6, D_MODEL), lambda i: (layer, mod_row_fn(i // tps), 0, 0)),
            _resident(w1.shape), _resident(w2.shape), _resident(g_post.shape),
        ],
        out_specs=pl.BlockSpec((tm, D_MODEL), row),
        out_shape=jax.ShapeDtypeStruct((n, D_MODEL), F32),
        scratch_shapes=[pltpu.VMEM((tm, D_FF), BF16)],
        compiler_params=_cparams(("arbitrary",)),
        name="mlp",
    )(x, hf, mod, w1, w2, g_post)


def _rope_tables(seq):
    rows = seq // GRID_W
    row = jnp.repeat(jnp.arange(rows), GRID_W).astype(F32)
    col = jnp.tile(jnp.arange(GRID_W), rows).astype(F32)
    inv = 1.0 / (ROPE_BASE ** (jnp.arange(ROPE_PAIRS_PER_AXIS, dtype=F32) * 2.0 / ROPE_AXIS_DIM))
    ang = jnp.concatenate([row[:, None] * inv, col[:, None] * inv], axis=-1)
    cos, sin = jnp.cos(ang), jnp.sin(ang)
    cos2 = jnp.repeat(cos, 2, axis=-1)
    sin2 = jnp.stack([-sin, sin], axis=-1).reshape(seq, HEAD_DIM)
    return jnp.tile(cos2, (1, 2)), jnp.tile(sin2, (1, 2))


def kernel(x, c, ctx, c_ctx, w_mod, b_mod, g_pre_mix, g_post_mix, g_pre_ffn, g_post_ffn,
           w_in, w_pool_grp, pool_scale, lambda_q1, lambda_k1, lambda_q2, lambda_k2,
           diff_subln, gqa_q_norm, gqa_k_norm, w_pool_out, w_diff_out, w_gqa_out, w_o,
           w_ff1, w_ff2):
    batch, seq, d = x.shape
    ctx_len = ctx.shape[1]
    depth = w_mod.shape[0]
    assert d == D_MODEL and batch < MOD_ROWS and w_in.shape[-1] == IN_WIDTH

    lat_tm, ctx_tm = 512, ctx_len
    rope_tabs = _rope_tables(seq)
    c_all = jnp.zeros((MOD_ROWS, d), F32).at[:batch].set(c).at[batch].set(c_ctx)
    mod = _modulation(c_all, w_mod, b_mod)
    lat_row = lambda s: s
    ctx_row = lambda s: batch

    xl = x.reshape(batch * seq, d)
    xc = ctx.reshape(batch * ctx_len, d)
    row = lambda a: a.reshape(1, -1)
    twice = lambda a: jnp.tile(a.reshape(1, -1), (1, 2))

    for l in range(depth):
        last = l == depth - 1
        lam_init = 0.8 - 0.6 * math.exp(-0.3 * l)
        w_in_b = w_in[l].astype(BF16)
        qg2, kg2 = twice(gqa_q_norm[l]), twice(gqa_k_norm[l])
        lam_params = jnp.stack([lambda_q1[l], lambda_k1[l], lambda_q2[l], lambda_k2[l]])
        merge_w = (w_pool_grp[l].astype(BF16), row(pool_scale[l]), w_pool_out[l].astype(BF16),
                   w_diff_out[l].astype(BF16), w_gqa_out[l].astype(BF16), w_o[l].astype(BF16),
                   row(g_post_mix[l]), row(g_pre_ffn[l]))
        mlp_w = (w_ff1[l].astype(BF16), w_ff2[l].astype(BF16), row(g_post_ffn[l]))

        cp = _in_proj(xc, mod, l, ctx_row, row(g_pre_mix[l]), w_in_b, qg2, kg2, None,
                      seg_len=ctx_len, tm=ctx_tm, q_side=not last)
        c_dkt, c_dv, c_gkt, c_gv = cp[-4:]
        pool_in, dq, gq, gates, l_dkt, l_dv, l_gkt, l_gv = _in_proj(
            xl, mod, l, lat_row, row(g_pre_mix[l]), w_in_b, qg2, kg2, rope_tabs,
            seg_len=seq, tm=lat_tm, q_side=True)

        diff_out = _diff_attn(dq, (c_dkt, l_dkt), (c_dv, l_dv), lam_params,
                              row(diff_subln[l]), lam_init, tq=512)
        gqa_out = _gqa_attn(gq, (c_gkt, l_gkt), (c_gv, l_gv), tq=256)
        xl_new, hf = _merge(xl, pool_in, diff_out, gqa_out, gates, mod, l, lat_row, *merge_w,
                            seg_len=seq, tm=lat_tm)
        xl_next = _mlp(xl_new, hf, mod, l, lat_row, *mlp_w, seg_len=seq, tm=lat_tm)

        if not last:
            c_pool, c_dq, c_gq, c_gates = cp[:4]
            c_diff = _diff_attn(c_dq, (c_dkt,), (c_dv,), lam_params, row(diff_subln[l]),
                                lam_init, tq=ctx_len)
            c_gqa = _gqa_attn(c_gq, (c_gkt,), (c_gv,), tq=ctx_len)
            xc_new, c_hf = _merge(xc, c_pool, c_diff, c_gqa, c_gates, mod, l, ctx_row, *merge_w,
                                  seg_len=ctx_len, tm=ctx_tm)
            xc = _mlp(xc_new, c_hf, mod, l, ctx_row, *mlp_w, seg_len=ctx_len, tm=ctx_tm)
        xl = xl_next
    return xl.reshape(batch, seq, d)
```

```python
import functools
import math

import numpy as np
import jax
import jax.numpy as jnp
from jax import lax
from jax.experimental import pallas as pl
from jax.experimental.pallas import tpu as pltpu

F32 = jnp.float32
BF16 = jnp.bfloat16

D_MODEL = 1024
GRID_W = 64
HEAD_DIM = 64
HALF_HEAD = HEAD_DIM // 2
ATTN_SCALE = HEAD_DIM ** -0.5
LOG2E = math.log2(math.e)
ROPE_BASE = 10000.0
ROPE_AXIS_DIM = HEAD_DIM // 2
ROPE_PAIRS_PER_AXIS = ROPE_AXIS_DIM // 2
EPS = 1e-6

POOL_WINDOWS = (2, 4, 8, 16)
POOL_WIDTH = D_MODEL // 2
POOL_GROUP_DIM = POOL_WIDTH // len(POOL_WINDOWS)
POOL_HALO = max(POOL_WINDOWS) // 2

DIFF_HEADS = D_MODEL // 256
DIFF_WIDTH = DIFF_HEADS * 2 * HEAD_DIM
GQA_HEADS = D_MODEL // 128
GQA_KV_HEADS = GQA_HEADS // 4
GQA_GROUP = GQA_HEADS // GQA_KV_HEADS
GQA_WIDTH = GQA_HEADS * HEAD_DIM
GQA_KV_WIDTH = GQA_KV_HEADS * HEAD_DIM
N_BRANCH = 3
D_FF = 4 * D_MODEL

OFF_POOL = 0
OFF_DQ = OFF_POOL + POOL_WIDTH
OFF_GQ = OFF_DQ + DIFF_WIDTH
OFF_GATE = OFF_GQ + GQA_WIDTH
OFF_DK = OFF_GATE + N_BRANCH * D_MODEL
OFF_DV = OFF_DK + DIFF_WIDTH
OFF_GK = OFF_DV + DIFF_WIDTH
OFF_GV = OFF_GK + GQA_KV_WIDTH
IN_WIDTH = OFF_GV + GQA_KV_WIDTH

LANES = 128
MXU_DIM = 256
PANEL = 2 * MXU_DIM
BF16_ROWS = 16
PROJ_CHUNK = 512
KEY_BLOCK = 256
VMEM_LIMIT = 56 * 1024 * 1024

MOD_ROWS = 8


def _cparams(sem):
    return pltpu.CompilerParams(dimension_semantics=sem, vmem_limit_bytes=VMEM_LIMIT)


def _resident(shape):
    nd = len(shape)
    return pl.BlockSpec(shape, lambda *_: (0,) * nd, pipeline_mode=pl.Buffered(1))


def _rms(x, g):
    ms = jnp.mean(x * x, axis=-1, keepdims=True)
    return x * lax.rsqrt(ms + EPS) * g


def _mod_kernel(c_ref, w_ref, b_ref, o_ref):
    c = c_ref[...]
    a = c * (1.0 / (1.0 + jnp.exp(-c)))
    o_ref[0] = jnp.dot(a.astype(BF16), w_ref[0].astype(BF16),
                       preferred_element_type=F32) + b_ref[0]


def _modulation(c_all, w_mod, b_mod):
    depth = w_mod.shape[0]
    nblk = w_mod.shape[2] // D_MODEL
    out = pl.pallas_call(
        _mod_kernel,
        grid=(depth, nblk),
        in_specs=[
            pl.BlockSpec((MOD_ROWS, D_MODEL), lambda l, j: (0, 0)),
            pl.BlockSpec((1, D_MODEL, D_MODEL), lambda l, j: (l, 0, j)),
            pl.BlockSpec((1, 1, D_MODEL), lambda l, j: (l, 0, j)),
        ],
        out_specs=pl.BlockSpec((1, MOD_ROWS, D_MODEL), lambda l, j: (l, 0, j)),
        out_shape=jax.ShapeDtypeStruct((depth, MOD_ROWS, nblk * D_MODEL), F32),
        compiler_params=_cparams(("arbitrary", "arbitrary")),
        name="modulation",
    )(c_all, w_mod, b_mod.reshape(depth, 1, nblk * D_MODEL))
    return out.reshape(depth, MOD_ROWS, nblk, D_MODEL)


def _rope(x, cos, sin_signed):
    return x * cos + pltpu.roll(x, HEAD_DIM, 1) * sin_signed


def _head_norm(x, gain2):
    lane = lax.broadcasted_iota(jnp.int32, x.shape, 1)
    first = (lane & HALF_HEAD) == 0
    x2 = x * x
    ms_a = jnp.sum(jnp.where(first, x2, 0.0), axis=-1, keepdims=True) * (1.0 / HEAD_DIM)
    ms_b = jnp.sum(jnp.where(first, 0.0, x2), axis=-1, keepdims=True) * (1.0 / HEAD_DIM)
    inv = jnp.where(first, lax.rsqrt(ms_a + EPS), lax.rsqrt(ms_b + EPS))
    return x * inv * gain2


def _in_proj_kernel(*refs, rope, q_side):
    it = iter(refs)
    x_ref, mod_ref, gpre_ref, w_ref, qg_ref, kg_ref = (next(it) for _ in range(6))
    cos_ref = next(it) if rope else None
    sin_ref = next(it) if rope else None
    if q_side:
        pool_ref, dqt_ref, gqt_ref, gate_ref = (next(it) for _ in range(4))
    dk_ref, dvt_ref, gk_ref, gvt_ref = (next(it) for _ in range(4))

    x = x_ref[...]
    mod = mod_ref[0, 0]
    h = _rms(x, gpre_ref[...]) * (1.0 + mod[1:2]) + mod[0:1]
    hb = h.astype(BF16)
    if rope:
        cos, sin = cos_ref[...], sin_ref[...]

    def proj(off, width):
        return jnp.dot(hb, w_ref[:, off:off + width], preferred_element_type=F32)

    def blocks(off, width):
        y = proj(off, width)
        return [y[:, j * LANES:(j + 1) * LANES] for j in range(width // LANES)]

    def store_heads_t(ref_a, ref_b, y):
        yt = y.T
        row = lax.broadcasted_iota(jnp.int32, yt.shape, 0)
        first = (row & HALF_HEAD) == 0
        ref_a[...] = jnp.where(first, yt, 0.0).astype(BF16)
        ref_b[...] = jnp.where(first, 0.0, yt).astype(BF16)

    for hh, y in enumerate(blocks(OFF_DK, DIFF_WIDTH)):
        if rope:
            y = _rope(y, cos, sin)
        dk_ref[:, hh * LANES:(hh + 1) * LANES] = y.astype(BF16)
    (y,) = blocks(OFF_GK, GQA_KV_WIDTH)
    y = _head_norm(y, kg_ref[...])
    if rope:
        y = _rope(y, cos, sin)
    gk_ref[...] = y.astype(BF16)

    if q_side:
        q_scale = ATTN_SCALE * LOG2E
        for hh, y in enumerate(blocks(OFF_DQ, DIFF_WIDTH)):
            if rope:
                y = _rope(y, cos, sin)
            store_heads_t(dqt_ref.at[0, hh, 0], dqt_ref.at[0, hh, 1], y * q_scale)
        for jj, y in enumerate(blocks(OFF_GQ, GQA_WIDTH)):
            y = _head_norm(y, qg_ref[...])
            if rope:
                y = _rope(y, cos, sin)
            store_heads_t(gqt_ref.at[0, 0, jj], gqt_ref.at[0, 1, jj], y * q_scale)

    for hh, y in enumerate(blocks(OFF_DV, DIFF_WIDTH)):
        dvt_ref[0, hh] = y.T.astype(BF16)
    (y,) = blocks(OFF_GV, GQA_KV_WIDTH)
    gvt_ref[0] = y.T.astype(BF16)

    if q_side:
        pool_ref[...] = proj(OFF_POOL, POOL_WIDTH)
        for j in range(N_BRANCH * D_MODEL // PROJ_CHUNK):
            gate_ref[:, j * PROJ_CHUNK:(j + 1) * PROJ_CHUNK] = proj(
                OFF_GATE + j * PROJ_CHUNK, PROJ_CHUNK)


def _in_proj(x, mod, layer, mod_row_fn, g_pre, w_in, q_gain2, k_gain2, rope_tabs, *,
             seg_len, tm, q_side):
    n = x.shape[0]
    n_seg = n // seg_len
    tps = seg_len // tm
    rope = rope_tabs is not None
    seg = lambda i: i // tps
    pos = lambda i: i % tps
    row = lambda i: (i, 0)

    in_specs = [
        pl.BlockSpec((tm, D_MODEL), row),
        pl.BlockSpec((1, 1, 6, D_MODEL), lambda i: (layer, mod_row_fn(seg(i)), 0, 0)),
        _resident((1, D_MODEL)),
        _resident(w_in.shape),
        _resident((1, LANES)),
        _resident((1, LANES)),
    ]
    args = [x, mod, g_pre, w_in, q_gain2, k_gain2]
    if rope:
        in_specs += [pl.BlockSpec((tm, LANES), lambda i: (pos(i), 0))] * 2
        args += list(rope_tabs)

    out_shape, out_specs = [], []
    if q_side:
        out_shape += [
            jax.ShapeDtypeStruct((n, POOL_WIDTH), F32),
            jax.ShapeDtypeStruct((n_seg, DIFF_HEADS, 2, LANES, seg_len), BF16),
            jax.ShapeDtypeStruct((n_seg, GQA_KV_HEADS, GQA_GROUP, LANES, seg_len), BF16),
            jax.ShapeDtypeStruct((n, N_BRANCH * D_MODEL), F32),
        ]
        out_specs += [
            pl.BlockSpec((tm, POOL_WIDTH), row),
            pl.BlockSpec((1, DIFF_HEADS, 2, LANES, tm), lambda i: (seg(i), 0, 0, 0, pos(i))),
            pl.BlockSpec((1, GQA_KV_HEADS, GQA_GROUP, LANES, tm),
                         lambda i: (seg(i), 0, 0, 0, pos(i))),
            pl.BlockSpec((tm, N_BRANCH * D_MODEL), row),
        ]
    out_shape += [
        jax.ShapeDtypeStruct((n, DIFF_WIDTH), BF16),
        jax.ShapeDtypeStruct((n_seg, DIFF_HEADS, 2 * HEAD_DIM, seg_len), BF16),
        jax.ShapeDtypeStruct((n, GQA_KV_WIDTH), BF16),
        jax.ShapeDtypeStruct((n_seg, GQA_KV_WIDTH, seg_len), BF16),
    ]
    out_specs += [
        pl.BlockSpec((tm, DIFF_WIDTH), row),
        pl.BlockSpec((1, DIFF_HEADS, 2 * HEAD_DIM, tm), lambda i: (seg(i), 0, 0, pos(i))),
        pl.BlockSpec((tm, GQA_KV_WIDTH), row),
        pl.BlockSpec((1, GQA_KV_WIDTH, tm), lambda i: (seg(i), 0, pos(i))),
    ]
    return pl.pallas_call(
        functools.partial(_in_proj_kernel, rope=rope, q_side=q_side),
        grid=(n // tm,),
        in_specs=in_specs,
        out_specs=out_specs,
        out_shape=out_shape,
        compiler_params=_cparams(("arbitrary",)),
        name="in_proj" + ("_rope" if rope else "") + ("" if q_side else "_kv"),
    )(*args)


def _fill_values_t(vt_scr, vt_refs, rows):
    c = 0
    for vt_ref in vt_refs:
        vt = vt_ref[...]
        vt = vt.reshape(vt.shape[-2], vt.shape[-1])
        vt_scr[0:rows, c:c + vt.shape[1]] = vt
        c += vt.shape[1]
    vt_scr[rows:, :] = jnp.ones((vt_scr.shape[0] - rows, vt_scr.shape[1]), BF16)


def _attn_scratch(nk, pw, v_rows, per_out):
    return [
        pltpu.VMEM((2, nk, pw), F32),
        pltpu.VMEM((2, 8, pw), F32),
        pltpu.VMEM((per_out, v_rows, pw), F32),
        pltpu.VMEM((v_rows + BF16_ROWS, nk), BF16),
    ]


def _attn_pipeline(qt_ref, k_refs, vt_refs, s_scr, m_scr, on_scr, vt_scr, *, n_panels, emit):
    p = pl.program_id(2)
    per_out, v_rows = on_scr.shape[0], on_scr.shape[1]
    nk = s_scr.shape[1]

    @pl.when(p == 0)
    def _():
        _fill_values_t(vt_scr, vt_refs, v_rows)

    key_chunks = [(k_ref, c0) for k_ref in k_refs for c0 in range(0, k_ref.shape[0], MXU_DIM)]

    def phase(score_buf, value_buf, slot):
        if score_buf is not None:
            qt = qt_ref[...].reshape(qt_ref.shape[-2:])
        if value_buf is not None:
            m_prev = m_scr[value_buf][0:1]
        m, acc = None, None
        for c, (k_ref, c0) in enumerate(key_chunks):
            rows = slice(c * MXU_DIM, (c + 1) * MXU_DIM)
            if value_buf is not None:
                pt = jnp.exp2(s_scr[value_buf, rows, :] - m_prev).astype(BF16)
                d = jnp.dot(vt_scr[:, rows], pt, preferred_element_type=F32)
                acc = d if acc is None else acc + d
            if score_buf is not None:
                s = jnp.dot(k_ref[c0:c0 + MXU_DIM, :], qt, preferred_element_type=F32)
                s_scr[score_buf, rows, :] = s
                mc = jnp.max(s, axis=0, keepdims=True)
                m = mc if m is None else jnp.maximum(m, mc)
        if score_buf is not None:
            m_scr[score_buf] = jnp.broadcast_to(m, m_scr.shape[1:])
        if value_buf is not None:
            on_scr[slot] = acc[0:v_rows] / acc[v_rows:v_rows + 1]

    prev_slot = (p - 1) & (per_out - 1)

    @pl.when(p == 0)
    def _():
        phase(0, None, None)

    for parity in range(2):
        @pl.when((p > 0) & (p < n_panels) & ((p & 1) == parity))
        def _():
            phase(parity, 1 - parity, prev_slot)

    @pl.when(p == n_panels)
    def _():
        phase(None, (n_panels - 1) % 2, (n_panels - 1) % per_out)

    @pl.when((p > 0) & (prev_slot == per_out - 1))
    def _():
        emit()


def _attn_index_maps(per_out, n_panels, tiles_per_seg):
    def panel(p):
        q = jnp.minimum(p, n_panels - 1)
        return q % per_out, q // per_out

    def out_tile(b, p):
        return b * tiles_per_seg + jnp.maximum(p - 1, 0) // per_out

    return panel, out_tile


def _diff_attn_kernel(*refs, n_seg, n_panels, lam_init):
    it = iter(refs)
    lam_ref, subg_ref, qt_ref = next(it), next(it), next(it)
    k_refs = [next(it) for _ in range(n_seg)]
    vt_refs = [next(it) for _ in range(n_seg)]
    o_ref, s_scr, m_scr, on_scr, vt_scr = (next(it) for _ in range(5))

    def emit():
        lp = lam_ref[...]
        lam = (jnp.exp(jnp.sum(lp[0:1] * lp[1:2], axis=-1, keepdims=True))
               - jnp.exp(jnp.sum(lp[2:3] * lp[3:4], axis=-1, keepdims=True)) + lam_init)
        o = on_scr[0] - lam * on_scr[1]
        ms = jnp.mean(o * o, axis=0, keepdims=True)
        o = o * lax.rsqrt(ms + EPS) * subg_ref[...] * (1.0 - lam_init)
        o_ref[...] = o.T.astype(BF16)

    _attn_pipeline(qt_ref, k_refs, vt_refs, s_scr, m_scr, on_scr, vt_scr,
                   n_panels=n_panels, emit=emit)


def _diff_attn(dqt, ks, vts, lam_params, sub_gain_col, lam_init, *, pw):
    n_b, _, per_out, _, lq = dqt.shape
    lks = tuple(vt.shape[-1] for vt in vts)
    vw = 2 * HEAD_DIM
    n_panels = per_out * (lq // pw)
    panel, out_tile = _attn_index_maps(per_out, n_panels, lq // pw)
    in_specs = [
        _resident(lam_params.shape),
        _resident(sub_gain_col.shape),
        pl.BlockSpec((1, 1, 1, LANES, pw), lambda b, h, p: (b, h, panel(p)[0], 0, panel(p)[1])),
    ]
    in_specs += [pl.BlockSpec((lk, LANES), lambda b, h, p: (b, h)) for lk in lks]
    in_specs += [pl.BlockSpec((1, 1, vw, lk), lambda b, h, p: (b, h, 0, 0)) for lk in lks]
    return pl.pallas_call(
        functools.partial(_diff_attn_kernel, n_seg=len(lks), n_panels=n_panels,
                          lam_init=lam_init),
        grid=(n_b, DIFF_HEADS, n_panels + 1),
        in_specs=in_specs,
        out_specs=pl.BlockSpec((pw, vw), lambda b, h, p: (out_tile(b, p), h)),
        out_shape=jax.ShapeDtypeStruct((n_b * lq, DIFF_WIDTH), BF16),
        scratch_shapes=_attn_scratch(sum(lks), pw, vw, per_out),
        compiler_params=_cparams(("arbitrary", "arbitrary", "arbitrary")),
        name="diff_attn",
    )(lam_params, sub_gain_col, dqt, *ks, *vts)


def _gqa_attn_kernel(*refs, n_seg, n_panels):
    it = iter(refs)
    qt_ref = next(it)
    k_refs = [next(it) for _ in range(n_seg)]
    vt_refs = [next(it) for _ in range(n_seg)]
    o_ref, s_scr, m_scr, on_scr, vt_scr = (next(it) for _ in range(5))

    def emit():
        pairs = [jnp.concatenate([on_scr[2 * i], on_scr[2 * i + 1]], axis=0).T
                 for i in range(GQA_GROUP // 2)]
        o_ref[...] = jnp.concatenate(pairs, axis=1).astype(BF16)

    _attn_pipeline(qt_ref, k_refs, vt_refs, s_scr, m_scr, on_scr, vt_scr,
                   n_panels=n_panels, emit=emit)


def _gqa_attn(gqt, ks, vts, *, pw):
    n_b, _, per_out, _, lq = gqt.shape
    lks = tuple(vt.shape[-1] for vt in vts)
    n_panels = per_out * (lq // pw)
    panel, out_tile = _attn_index_maps(per_out, n_panels, lq // pw)
    in_specs = [pl.BlockSpec((1, 1, 1, LANES, pw),
                             lambda b, g, p: (b, g, panel(p)[0], 0, panel(p)[1]))]
    in_specs += [pl.BlockSpec((lk, LANES), lambda b, g, p: (b, 0)) for lk in lks]
    in_specs += [pl.BlockSpec((1, HEAD_DIM, lk), lambda b, g, p: (b, g, 0)) for lk in lks]
    return pl.pallas_call(
        functools.partial(_gqa_attn_kernel, n_seg=len(lks), n_panels=n_panels),
        grid=(n_b, GQA_KV_HEADS, n_panels + 1),
        in_specs=in_specs,
        out_specs=pl.BlockSpec((pw, GQA_GROUP * HEAD_DIM), lambda b, g, p: (out_tile(b, p), g)),
        out_shape=jax.ShapeDtypeStruct((n_b * lq, GQA_WIDTH), BF16),
        scratch_shapes=_attn_scratch(sum(lks), pw, HEAD_DIM, per_out),
        compiler_params=_cparams(("arbitrary", "arbitrary", "arbitrary")),
        name="gqa_attn",
    )(gqt, *ks, *vts)


def _merge_kernel(x_ref, zp_ref, z_ref, zn_ref, do_ref, go_ref, gate_ref, mod_ref,
                  wgrp_ref, pscale_ref, wpo_ref, wdo_ref, wgo_ref, wo_ref,
                  gpost_ref, gpre_ref, xo_ref, hf_ref, zext_scr, *, seg_len, tm):
    i = pl.program_id(0)
    tps = seg_len // tm
    t = i % tps
    z = z_ref[...]
    zext_scr[0:POOL_HALO] = jnp.where(t > 0, zp_ref[...], 0.0)
    zext_scr[POOL_HALO:POOL_HALO + tm] = z
    zext_scr[POOL_HALO + tm:] = jnp.where(t < tps - 1, zn_ref[...], 0.0)
    pos = t * tm + lax.broadcasted_iota(jnp.int32, (tm, POOL_GROUP_DIM), 0)
    mixed = []
    for g, w in enumerate(POOL_WINDOWS):
        c0 = g * POOL_GROUP_DIM
        ssum = None
        for d in range(-(w // 2), w // 2):
            zz = zext_scr[POOL_HALO + d:POOL_HALO + d + tm, c0:c0 + POOL_GROUP_DIM]
            ssum = zz if ssum is None else ssum + zz
        cnt = (jnp.minimum(pos + w // 2, seg_len) - jnp.maximum(pos - w // 2, 0)).astype(F32)
        pooled = ssum / cnt - z[:, c0:c0 + POOL_GROUP_DIM]
        mixed.append(jnp.dot(pooled.astype(BF16), wgrp_ref[g], preferred_element_type=F32))
    pool_out = jnp.concatenate(mixed, axis=1) * pscale_ref[...]

    def gate(k):
        gk = gate_ref[:, k * D_MODEL:(k + 1) * D_MODEL]
        return 1.0 / (1.0 + jnp.exp(-gk))

    merged = gate(0) * jnp.dot(pool_out.astype(BF16), wpo_ref[...], preferred_element_type=F32)
    merged += gate(1) * jnp.dot(do_ref[...], wdo_ref[...], preferred_element_type=F32)
    merged += gate(2) * jnp.dot(go_ref[...], wgo_ref[...], preferred_element_type=F32)
    y = jnp.dot(merged.astype(BF16), wo_ref[...], preferred_element_type=F32)
    mod = mod_ref[0, 0]
    xn = x_ref[...] + mod[2:3] * _rms(y, gpost_ref[...])
    xo_ref[...] = xn
    hf_ref[...] = (_rms(xn, gpre_ref[...]) * (1.0 + mod[4:5]) + mod[3:4]).astype(BF16)


def _merge(x, pool_in, diff_out, gqa_out, gates, mod, layer, mod_row_fn, w_grp, pool_scale,
           w_pool_out, w_diff_out, w_gqa_out, w_o, g_post, g_pre_ffn, *, seg_len, tm):
    n = x.shape[0]
    tps = seg_len // tm
    hb = tm // POOL_HALO
    n_hblk = n // POOL_HALO
    row = lambda i: (i, 0)
    in_specs = [
        pl.BlockSpec((tm, D_MODEL), row),
        pl.BlockSpec((POOL_HALO, POOL_WIDTH), lambda i: (jnp.maximum(i * hb - 1, 0), 0)),
        pl.BlockSpec((tm, POOL_WIDTH), row),
        pl.BlockSpec((POOL_HALO, POOL_WIDTH), lambda i: (jnp.minimum((i + 1) * hb, n_hblk - 1), 0)),
        pl.BlockSpec((tm, DIFF_WIDTH), row),
        pl.BlockSpec((tm, GQA_WIDTH), row),
        pl.BlockSpec((tm, N_BRANCH * D_MODEL), row),
        pl.BlockSpec((1, 1, 6, D_MODEL), lambda i: (layer, mod_row_fn(i // tps), 0, 0)),
        _resident(w_grp.shape), _resident(pool_scale.shape), _resident(w_pool_out.shape),
        _resident(w_diff_out.shape), _resident(w_gqa_out.shape), _resident(w_o.shape),
        _resident(g_post.shape), _resident(g_pre_ffn.shape),
    ]
    return pl.pallas_call(
        functools.partial(_merge_kernel, seg_len=seg_len, tm=tm),
        grid=(n // tm,),
        in_specs=in_specs,
        out_specs=[pl.BlockSpec((tm, D_MODEL), row), pl.BlockSpec((tm, D_MODEL), row)],
        out_shape=[jax.ShapeDtypeStruct((n, D_MODEL), F32),
                   jax.ShapeDtypeStruct((n, D_MODEL), BF16)],
        scratch_shapes=[pltpu.VMEM((tm + 2 * POOL_HALO, POOL_WIDTH), F32)],
        compiler_params=_cparams(("arbitrary",)),
        name="merge",
    )(x, pool_in, pool_in, pool_in, diff_out, gqa_out, gates, mod, w_grp, pool_scale,
      w_pool_out, w_diff_out, w_gqa_out, w_o, g_post, g_pre_ffn)


def _mlp_kernel(x_ref, hf_ref, mod_ref, w1_ref, w2_ref, gpost_ref, xo_ref, h_scr):
    hf = hf_ref[...]
    for j in range(D_FF // PROJ_CHUNK):
        cols = slice(j * PROJ_CHUNK, (j + 1) * PROJ_CHUNK)
        h = jnp.maximum(jnp.dot(hf, w1_ref[:, cols], preferred_element_type=F32), 0.0)
        h_scr[:, cols] = (h * h).astype(BF16)
    y = jnp.dot(h_scr[...], w2_ref[...], preferred_element_type=F32)
    xo_ref[...] = x_ref[...] + mod_ref[0, 0][5:6] * _rms(y, gpost_ref[...])


def _mlp(x, hf, mod, layer, mod_row_fn, w1, w2, g_post, *, seg_len, tm):
    n = x.shape[0]
    tps = seg_len // tm
    row = lambda i: (i, 0)
    return pl.pallas_call(
        _mlp_kernel,
        grid=(n // tm,),
        in_specs=[
            pl.BlockSpec((tm, D_MODEL), row),
            pl.BlockSpec((tm, D_MODEL), row),
            pl.BlockSpec((1, 1, 6, D_MODEL), lambda i: (layer, mod_row_fn(i // tps), 0, 0)),
            _resident(w1.shape), _resident(w2.shape), _resident(g_post.shape),
        ],
        out_specs=pl.BlockSpec((tm, D_MODEL), row),
        out_shape=jax.ShapeDtypeStruct((n, D_MODEL), F32),
        scratch_shapes=[pltpu.VMEM((tm, D_FF), BF16)],
        compiler_params=_cparams(("arbitrary",)),
        name="mlp",
    )(x, hf, mod, w1, w2, g_post)


def _rope_tables(seq):
    rows = seq // GRID_W
    row = jnp.repeat(jnp.arange(rows), GRID_W).astype(F32)
    col = jnp.tile(jnp.arange(GRID_W), rows).astype(F32)
    inv = 1.0 / (ROPE_BASE ** (jnp.arange(ROPE_PAIRS_PER_AXIS, dtype=F32) * 2.0 / ROPE_AXIS_DIM))
    ang = jnp.concatenate([row[:, None] * inv, col[:, None] * inv], axis=-1)
    cos, sin = jnp.cos(ang), jnp.sin(ang)
    return (jnp.tile(cos, (1, 4)),
            jnp.concatenate([-sin, -sin, sin, sin], axis=-1))


def _pair_block(base_a, base_b):
    ev, od = np.arange(0, HEAD_DIM, 2), np.arange(1, HEAD_DIM, 2)
    return np.concatenate([base_a + ev, base_b + ev, base_a + od, base_b + od])


def _in_proj_columns():
    idx = np.arange(IN_WIDTH)
    for off in (OFF_DQ, OFF_DK):
        for h in range(DIFF_HEADS):
            b = off + h * LANES
            idx[b:b + LANES] = _pair_block(b, b + HEAD_DIM)
    for j in range(GQA_GROUP):
        b = OFF_GQ + j * LANES
        idx[b:b + LANES] = _pair_block(OFF_GQ + j * HEAD_DIM, OFF_GQ + (GQA_GROUP + j) * HEAD_DIM)
    idx[OFF_GK:OFF_GK + LANES] = _pair_block(OFF_GK, OFF_GK + HEAD_DIM)
    return idx


def _pair_gain(g):
    return jnp.concatenate([g[0::2], g[0::2], g[1::2], g[1::2]]).reshape(1, LANES)


def kernel(x, c, ctx, c_ctx, w_mod, b_mod, g_pre_mix, g_post_mix, g_pre_ffn, g_post_ffn,
           w_in, w_pool_grp, pool_scale, lambda_q1, lambda_k1, lambda_q2, lambda_k2,
           diff_subln, gqa_q_norm, gqa_k_norm, w_pool_out, w_diff_out, w_gqa_out, w_o,
           w_ff1, w_ff2):
    batch, seq, d = x.shape
    ctx_len = ctx.shape[1]
    depth = w_mod.shape[0]
    assert d == D_MODEL and batch < MOD_ROWS and w_in.shape[-1] == IN_WIDTH
    assert GQA_KV_HEADS == 2 and ctx_len % MXU_DIM == 0

    lat_tm, ctx_tm = 512, ctx_len
    rope_tabs = _rope_tables(seq)
    in_cols = _in_proj_columns()
    c_all = jnp.zeros((MOD_ROWS, d), F32).at[:batch].set(c).at[batch].set(c_ctx)
    mod = _modulation(c_all, w_mod, b_mod)
    lat_row = lambda s: s
    ctx_row = lambda s: batch

    xl = x.reshape(batch * seq, d)
    xc = ctx.reshape(batch * ctx_len, d)
    row = lambda a: a.reshape(1, -1)

    for l in range(depth):
        last = l == depth - 1
        lam_init = 0.8 - 0.6 * math.exp(-0.3 * l)
        w_in_b = w_in[l][:, in_cols].astype(BF16)
        qg2, kg2 = _pair_gain(gqa_q_norm[l]), _pair_gain(gqa_k_norm[l])
        lam_params = jnp.stack([lambda_q1[l], lambda_k1[l], lambda_q2[l], lambda_k2[l]])
        sub_gain = diff_subln[l].reshape(-1, 1)
        merge_w = (w_pool_grp[l].astype(BF16), row(pool_scale[l]), w_pool_out[l].astype(BF16),
                   w_diff_out[l].astype(BF16), w_gqa_out[l].astype(BF16), w_o[l].astype(BF16),
                   row(g_post_mix[l]), row(g_pre_ffn[l]))
        mlp_w = (w_ff1[l].astype(BF16), w_ff2[l].astype(BF16), row(g_post_ffn[l]))

        cp = _in_proj(xc, mod, l, ctx_row, row(g_pre_mix[l]), w_in_b, qg2, kg2, None,
                      seg_len=ctx_len, tm=ctx_tm, q_side=not last)
        c_dk, c_dvt, c_gk, c_gvt = cp[-4:]
        pool_in, dqt, gqt, gates, l_dk, l_dvt, l_gk, l_gvt = _in_proj(
            xl, mod, l, lat_row, row(g_pre_mix[l]), w_in_b, qg2, kg2, rope_tabs,
            seg_len=seq, tm=lat_tm, q_side=True)

        diff_out = _diff_attn(dqt, (c_dk, l_dk), (c_dvt, l_dvt), lam_params, sub_gain,
                              lam_init, pw=PANEL)
        gqa_out = _gqa_attn(gqt, (c_gk, l_gk), (c_gvt, l_gvt), pw=PANEL)
        xl_new, hf = _merge(xl, pool_in, diff_out, gqa_out, gates, mod, l, lat_row, *merge_w,
                            seg_len=seq, tm=lat_tm)
        xl_next = _mlp(xl_new, hf, mod, l, lat_row, *mlp_w, seg_len=seq, tm=lat_tm)

        if not last:
            c_pool, c_dqt, c_gqt, c_gates = cp[:4]
            c_diff = _diff_attn(c_dqt, (c_dk,), (c_dvt,), lam_params, sub_gain, lam_init,
                                pw=ctx_len)
            c_gqa = _gqa_attn(c_gqt, (c_gk,), (c_gvt,), pw=ctx_len)
            xc_new, c_hf = _merge(xc, c_pool, c_diff, c_gqa, c_gates, mod, l, ctx_row, *merge_w,
                                  seg_len=ctx_len, tm=ctx_tm)
            xc = _mlp(xc_new, c_hf, mod, l, ctx_row, *mlp_w, seg_len=ctx_len, tm=ctx_tm)
        xl = xl_next
    return xl.reshape(batch, seq, d)
```

```python
import functools
import math

import numpy as np
import jax
import jax.numpy as jnp
from jax import lax
from jax.experimental import pallas as pl
from jax.experimental.pallas import tpu as pltpu

F32 = jnp.float32
BF16 = jnp.bfloat16

D_MODEL = 1024
GRID_W = 64
HEAD_DIM = 64
HALF_HEAD = HEAD_DIM // 2
ATTN_SCALE = HEAD_DIM ** -0.5
LOG2E = math.log2(math.e)
ROPE_BASE = 10000.0
ROPE_AXIS_DIM = HEAD_DIM // 2
ROPE_PAIRS_PER_AXIS = ROPE_AXIS_DIM // 2
EPS = 1e-6

POOL_WINDOWS = (2, 4, 8, 16)
POOL_WIDTH = D_MODEL // 2
POOL_GROUP_DIM = POOL_WIDTH // len(POOL_WINDOWS)
POOL_HALO = max(POOL_WINDOWS) // 2

DIFF_HEADS = D_MODEL // 256
DIFF_WIDTH = DIFF_HEADS * 2 * HEAD_DIM
GQA_HEADS = D_MODEL // 128
GQA_KV_HEADS = GQA_HEADS // 4
GQA_GROUP = GQA_HEADS // GQA_KV_HEADS
GQA_WIDTH = GQA_HEADS * HEAD_DIM
GQA_KV_WIDTH = GQA_KV_HEADS * HEAD_DIM
N_BRANCH = 3
D_FF = 4 * D_MODEL

OFF_POOL = 0
OFF_DQ = OFF_POOL + POOL_WIDTH
OFF_GQ = OFF_DQ + DIFF_WIDTH
OFF_GATE = OFF_GQ + GQA_WIDTH
OFF_DK = OFF_GATE + N_BRANCH * D_MODEL
OFF_DV = OFF_DK + DIFF_WIDTH
OFF_GK = OFF_DV + DIFF_WIDTH
OFF_GV = OFF_GK + GQA_KV_WIDTH
IN_WIDTH = OFF_GV + GQA_KV_WIDTH

LANES = 128
MXU_DIM = 256
PANEL = 4 * MXU_DIM
BF16_ROWS = 16
PROJ_CHUNK = 512
KEY_BLOCK = 256
VMEM_LIMIT = 56 * 1024 * 1024

MOD_ROWS = 8


def _cparams(sem):
    return pltpu.CompilerParams(dimension_semantics=sem, vmem_limit_bytes=VMEM_LIMIT)


def _resident(shape):
    nd = len(shape)
    return pl.BlockSpec(shape, lambda *_: (0,) * nd, pipeline_mode=pl.Buffered(1))


def _rms(x, g):
    ms = jnp.mean(x * x, axis=-1, keepdims=True)
    return x * lax.rsqrt(ms + EPS) * g


def _mod_kernel(c_ref, w_ref, b_ref, o_ref):
    c = c_ref[...]
    a = c * (1.0 / (1.0 + jnp.exp(-c)))
    o_ref[0] = jnp.dot(a.astype(BF16), w_ref[0].astype(BF16),
                       preferred_element_type=F32) + b_ref[0]


def _modulation(c_all, w_mod, b_mod):
    depth = w_mod.shape[0]
    nblk = w_mod.shape[2] // D_MODEL
    out = pl.pallas_call(
        _mod_kernel,
        grid=(depth, nblk),
        in_specs=[
            pl.BlockSpec((MOD_ROWS, D_MODEL), lambda l, j: (0, 0)),
            pl.BlockSpec((1, D_MODEL, D_MODEL), lambda l, j: (l, 0, j)),
            pl.BlockSpec((1, 1, D_MODEL), lambda l, j: (l, 0, j)),
        ],
        out_specs=pl.BlockSpec((1, MOD_ROWS, D_MODEL), lambda l, j: (l, 0, j)),
        out_shape=jax.ShapeDtypeStruct((depth, MOD_ROWS, nblk * D_MODEL), F32),
        compiler_params=_cparams(("arbitrary", "arbitrary")),
        name="modulation",
    )(c_all, w_mod, b_mod.reshape(depth, 1, nblk * D_MODEL))
    return out.reshape(depth, MOD_ROWS, nblk, D_MODEL)


def _rope(x, cos, sin_signed):
    return x * cos + pltpu.roll(x, HEAD_DIM, 1) * sin_signed


def _head_norm(x, gain2):
    lane = lax.broadcasted_iota(jnp.int32, x.shape, 1)
    first = (lane & HALF_HEAD) == 0
    x2 = x * x
    ms_a = jnp.sum(jnp.where(first, x2, 0.0), axis=-1, keepdims=True) * (1.0 / HEAD_DIM)
    ms_b = jnp.sum(jnp.where(first, 0.0, x2), axis=-1, keepdims=True) * (1.0 / HEAD_DIM)
    inv = jnp.where(first, lax.rsqrt(ms_a + EPS), lax.rsqrt(ms_b + EPS))
    return x * inv * gain2


def _in_proj_kernel(*refs, rope, q_side):
    it = iter(refs)
    x_ref, mod_ref, gpre_ref, w_ref, qg_ref, kg_ref = (next(it) for _ in range(6))
    cos_ref = next(it) if rope else None
    sin_ref = next(it) if rope else None
    if q_side:
        pool_ref, dqt_ref, gqt_ref, gate_ref = (next(it) for _ in range(4))
    dk_ref, dvt_ref, gk_ref, gvt_ref = (next(it) for _ in range(4))

    x = x_ref[...]
    mod = mod_ref[0, 0]
    h = _rms(x, gpre_ref[...]) * (1.0 + mod[1:2]) + mod[0:1]
    hb = h.astype(BF16)
    if rope:
        cos, sin = cos_ref[...], sin_ref[...]

    def proj(off, width):
        return jnp.dot(hb, w_ref[:, off:off + width], preferred_element_type=F32)

    def blocks(off, width):
        y = proj(off, width)
        return [y[:, j * LANES:(j + 1) * LANES] for j in range(width // LANES)]

    def store_heads_t(ref_a, ref_b, y):
        yt = y.T
        row = lax.broadcasted_iota(jnp.int32, yt.shape, 0)
        first = (row & HALF_HEAD) == 0
        ref_a[...] = jnp.where(first, yt, 0.0).astype(BF16)
        ref_b[...] = jnp.where(first, 0.0, yt).astype(BF16)

    for hh, y in enumerate(blocks(OFF_DK, DIFF_WIDTH)):
        if rope:
            y = _rope(y, cos, sin)
        dk_ref[:, hh * LANES:(hh + 1) * LANES] = y.astype(BF16)
    (y,) = blocks(OFF_GK, GQA_KV_WIDTH)
    y = _head_norm(y, kg_ref[...])
    if rope:
        y = _rope(y, cos, sin)
    gk_ref[...] = y.astype(BF16)

    if q_side:
        q_scale = ATTN_SCALE * LOG2E
        for hh, y in enumerate(blocks(OFF_DQ, DIFF_WIDTH)):
            if rope:
                y = _rope(y, cos, sin)
            store_heads_t(dqt_ref.at[0, hh, 0], dqt_ref.at[0, hh, 1], y * q_scale)
        for jj, y in enumerate(blocks(OFF_GQ, GQA_WIDTH)):
            y = _head_norm(y, qg_ref[...])
            if rope:
                y = _rope(y, cos, sin)
            store_heads_t(gqt_ref.at[0, 0, jj], gqt_ref.at[0, 1, jj], y * q_scale)

    for hh, y in enumerate(blocks(OFF_DV, DIFF_WIDTH)):
        dvt_ref[0, hh] = y.T.astype(BF16)
    (y,) = blocks(OFF_GV, GQA_KV_WIDTH)
    gvt_ref[0] = y.T.astype(BF16)

    if q_side:
        pool_ref[...] = proj(OFF_POOL, POOL_WIDTH)
        for j in range(N_BRANCH * D_MODEL // PROJ_CHUNK):
            gate_ref[:, j * PROJ_CHUNK:(j + 1) * PROJ_CHUNK] = proj(
                OFF_GATE + j * PROJ_CHUNK, PROJ_CHUNK)


def _in_proj(x, mod, layer, mod_row_fn, g_pre, w_in, q_gain2, k_gain2, rope_tabs, *,
             seg_len, tm, q_side):
    n = x.shape[0]
    n_seg = n // seg_len
    tps = seg_len // tm
    rope = rope_tabs is not None
    seg = lambda i: i // tps
    pos = lambda i: i % tps
    row = lambda i: (i, 0)

    in_specs = [
        pl.BlockSpec((tm, D_MODEL), row),
        pl.BlockSpec((1, 1, 6, D_MODEL), lambda i: (layer, mod_row_fn(seg(i)), 0, 0)),
        _resident((1, D_MODEL)),
        _resident(w_in.shape),
        _resident((1, LANES)),
        _resident((1, LANES)),
    ]
    args = [x, mod, g_pre, w_in, q_gain2, k_gain2]
    if rope:
        in_specs += [pl.BlockSpec((tm, LANES), lambda i: (pos(i), 0))] * 2
        args += list(rope_tabs)

    out_shape, out_specs = [], []
    if q_side:
        out_shape += [
            jax.ShapeDtypeStruct((n, POOL_WIDTH), F32),
            jax.ShapeDtypeStruct((n_seg, DIFF_HEADS, 2, LANES, seg_len), BF16),
            jax.ShapeDtypeStruct((n_seg, GQA_KV_HEADS, GQA_GROUP, LANES, seg_len), BF16),
            jax.ShapeDtypeStruct((n, N_BRANCH * D_MODEL), F32),
        ]
        out_specs += [
            pl.BlockSpec((tm, POOL_WIDTH), row),
            pl.BlockSpec((1, DIFF_HEADS, 2, LANES, tm), lambda i: (seg(i), 0, 0, 0, pos(i))),
            pl.BlockSpec((1, GQA_KV_HEADS, GQA_GROUP, LANES, tm),
                         lambda i: (seg(i), 0, 0, 0, pos(i))),
            pl.BlockSpec((tm, N_BRANCH * D_MODEL), row),
        ]
    out_shape += [
        jax.ShapeDtypeStruct((n, DIFF_WIDTH), BF16),
        jax.ShapeDtypeStruct((n_seg, DIFF_HEADS, 2 * HEAD_DIM, seg_len), BF16),
        jax.ShapeDtypeStruct((n, GQA_KV_WIDTH), BF16),
        jax.ShapeDtypeStruct((n_seg, GQA_KV_WIDTH, seg_len), BF16),
    ]
    out_specs += [
        pl.BlockSpec((tm, DIFF_WIDTH), row),
        pl.BlockSpec((1, DIFF_HEADS, 2 * HEAD_DIM, tm), lambda i: (seg(i), 0, 0, pos(i))),
        pl.BlockSpec((tm, GQA_KV_WIDTH), row),
        pl.BlockSpec((1, GQA_KV_WIDTH, tm), lambda i: (seg(i), 0, pos(i))),
    ]
    return pl.pallas_call(
        functools.partial(_in_proj_kernel, rope=rope, q_side=q_side),
        grid=(n // tm,),
        in_specs=in_specs,
        out_specs=out_specs,
        out_shape=out_shape,
        compiler_params=_cparams(("arbitrary",)),
        name="in_proj" + ("_rope" if rope else "") + ("" if q_side else "_kv"),
    )(*args)


class _Panels:
    def __init__(self, n_b, n_h, lq, pw, per_out):
        self.n_h, self.per_out, self.tiles = n_h, per_out, lq // pw
        self.per_head = per_out * self.tiles
        self.total = n_b * n_h * self.per_head

    def split(self, q):
        bh, loc = q // self.per_head, q % self.per_head
        return bh // self.n_h, bh % self.n_h, loc % self.per_out, loc // self.per_out

    def loading(self, g):
        return self.split(jnp.minimum(g, self.total - 1))

    def finishing(self, g):
        return self.split(jnp.maximum(g - 1, 0))


def _attn_scratch(nk, pw, v_rows, per_out):
    return [
        pltpu.VMEM((2, nk, pw), F32),
        pltpu.VMEM((2, 8, pw), F32),
        pltpu.VMEM((per_out, v_rows, pw), F32),
        pltpu.VMEM((2, v_rows + BF16_ROWS, nk), BF16),
    ]


def _attn_pipeline(qt_ref, k_refs, vt_refs, s_scr, m_scr, on_scr, vt_scr, *, panels, emit):
    g = pl.program_id(0)
    total, per_head, per_out = panels.total, panels.per_head, panels.per_out
    q_load = jnp.minimum(g, total - 1)
    q_done = jnp.maximum(g - 1, 0)

    v_rows = on_scr.shape[1]

    @pl.when((g < total) & (q_load % per_head == 0))
    def _():
        dst = vt_scr.at[(q_load // per_head) & 1]
        c = 0
        for vt_ref in vt_refs:
            vt = vt_ref[...]
            dst[0:v_rows, c:c + vt.shape[-1]] = vt.reshape(vt.shape[-2:])
            c += vt.shape[-1]
        dst[v_rows:, :] = jnp.ones((dst.shape[0] - v_rows, dst.shape[1]), BF16)

    key_chunks = [(k_ref, c0) for k_ref in k_refs for c0 in range(0, k_ref.shape[0], MXU_DIM)]

    def phase(score_buf, value_buf):
        if score_buf is not None:
            qt = qt_ref[...].reshape(qt_ref.shape[-2:])
        if value_buf is not None:
            m_prev = m_scr[value_buf][0:1]
            vt = vt_scr.at[(q_done // per_head) & 1]
        m, acc = None, None
        for c, (k_ref, c0) in enumerate(key_chunks):
            rows = slice(c * MXU_DIM, (c + 1) * MXU_DIM)
            if value_buf is not None:
                pt = jnp.exp2(s_scr[value_buf, rows, :] - m_prev).astype(BF16)
                d = jnp.dot(vt[:, rows], pt, preferred_element_type=F32)
                acc = d if acc is None else acc + d
            if score_buf is not None:
                s = jnp.dot(k_ref[c0:c0 + MXU_DIM, :], qt, preferred_element_type=F32)
                s_scr[score_buf, rows, :] = s
                mc = jnp.max(s, axis=0, keepdims=True)
                m = mc if m is None else jnp.maximum(m, mc)
        if score_buf is not None:
            m_scr[score_buf] = jnp.broadcast_to(m, m_scr.shape[1:])
        if value_buf is not None:
            on_scr[q_done % per_out] = acc[0:v_rows] / acc[v_rows:v_rows + 1]

    @pl.when(g == 0)
    def _():
        phase(0, None)

    for parity in range(2):
        @pl.when((g > 0) & (g < total) & ((g & 1) == parity))
        def _():
            phase(parity, 1 - parity)

    @pl.when(g == total)
    def _():
        phase(None, (total - 1) % 2)

    @pl.when((g > 0) & (q_done % per_out == per_out - 1))
    def _():
        emit()


def _diff_attn_kernel(*refs, n_seg, panels, lam_init):
    it = iter(refs)
    lam_ref, subg_ref, qt_ref = next(it), next(it), next(it)
    k_refs = [next(it) for _ in range(n_seg)]
    vt_refs = [next(it) for _ in range(n_seg)]
    o_ref, s_scr, m_scr, on_scr, vt_scr = (next(it) for _ in range(5))

    def emit():
        lp = lam_ref[...]
        lam = (jnp.exp(jnp.sum(lp[0:1] * lp[1:2], axis=-1, keepdims=True))
               - jnp.exp(jnp.sum(lp[2:3] * lp[3:4], axis=-1, keepdims=True)) + lam_init)
        o = on_scr[0] - lam * on_scr[1]
        ms = jnp.mean(o * o, axis=0, keepdims=True)
        o = o * lax.rsqrt(ms + EPS) * subg_ref[...] * (1.0 - lam_init)
        o_ref[...] = o.T.astype(BF16)

    _attn_pipeline(qt_ref, k_refs, vt_refs, s_scr, m_scr, on_scr, vt_scr,
                   panels=panels, emit=emit)


def _diff_attn(dqt, ks, vts, lam_params, sub_gain_col, lam_init, *, pw):
    n_b, n_h, per_out, _, lq = dqt.shape
    lks = tuple(vt.shape[-1] for vt in vts)
    vw = 2 * HEAD_DIM
    panels = _Panels(n_b, n_h, lq, pw, per_out)

    def qt_map(g):
        b, h, j, n = panels.loading(g)
        return b, h, j, 0, n

    def k_map(g):
        b, h, _, _ = panels.loading(g)
        return b, h

    def vt_map(g):
        b, h, _, _ = panels.loading(g)
        return b, h, 0, 0

    def out_map(g):
        b, h, _, n = panels.finishing(g)
        return b * panels.tiles + n, h

    in_specs = [
        _resident(lam_params.shape),
        _resident(sub_gain_col.shape),
        pl.BlockSpec((1, 1, 1, LANES, pw), qt_map),
    ]
    in_specs += [pl.BlockSpec((lk, LANES), k_map) for lk in lks]
    in_specs += [pl.BlockSpec((1, 1, vw, lk), vt_map) for lk in lks]
    return pl.pallas_call(
        functools.partial(_diff_attn_kernel, n_seg=len(lks), panels=panels, lam_init=lam_init),
        grid=(panels.total + 1,),
        in_specs=in_specs,
        out_specs=pl.BlockSpec((pw, vw), out_map),
        out_shape=jax.ShapeDtypeStruct((n_b * lq, DIFF_WIDTH), BF16),
        scratch_shapes=_attn_scratch(sum(lks), pw, vw, per_out),
        compiler_params=_cparams(("arbitrary",)),
        name="diff_attn",
    )(lam_params, sub_gain_col, dqt, *ks, *vts)


def _gqa_attn_kernel(*refs, n_seg, panels):
    it = iter(refs)
    qt_ref = next(it)
    k_refs = [next(it) for _ in range(n_seg)]
    vt_refs = [next(it) for _ in range(n_seg)]
    o_ref, s_scr, m_scr, on_scr, vt_scr = (next(it) for _ in range(5))

    def emit():
        pairs = [jnp.concatenate([on_scr[2 * i], on_scr[2 * i + 1]], axis=0).T
                 for i in range(GQA_GROUP // 2)]
        o_ref[...] = jnp.concatenate(pairs, axis=1).astype(BF16)

    _attn_pipeline(qt_ref, k_refs, vt_refs, s_scr, m_scr, on_scr, vt_scr,
                   panels=panels, emit=emit)


def _gqa_attn(gqt, ks, vts, *, pw):
    n_b, n_h, per_out, _, lq = gqt.shape
    lks = tuple(vt.shape[-1] for vt in vts)
    panels = _Panels(n_b, n_h, lq, pw, per_out)

    def qt_map(g):
        b, h, j, n = panels.loading(g)
        return b, h, j, 0, n

    def k_map(g):
        return panels.loading(g)[0], 0

    def vt_map(g):
        b, h, _, _ = panels.loading(g)
        return b, h, 0

    def out_map(g):
        b, h, _, n = panels.finishing(g)
        return b * panels.tiles + n, h

    in_specs = [pl.BlockSpec((1, 1, 1, LANES, pw), qt_map)]
    in_specs += [pl.BlockSpec((lk, LANES), k_map) for lk in lks]
    in_specs += [pl.BlockSpec((1, HEAD_DIM, lk), vt_map) for lk in lks]
    return pl.pallas_call(
        functools.partial(_gqa_attn_kernel, n_seg=len(lks), panels=panels),
        grid=(panels.total + 1,),
        in_specs=in_specs,
        out_specs=pl.BlockSpec((pw, GQA_GROUP * HEAD_DIM), out_map),
        out_shape=jax.ShapeDtypeStruct((n_b * lq, GQA_WIDTH), BF16),
        scratch_shapes=_attn_scratch(sum(lks), pw, HEAD_DIM, per_out),
        compiler_params=_cparams(("arbitrary",)),
        name="gqa_attn",
    )(gqt, *ks, *vts)


def _merge_kernel(x_ref, zp_ref, z_ref, zn_ref, do_ref, go_ref, gate_ref, mod_ref,
                  wgrp_ref, pscale_ref, wpo_ref, wdo_ref, wgo_ref, wo_ref,
                  gpost_ref, gpre_ref, xo_ref, hf_ref, zext_scr, *, seg_len, tm):
    i = pl.program_id(0)
    tps = seg_len // tm
    t = i % tps
    z = z_ref[...]
    zext_scr[0:POOL_HALO] = jnp.where(t > 0, zp_ref[...], 0.0)
    zext_scr[POOL_HALO:POOL_HALO + tm] = z
    zext_scr[POOL_HALO + tm:] = jnp.where(t < tps - 1, zn_ref[...], 0.0)
    pos = t * tm + lax.broadcasted_iota(jnp.int32, (tm, POOL_GROUP_DIM), 0)
    mixed = []
    for g, w in enumerate(POOL_WINDOWS):
        c0 = g * POOL_GROUP_DIM
        ssum = None
        for d in range(-(w // 2), w // 2):
            zz = zext_scr[POOL_HALO + d:POOL_HALO + d + tm, c0:c0 + POOL_GROUP_DIM]
            ssum = zz if ssum is None else ssum + zz
        cnt = (jnp.minimum(pos + w // 2, seg_len) - jnp.maximum(pos - w // 2, 0)).astype(F32)
        pooled = ssum / cnt - z[:, c0:c0 + POOL_GROUP_DIM]
        mixed.append(jnp.dot(pooled.astype(BF16), wgrp_ref[g], preferred_element_type=F32))
    pool_out = jnp.concatenate(mixed, axis=1) * pscale_ref[...]

    def gate(k):
        gk = gate_ref[:, k * D_MODEL:(k + 1) * D_MODEL]
        return 1.0 / (1.0 + jnp.exp(-gk))

    merged = gate(0) * jnp.dot(pool_out.astype(BF16), wpo_ref[...], preferred_element_type=F32)
    merged += gate(1) * jnp.dot(do_ref[...], wdo_ref[...], preferred_element_type=F32)
    merged += gate(2) * jnp.dot(go_ref[...], wgo_ref[...], preferred_element_type=F32)
    y = jnp.dot(merged.astype(BF16), wo_ref[...], preferred_element_type=F32)
    mod = mod_ref[0, 0]
    xn = x_ref[...] + mod[2:3] * _rms(y, gpost_ref[...])
    xo_ref[...] = xn
    hf_ref[...] = (_rms(xn, gpre_ref[...]) * (1.0 + mod[4:5]) + mod[3:4]).astype(BF16)


def _merge(x, pool_in, diff_out, gqa_out, gates, mod, layer, mod_row_fn, w_grp, pool_scale,
           w_pool_out, w_diff_out, w_gqa_out, w_o, g_post, g_pre_ffn, *, seg_len, tm):
    n = x.shape[0]
    tps = seg_len // tm
    hb = tm // POOL_HALO
    n_hblk = n // POOL_HALO
    row = lambda i: (i, 0)
    in_specs = [
        pl.BlockSpec((tm, D_MODEL), row),
        pl.BlockSpec((POOL_HALO, POOL_WIDTH), lambda i: (jnp.maximum(i * hb - 1, 0), 0)),
        pl.BlockSpec((tm, POOL_WIDTH), row),
        pl.BlockSpec((POOL_HALO, POOL_WIDTH), lambda i: (jnp.minimum((i + 1) * hb, n_hblk - 1), 0)),
        pl.BlockSpec((tm, DIFF_WIDTH), row),
        pl.BlockSpec((tm, GQA_WIDTH), row),
        pl.BlockSpec((tm, N_BRANCH * D_MODEL), row),
        pl.BlockSpec((1, 1, 6, D_MODEL), lambda i: (layer, mod_row_fn(i // tps), 0, 0)),
        _resident(w_grp.shape), _resident(pool_scale.shape), _resident(w_pool_out.shape),
        _resident(w_diff_out.shape), _resident(w_gqa_out.shape), _resident(w_o.shape),
        _resident(g_post.shape), _resident(g_pre_ffn.shape),
    ]
    return pl.pallas_call(
        functools.partial(_merge_kernel, seg_len=seg_len, tm=tm),
        grid=(n // tm,),
        in_specs=in_specs,
        out_specs=[pl.BlockSpec((tm, D_MODEL), row), pl.BlockSpec((tm, D_MODEL), row)],
        out_shape=[jax.ShapeDtypeStruct((n, D_MODEL), F32),
                   jax.ShapeDtypeStruct((n, D_MODEL), BF16)],
        scratch_shapes=[pltpu.VMEM((tm + 2 * POOL_HALO, POOL_WIDTH), F32)],
        compiler_params=_cparams(("arbitrary",)),
        name="merge",
    )(x, pool_in, pool_in, pool_in, diff_out, gqa_out, gates, mod, w_grp, pool_scale,
      w_pool_out, w_diff_out, w_gqa_out, w_o, g_post, g_pre_ffn)


def _mlp_kernel(x_ref, hf_ref, mod_ref, w1_ref, w2_ref, gpost_ref, xo_ref, h_scr):
    hf = hf_ref[...]
    for j in range(D_FF // PROJ_CHUNK):
        cols = slice(j * PROJ_CHUNK, (j + 1) * PROJ_CHUNK)
        h = jnp.maximum(jnp.dot(hf, w1_ref[:, cols], preferred_element_type=F32), 0.0)
        h_scr[:, cols] = (h * h).astype(BF16)
    y = jnp.dot(h_scr[...], w2_ref[...], preferred_element_type=F32)
    xo_ref[...] = x_ref[...] + mod_ref[0, 0][5:6] * _rms(y, gpost_ref[...])


def _mlp(x, hf, mod, layer, mod_row_fn, w1, w2, g_post, *, seg_len, tm):
    n = x.shape[0]
    tps = seg_len // tm
    row = lambda i: (i, 0)
    return pl.pallas_call(
        _mlp_kernel,
        grid=(n // tm,),
        in_specs=[
            pl.BlockSpec((tm, D_MODEL), row),
            pl.BlockSpec((tm, D_MODEL), row),
            pl.BlockSpec((1, 1, 6, D_MODEL), lambda i: (layer, mod_row_fn(i // tps), 0, 0)),
            _resident(w1.shape), _resident(w2.shape), _resident(g_post.shape),
        ],
        out_specs=pl.BlockSpec((tm, D_MODEL), row),
        out_shape=jax.ShapeDtypeStruct((n, D_MODEL), F32),
        scratch_shapes=[pltpu.VMEM((tm, D_FF), BF16)],
        compiler_params=_cparams(("arbitrary",)),
        name="mlp",
    )(x, hf, mod, w1, w2, g_post)


def _rope_tables(seq):
    rows = seq // GRID_W
    row = jnp.repeat(jnp.arange(rows), GRID_W).astype(F32)
    col = jnp.tile(jnp.arange(GRID_W), rows).astype(F32)
    inv = 1.0 / (ROPE_BASE ** (jnp.arange(ROPE_PAIRS_PER_AXIS, dtype=F32) * 2.0 / ROPE_AXIS_DIM))
    ang = jnp.concatenate([row[:, None] * inv, col[:, None] * inv], axis=-1)
    cos, sin = jnp.cos(ang), jnp.sin(ang)
    return (jnp.tile(cos, (1, 4)),
            jnp.concatenate([-sin, -sin, sin, sin], axis=-1))


def _pair_block(base_a, base_b):
    ev, od = np.arange(0, HEAD_DIM, 2), np.arange(1, HEAD_DIM, 2)
    return np.concatenate([base_a + ev, base_b + ev, base_a + od, base_b + od])


def _in_proj_columns():
    idx = np.arange(IN_WIDTH)
    for off in (OFF_DQ, OFF_DK):
        for h in range(DIFF_HEADS):
            b = off + h * LANES
            idx[b:b + LANES] = _pair_block(b, b + HEAD_DIM)
    for j in range(GQA_GROUP):
        b = OFF_GQ + j * LANES
        idx[b:b + LANES] = _pair_block(OFF_GQ + j * HEAD_DIM, OFF_GQ + (GQA_GROUP + j) * HEAD_DIM)
    idx[OFF_GK:OFF_GK + LANES] = _pair_block(OFF_GK, OFF_GK + HEAD_DIM)
    return idx


def _pair_gain(g):
    return jnp.concatenate([g[0::2], g[0::2], g[1::2], g[1::2]]).reshape(1, LANES)


def kernel(x, c, ctx, c_ctx, w_mod, b_mod, g_pre_mix, g_post_mix, g_pre_ffn, g_post_ffn,
           w_in, w_pool_grp, pool_scale, lambda_q1, lambda_k1, lambda_q2, lambda_k2,
           diff_subln, gqa_q_norm, gqa_k_norm, w_pool_out, w_diff_out, w_gqa_out, w_o,
           w_ff1, w_ff2):
    batch, seq, d = x.shape
    ctx_len = ctx.shape[1]
    depth = w_mod.shape[0]
    assert d == D_MODEL and batch < MOD_ROWS and w_in.shape[-1] == IN_WIDTH
    assert GQA_KV_HEADS == 2 and ctx_len % MXU_DIM == 0

    lat_tm, ctx_tm = 512, ctx_len
    rope_tabs = _rope_tables(seq)
    in_cols = _in_proj_columns()
    c_all = jnp.zeros((MOD_ROWS, d), F32).at[:batch].set(c).at[batch].set(c_ctx)
    mod = _modulation(c_all, w_mod, b_mod)
    lat_row = lambda s: s
    ctx_row = lambda s: batch

    xl = x.reshape(batch * seq, d)
    xc = ctx.reshape(batch * ctx_len, d)
    row = lambda a: a.reshape(1, -1)

    for l in range(depth):
        last = l == depth - 1
        lam_init = 0.8 - 0.6 * math.exp(-0.3 * l)
        w_in_b = w_in[l][:, in_cols].astype(BF16)
        qg2, kg2 = _pair_gain(gqa_q_norm[l]), _pair_gain(gqa_k_norm[l])
        lam_params = jnp.stack([lambda_q1[l], lambda_k1[l], lambda_q2[l], lambda_k2[l]])
        sub_gain = diff_subln[l].reshape(-1, 1)
        merge_w = (w_pool_grp[l].astype(BF16), row(pool_scale[l]), w_pool_out[l].astype(BF16),
                   w_diff_out[l].astype(BF16), w_gqa_out[l].astype(BF16), w_o[l].astype(BF16),
                   row(g_post_mix[l]), row(g_pre_ffn[l]))
        mlp_w = (w_ff1[l].astype(BF16), w_ff2[l].astype(BF16), row(g_post_ffn[l]))

        cp = _in_proj(xc, mod, l, ctx_row, row(g_pre_mix[l]), w_in_b, qg2, kg2, None,
                      seg_len=ctx_len, tm=ctx_tm, q_side=not last)
        c_dk, c_dvt, c_gk, c_gvt = cp[-4:]
        pool_in, dqt, gqt, gates, l_dk, l_dvt, l_gk, l_gvt = _in_proj(
            xl, mod, l, lat_row, row(g_pre_mix[l]), w_in_b, qg2, kg2, rope_tabs,
            seg_len=seq, tm=lat_tm, q_side=True)

        diff_out = _diff_attn(dqt, (c_dk, l_dk), (c_dvt, l_dvt), lam_params, sub_gain,
                              lam_init, pw=PANEL)
        gqa_out = _gqa_attn(gqt, (c_gk, l_gk), (c_gvt, l_gvt), pw=PANEL)
        xl_new, hf = _merge(xl, pool_in, diff_out, gqa_out, gates, mod, l, lat_row, *merge_w,
                            seg_len=seq, tm=lat_tm)
        xl_next = _mlp(xl_new, hf, mod, l, lat_row, *mlp_w, seg_len=seq, tm=lat_tm)

        if not last:
            c_pool, c_dqt, c_gqt, c_gates = cp[:4]
            c_diff = _diff_attn(c_dqt, (c_dk,), (c_dvt,), lam_params, sub_gain, lam_init,
                                pw=ctx_len)
            c_gqa = _gqa_attn(c_gqt, (c_gk,), (c_gvt,), pw=ctx_len)
            xc_new, c_hf = _merge(xc, c_pool, c_diff, c_gqa, c_gates, mod, l, ctx_row, *merge_w,
                                  seg_len=ctx_len, tm=ctx_tm)
            xc = _mlp(xc_new, c_hf, mod, l, ctx_row, *mlp_w, seg_len=ctx_len, tm=ctx_tm)
        xl = xl_next
    return xl.reshape(batch, seq, d)
```

```python
import functools
import math

import numpy as np
import jax
import jax.numpy as jnp
from jax import lax
from jax.experimental import pallas as pl
from jax.experimental.pallas import tpu as pltpu

F32 = jnp.float32
BF16 = jnp.bfloat16

D_MODEL = 1024
GRID_W = 64
HEAD_DIM = 64
HALF_HEAD = HEAD_DIM // 2
ATTN_SCALE = HEAD_DIM ** -0.5
LOG2E = math.log2(math.e)
ROPE_BASE = 10000.0
ROPE_AXIS_DIM = HEAD_DIM // 2
ROPE_PAIRS_PER_AXIS = ROPE_AXIS_DIM // 2
EPS = 1e-6

POOL_WINDOWS = (2, 4, 8, 16)
POOL_WIDTH = D_MODEL // 2
POOL_GROUP_DIM = POOL_WIDTH // len(POOL_WINDOWS)
POOL_HALO = max(POOL_WINDOWS) // 2

DIFF_HEADS = D_MODEL // 256
DIFF_WIDTH = DIFF_HEADS * 2 * HEAD_DIM
GQA_HEADS = D_MODEL // 128
GQA_KV_HEADS = GQA_HEADS // 4
GQA_GROUP = GQA_HEADS // GQA_KV_HEADS
GQA_WIDTH = GQA_HEADS * HEAD_DIM
GQA_KV_WIDTH = GQA_KV_HEADS * HEAD_DIM
N_BRANCH = 3
D_FF = 4 * D_MODEL

OFF_POOL = 0
OFF_DQ = OFF_POOL + POOL_WIDTH
OFF_GQ = OFF_DQ + DIFF_WIDTH
OFF_GATE = OFF_GQ + GQA_WIDTH
OFF_DK = OFF_GATE + N_BRANCH * D_MODEL
OFF_DV = OFF_DK + DIFF_WIDTH
OFF_GK = OFF_DV + DIFF_WIDTH
OFF_GV = OFF_GK + GQA_KV_WIDTH
IN_WIDTH = OFF_GV + GQA_KV_WIDTH

LANES = 128
MXU_DIM = 256
PANEL = 4 * MXU_DIM
BF16_ROWS = 16
PROJ_CHUNK = 512
SUB_ROWS = 256
VMEM_LIMIT = 56 * 1024 * 1024

MOD_ROWS = 8


def _cparams(sem):
    return pltpu.CompilerParams(dimension_semantics=sem, vmem_limit_bytes=VMEM_LIMIT)


def _resident(shape):
    nd = len(shape)
    return pl.BlockSpec(shape, lambda *_: (0,) * nd, pipeline_mode=pl.Buffered(1))


def _rms(x, g):
    ms = jnp.mean(x * x, axis=-1, keepdims=True)
    return x * lax.rsqrt(ms + EPS) * g


def _staggered(stages, row_starts):
    row_starts = list(row_starts)
    carry = {}
    for step in range(len(row_starts) + len(stages) - 1):
        for k in reversed(range(len(stages))):
            i = step - k
            if 0 <= i < len(row_starts):
                carry[i] = stages[k](row_starts[i], carry.get(i))


def _mod_kernel(c_ref, w_ref, b_ref, o_ref):
    c = c_ref[...]
    a = c * (1.0 / (1.0 + jnp.exp(-c)))
    o_ref[0] = jnp.dot(a.astype(BF16), w_ref[0].astype(BF16),
                       preferred_element_type=F32) + b_ref[0]


def _modulation(c_all, w_mod, b_mod):
    depth = w_mod.shape[0]
    nblk = w_mod.shape[2] // D_MODEL
    out = pl.pallas_call(
        _mod_kernel,
        grid=(depth, nblk),
        in_specs=[
            pl.BlockSpec((MOD_ROWS, D_MODEL), lambda l, j: (0, 0)),
            pl.BlockSpec((1, D_MODEL, D_MODEL), lambda l, j: (l, 0, j)),
            pl.BlockSpec((1, 1, D_MODEL), lambda l, j: (l, 0, j)),
        ],
        out_specs=pl.BlockSpec((1, MOD_ROWS, D_MODEL), lambda l, j: (l, 0, j)),
        out_shape=jax.ShapeDtypeStruct((depth, MOD_ROWS, nblk * D_MODEL), F32),
        compiler_params=_cparams(("arbitrary", "arbitrary")),
        name="modulation",
    )(c_all, w_mod, b_mod.reshape(depth, 1, nblk * D_MODEL))
    return out.reshape(depth, MOD_ROWS, nblk, D_MODEL)


def _rope(x, cos, sin_signed):
    return x * cos + pltpu.roll(x, HEAD_DIM, 1) * sin_signed


def _head_norm(x, gain2):
    lane = lax.broadcasted_iota(jnp.int32, x.shape, 1)
    first = (lane & HALF_HEAD) == 0
    x2 = x * x
    ms_a = jnp.sum(jnp.where(first, x2, 0.0), axis=-1, keepdims=True) * (1.0 / HEAD_DIM)
    ms_b = jnp.sum(jnp.where(first, 0.0, x2), axis=-1, keepdims=True) * (1.0 / HEAD_DIM)
    inv = jnp.where(first, lax.rsqrt(ms_a + EPS), lax.rsqrt(ms_b + EPS))
    return x * inv * gain2


def _in_proj_kernel(*refs, rope, q_side):
    it = iter(refs)
    x_ref, mod_ref, gpre_ref, w_ref, qg_ref, kg_ref = (next(it) for _ in range(6))
    cos_ref = next(it) if rope else None
    sin_ref = next(it) if rope else None
    if q_side:
        pool_ref, dqt_ref, gqt_ref, gate_ref = (next(it) for _ in range(4))
    dk_ref, dvt_ref, gk_ref, gvt_ref = (next(it) for _ in range(4))

    x = x_ref[...]
    mod = mod_ref[0, 0]
    h = _rms(x, gpre_ref[...]) * (1.0 + mod[1:2]) + mod[0:1]
    hb = h.astype(BF16)
    if rope:
        cos, sin = cos_ref[...], sin_ref[...]

    def proj(off, width):
        return jnp.dot(hb, w_ref[:, off:off + width], preferred_element_type=F32)

    def blocks(off, width):
        y = proj(off, width)
        return [y[:, j * LANES:(j + 1) * LANES] for j in range(width // LANES)]

    def store_heads_t(ref_a, ref_b, y):
        yt = y.T
        row = lax.broadcasted_iota(jnp.int32, yt.shape, 0)
        first = (row & HALF_HEAD) == 0
        ref_a[...] = jnp.where(first, yt, 0.0).astype(BF16)
        ref_b[...] = jnp.where(first, 0.0, yt).astype(BF16)

    q_scale = ATTN_SCALE * LOG2E

    def diff_keys():
        for hh, y in enumerate(blocks(OFF_DK, DIFF_WIDTH)):
            if rope:
                y = _rope(y, cos, sin)
            dk_ref[:, hh * LANES:(hh + 1) * LANES] = y.astype(BF16)

    def gqa_keys():
        (y,) = blocks(OFF_GK, GQA_KV_WIDTH)
        y = _head_norm(y, kg_ref[...])
        if rope:
            y = _rope(y, cos, sin)
        gk_ref[...] = y.astype(BF16)

    def diff_queries():
        for hh, y in enumerate(blocks(OFF_DQ, DIFF_WIDTH)):
            if rope:
                y = _rope(y, cos, sin)
            store_heads_t(dqt_ref.at[0, hh, 0], dqt_ref.at[0, hh, 1], y * q_scale)

    def gqa_queries():
        for jj, y in enumerate(blocks(OFF_GQ, GQA_WIDTH)):
            y = _head_norm(y, qg_ref[...])
            if rope:
                y = _rope(y, cos, sin)
            store_heads_t(gqt_ref.at[0, 0, jj], gqt_ref.at[0, 1, jj], y * q_scale)

    def values():
        for hh, y in enumerate(blocks(OFF_DV, DIFF_WIDTH)):
            dvt_ref[0, hh] = y.T.astype(BF16)
        (y,) = blocks(OFF_GV, GQA_KV_WIDTH)
        gvt_ref[0] = y.T.astype(BF16)

    def gates(j):
        gl = proj(OFF_GATE + j * PROJ_CHUNK, PROJ_CHUNK)
        gate_ref[:, j * PROJ_CHUNK:(j + 1) * PROJ_CHUNK] = 1.0 / (1.0 + jnp.exp(-gl))

    if q_side:
        n_gate = N_BRANCH * D_MODEL // PROJ_CHUNK
        order = [diff_keys, gqa_keys, 0, 1, diff_queries, 2, 3, gqa_queries, 4, 5, values]
        assert sorted(s for s in order if isinstance(s, int)) == list(range(n_gate))
        for step in order:
            if isinstance(step, int):
                gates(step)
            else:
                step()
        pool_ref[...] = proj(OFF_POOL, POOL_WIDTH)
    else:
        diff_keys()
        gqa_keys()
        values()


def _in_proj(x, mod, layer, mod_row_fn, g_pre, w_in, q_gain2, k_gain2, rope_tabs, *,
             seg_len, tm, q_side):
    n = x.shape[0]
    n_seg = n // seg_len
    tps = seg_len // tm
    rope = rope_tabs is not None
    seg = lambda i: i // tps
    pos = lambda i: i % tps
    row = lambda i: (i, 0)

    in_specs = [
        pl.BlockSpec((tm, D_MODEL), row),
        pl.BlockSpec((1, 1, 6, D_MODEL), lambda i: (layer, mod_row_fn(seg(i)), 0, 0)),
        _resident((1, D_MODEL)),
        _resident(w_in.shape),
        _resident((1, LANES)),
        _resident((1, LANES)),
    ]
    args = [x, mod, g_pre, w_in, q_gain2, k_gain2]
    if rope:
        in_specs += [pl.BlockSpec((tm, LANES), lambda i: (pos(i), 0))] * 2
        args += list(rope_tabs)

    out_shape, out_specs = [], []
    if q_side:
        out_shape += [
            jax.ShapeDtypeStruct((n, POOL_WIDTH), F32),
            jax.ShapeDtypeStruct((n_seg, DIFF_HEADS, 2, LANES, seg_len), BF16),
            jax.ShapeDtypeStruct((n_seg, GQA_KV_HEADS, GQA_GROUP, LANES, seg_len), BF16),
            jax.ShapeDtypeStruct((n, N_BRANCH * D_MODEL), F32),
        ]
        out_specs += [
            pl.BlockSpec((tm, POOL_WIDTH), row),
            pl.BlockSpec((1, DIFF_HEADS, 2, LANES, tm), lambda i: (seg(i), 0, 0, 0, pos(i))),
            pl.BlockSpec((1, GQA_KV_HEADS, GQA_GROUP, LANES, tm),
                         lambda i: (seg(i), 0, 0, 0, pos(i))),
            pl.BlockSpec((tm, N_BRANCH * D_MODEL), row),
        ]
    out_shape += [
        jax.ShapeDtypeStruct((n, DIFF_WIDTH), BF16),
        jax.ShapeDtypeStruct((n_seg, DIFF_HEADS, 2 * HEAD_DIM, seg_len), BF16),
        jax.ShapeDtypeStruct((n, GQA_KV_WIDTH), BF16),
        jax.ShapeDtypeStruct((n_seg, GQA_KV_WIDTH, seg_len), BF16),
    ]
    out_specs += [
        pl.BlockSpec((tm, DIFF_WIDTH), row),
        pl.BlockSpec((1, DIFF_HEADS, 2 * HEAD_DIM, tm), lambda i: (seg(i), 0, 0, pos(i))),
        pl.BlockSpec((tm, GQA_KV_WIDTH), row),
        pl.BlockSpec((1, GQA_KV_WIDTH, tm), lambda i: (seg(i), 0, pos(i))),
    ]
    return pl.pallas_call(
        functools.partial(_in_proj_kernel, rope=rope, q_side=q_side),
        grid=(n // tm,),
        in_specs=in_specs,
        out_specs=out_specs,
        out_shape=out_shape,
        compiler_params=_cparams(("arbitrary",)),
        name="in_proj" + ("_rope" if rope else "") + ("" if q_side else "_kv"),
    )(*args)


class _Panels:
    def __init__(self, n_b, n_h, lq, pw, per_out):
        self.n_h, self.per_out, self.tiles = n_h, per_out, lq // pw
        self.per_head = per_out * self.tiles
        self.total = n_b * n_h * self.per_head

    def split(self, q):
        bh, loc = q // self.per_head, q % self.per_head
        return bh // self.n_h, bh % self.n_h, loc % self.per_out, loc // self.per_out

    def loading(self, g):
        return self.split(jnp.minimum(g, self.total - 1))

    def finishing(self, g):
        return self.split(jnp.maximum(g - 1, 0))


def _attn_scratch(nk, pw, v_rows, per_out, score_pad=0):
    return [
        pltpu.VMEM((2, nk, pw + score_pad), F32),
        pltpu.VMEM((2, 8, pw), F32),
        pltpu.VMEM((per_out, v_rows, pw), F32),
        pltpu.VMEM((2, v_rows + BF16_ROWS, nk), BF16),
    ]


def _attn_pipeline(qt_ref, k_refs, vt_refs, s_scr, m_scr, on_scr, vt_scr, *, panels, emit):
    g = pl.program_id(0)
    pw = qt_ref.shape[-1]
    total, per_head, per_out = panels.total, panels.per_head, panels.per_out
    q_load = jnp.minimum(g, total - 1)
    q_done = jnp.maximum(g - 1, 0)

    v_rows = on_scr.shape[1]

    @pl.when((g < total) & (q_load % per_head == 0))
    def _():
        dst = vt_scr.at[(q_load // per_head) & 1]
        c = 0
        for vt_ref in vt_refs:
            vt = vt_ref[...]
            dst[0:v_rows, c:c + vt.shape[-1]] = vt.reshape(vt.shape[-2:])
            c += vt.shape[-1]
        dst[v_rows:, :] = jnp.ones((dst.shape[0] - v_rows, dst.shape[1]), BF16)

    key_chunks = [(k_ref, c0) for k_ref in k_refs for c0 in range(0, k_ref.shape[0], MXU_DIM)]

    def phase(score_buf, value_buf):
        if score_buf is not None:
            qt = qt_ref[...].reshape(qt_ref.shape[-2:])
        if value_buf is not None:
            m_prev = m_scr[value_buf][0:1]
            vt = vt_scr.at[(q_done // per_head) & 1]
        m, acc = None, None
        for c, (k_ref, c0) in enumerate(key_chunks):
            rows = slice(c * MXU_DIM, (c + 1) * MXU_DIM)
            if value_buf is not None:
                pt = jnp.exp2(s_scr[value_buf, rows, 0:pw] - m_prev).astype(BF16)
                d = jnp.dot(vt[:, rows], pt, preferred_element_type=F32)
                acc = d if acc is None else acc + d
            if score_buf is not None:
                s = jnp.dot(k_ref[c0:c0 + MXU_DIM, :], qt, preferred_element_type=F32)
                s_scr[score_buf, rows, 0:pw] = s
                mc = jnp.max(s, axis=0, keepdims=True)
                m = mc if m is None else jnp.maximum(m, mc)
        if score_buf is not None:
            m_scr[score_buf] = jnp.broadcast_to(m, m_scr.shape[1:])
        if value_buf is not None:
            on_scr[q_done % per_out] = acc[0:v_rows] / acc[v_rows:v_rows + 1]

    @pl.when(g == 0)
    def _():
        phase(0, None)

    for parity in range(2):
        @pl.when((g > 0) & (g < total) & ((g & 1) == parity))
        def _():
            phase(parity, 1 - parity)

    @pl.when(g == total)
    def _():
        phase(None, (total - 1) % 2)

    @pl.when((g > 0) & (q_done % per_out == per_out - 1))
    def _():
        emit()


def _diff_attn_kernel(*refs, n_seg, panels, lam_init):
    it = iter(refs)
    lam_ref, subg_ref, qt_ref = next(it), next(it), next(it)
    k_refs = [next(it) for _ in range(n_seg)]
    vt_refs = [next(it) for _ in range(n_seg)]
    o_ref, s_scr, m_scr, on_scr, vt_scr = (next(it) for _ in range(5))

    def emit():
        lp = lam_ref[...]
        lam = (jnp.exp(jnp.sum(lp[0:1] * lp[1:2], axis=-1, keepdims=True))
               - jnp.exp(jnp.sum(lp[2:3] * lp[3:4], axis=-1, keepdims=True)) + lam_init)
        o = on_scr[0] - lam * on_scr[1]
        ms = jnp.mean(o * o, axis=0, keepdims=True)
        o = o * lax.rsqrt(ms + EPS) * subg_ref[...] * (1.0 - lam_init)
        o_ref[...] = o.T.astype(BF16)

    _attn_pipeline(qt_ref, k_refs, vt_refs, s_scr, m_scr, on_scr, vt_scr,
                   panels=panels, emit=emit)


def _diff_attn(dqt, ks, vts, lam_params, sub_gain_col, lam_init, *, pw):
    n_b, n_h, per_out, _, lq = dqt.shape
    lks = tuple(vt.shape[-1] for vt in vts)
    vw = 2 * HEAD_DIM
    panels = _Panels(n_b, n_h, lq, pw, per_out)

    def qt_map(g):
        b, h, j, n = panels.loading(g)
        return b, h, j, 0, n

    def k_map(g):
        b, h, _, _ = panels.loading(g)
        return b, h

    def vt_map(g):
        b, h, _, _ = panels.loading(g)
        return b, h, 0, 0

    def out_map(g):
        b, h, _, n = panels.finishing(g)
        return b * panels.tiles + n, h

    in_specs = [
        _resident(lam_params.shape),
        _resident(sub_gain_col.shape),
        pl.BlockSpec((1, 1, 1, LANES, pw), qt_map),
    ]
    in_specs += [pl.BlockSpec((lk, LANES), k_map) for lk in lks]
    in_specs += [pl.BlockSpec((1, 1, vw, lk), vt_map) for lk in lks]
    return pl.pallas_call(
        functools.partial(_diff_attn_kernel, n_seg=len(lks), panels=panels, lam_init=lam_init),
        grid=(panels.total + 1,),
        in_specs=in_specs,
        out_specs=pl.BlockSpec((pw, vw), out_map),
        out_shape=jax.ShapeDtypeStruct((n_b * lq, DIFF_WIDTH), BF16),
        scratch_shapes=_attn_scratch(sum(lks), pw, vw, per_out, score_pad=LANES),
        compiler_params=_cparams(("arbitrary",)),
        name="diff_attn",
    )(lam_params, sub_gain_col, dqt, *ks, *vts)


def _gqa_attn_kernel(*refs, n_seg, panels):
    it = iter(refs)
    qt_ref = next(it)
    k_refs = [next(it) for _ in range(n_seg)]
    vt_refs = [next(it) for _ in range(n_seg)]
    o_ref, s_scr, m_scr, on_scr, vt_scr = (next(it) for _ in range(5))

    def emit():
        pairs = [jnp.concatenate([on_scr[2 * i], on_scr[2 * i + 1]], axis=0).T
                 for i in range(GQA_GROUP // 2)]
        o_ref[...] = jnp.concatenate(pairs, axis=1).astype(BF16)

    _attn_pipeline(qt_ref, k_refs, vt_refs, s_scr, m_scr, on_scr, vt_scr,
                   panels=panels, emit=emit)


def _gqa_attn(gqt, ks, vts, *, pw):
    n_b, n_h, per_out, _, lq = gqt.shape
    lks = tuple(vt.shape[-1] for vt in vts)
    panels = _Panels(n_b, n_h, lq, pw, per_out)

    def qt_map(g):
        b, h, j, n = panels.loading(g)
        return b, h, j, 0, n

    def k_map(g):
        return panels.loading(g)[0], 0

    def vt_map(g):
        b, h, _, _ = panels.loading(g)
        return b, h, 0

    def out_map(g):
        b, h, _, n = panels.finishing(g)
        return b * panels.tiles + n, h

    in_specs = [pl.BlockSpec((1, 1, 1, LANES, pw), qt_map)]
    in_specs += [pl.BlockSpec((lk, LANES), k_map) for lk in lks]
    in_specs += [pl.BlockSpec((1, HEAD_DIM, lk), vt_map) for lk in lks]
    return pl.pallas_call(
        functools.partial(_gqa_attn_kernel, n_seg=len(lks), panels=panels),
        grid=(panels.total + 1,),
        in_specs=in_specs,
        out_specs=pl.BlockSpec((pw, GQA_GROUP * HEAD_DIM), out_map),
        out_shape=jax.ShapeDtypeStruct((n_b * lq, GQA_WIDTH), BF16),
        scratch_shapes=_attn_scratch(sum(lks), pw, HEAD_DIM, per_out),
        compiler_params=_cparams(("arbitrary",)),
        name="gqa_attn",
    )(gqt, *ks, *vts)


def _merge_kernel(x_ref, zp_ref, z_ref, zn_ref, icnt_ref, do_ref, go_ref, gate_ref, mod_ref,
                  wgrp_ref, pscale_ref, wpo_ref, wdo_ref, wgo_ref, wo_ref,
                  gpost_ref, gpre_ref, xo_ref, hf_ref, zext_scr, *, seg_len, tm):
    i = pl.program_id(0)
    tps = seg_len // tm
    t = i % tps
    z = z_ref[...]
    zext_scr[0:POOL_HALO] = jnp.where(t > 0, zp_ref[...], 0.0)
    zext_scr[POOL_HALO:POOL_HALO + tm] = z
    zext_scr[POOL_HALO + tm:] = jnp.where(t < tps - 1, zn_ref[...], 0.0)
    mod = mod_ref[0, 0]
    sub = min(tm, SUB_ROWS)

    def pool_mixer(r0, _):
        mixed = []
        for g, w in enumerate(POOL_WINDOWS):
            cs = slice(g * POOL_GROUP_DIM, (g + 1) * POOL_GROUP_DIM)
            ssum = None
            for d in range(-(w // 2), w // 2):
                zz = zext_scr[POOL_HALO + d + r0:POOL_HALO + d + r0 + sub, cs]
                ssum = zz if ssum is None else ssum + zz
            pooled = ssum * icnt_ref[r0:r0 + sub, cs] - z_ref[r0:r0 + sub, cs]
            mixed.append(jnp.dot(pooled.astype(BF16), wgrp_ref[g], preferred_element_type=F32))
        return (jnp.concatenate(mixed, axis=1) * pscale_ref[...]).astype(BF16)

    def gated_branches(r0, pool_out):
        rs = slice(r0, r0 + sub)

        def gate(k):
            return gate_ref[rs, k * D_MODEL:(k + 1) * D_MODEL]

        merged = gate(0) * jnp.dot(pool_out, wpo_ref[...], preferred_element_type=F32)
        merged += gate(1) * jnp.dot(do_ref[rs, :], wdo_ref[...], preferred_element_type=F32)
        merged += gate(2) * jnp.dot(go_ref[rs, :], wgo_ref[...], preferred_element_type=F32)
        return merged.astype(BF16)

    def out_proj(r0, merged):
        return jnp.dot(merged, wo_ref[...], preferred_element_type=F32)

    def residual(r0, y):
        rs = slice(r0, r0 + sub)
        xn = x_ref[rs, :] + mod[2:3] * _rms(y, gpost_ref[...])
        xo_ref[rs, :] = xn
        hf_ref[rs, :] = (_rms(xn, gpre_ref[...]) * (1.0 + mod[4:5]) + mod[3:4]).astype(BF16)

    _staggered([pool_mixer, gated_branches, out_proj, residual], range(0, tm, sub))


def _pool_inverse_counts(seg_len):
    t = jnp.arange(seg_len)
    cols = []
    for w in POOL_WINDOWS:
        cnt = jnp.minimum(t + w // 2, seg_len) - jnp.maximum(t - w // 2, 0)
        cols.append(jnp.broadcast_to((1.0 / cnt.astype(F32))[:, None], (seg_len, POOL_GROUP_DIM)))
    return jnp.concatenate(cols, axis=1)


def _merge(x, pool_in, diff_out, gqa_out, gates, mod, layer, mod_row_fn, w_grp, pool_scale,
           w_pool_out, w_diff_out, w_gqa_out, w_o, g_post, g_pre_ffn, *, seg_len, tm):
    n = x.shape[0]
    tps = seg_len // tm
    hb = tm // POOL_HALO
    n_hblk = n // POOL_HALO
    row = lambda i: (i, 0)
    in_specs = [
        pl.BlockSpec((tm, D_MODEL), row),
        pl.BlockSpec((POOL_HALO, POOL_WIDTH), lambda i: (jnp.maximum(i * hb - 1, 0), 0)),
        pl.BlockSpec((tm, POOL_WIDTH), row),
        pl.BlockSpec((POOL_HALO, POOL_WIDTH), lambda i: (jnp.minimum((i + 1) * hb, n_hblk - 1), 0)),
        pl.BlockSpec((tm, POOL_WIDTH), lambda i: (i % tps, 0)),
        pl.BlockSpec((tm, DIFF_WIDTH), row),
        pl.BlockSpec((tm, GQA_WIDTH), row),
        pl.BlockSpec((tm, N_BRANCH * D_MODEL), row),
        pl.BlockSpec((1, 1, 6, D_MODEL), lambda i: (layer, mod_row_fn(i // tps), 0, 0)),
        _resident(w_grp.shape), _resident(pool_scale.shape), _resident(w_pool_out.shape),
        _resident(w_diff_out.shape), _resident(w_gqa_out.shape), _resident(w_o.shape),
        _resident(g_post.shape), _resident(g_pre_ffn.shape),
    ]
    return pl.pallas_call(
        functools.partial(_merge_kernel, seg_len=seg_len, tm=tm),
        grid=(n // tm,),
        in_specs=in_specs,
        out_specs=[pl.BlockSpec((tm, D_MODEL), row), pl.BlockSpec((tm, D_MODEL), row)],
        out_shape=[jax.ShapeDtypeStruct((n, D_MODEL), F32),
                   jax.ShapeDtypeStruct((n, D_MODEL), BF16)],
        scratch_shapes=[pltpu.VMEM((tm + 2 * POOL_HALO, POOL_WIDTH), F32)],
        compiler_params=_cparams(("arbitrary",)),
        name="merge",
    )(x, pool_in, pool_in, pool_in, _pool_inverse_counts(seg_len), diff_out, gqa_out, gates, mod,
      w_grp, pool_scale, w_pool_out, w_diff_out, w_gqa_out, w_o, g_post, g_pre_ffn)


def _mlp_kernel(x_ref, hf_ref, mod_ref, w1_ref, w2_ref, gpost_ref, xo_ref, h_scr):
    hf = hf_ref[...]
    for j in range(D_FF // PROJ_CHUNK):
        cols = slice(j * PROJ_CHUNK, (j + 1) * PROJ_CHUNK)
        h = jnp.maximum(jnp.dot(hf, w1_ref[:, cols], preferred_element_type=F32), 0.0)
        h_scr[:, cols] = (h * h).astype(BF16)
    y = jnp.dot(h_scr[...], w2_ref[...], preferred_element_type=F32)
    xo_ref[...] = x_ref[...] + mod_ref[0, 0][5:6] * _rms(y, gpost_ref[...])


def _mlp(x, hf, mod, layer, mod_row_fn, w1, w2, g_post, *, seg_len, tm):
    n = x.shape[0]
    tps = seg_len // tm
    row = lambda i: (i, 0)
    return pl.pallas_call(
        _mlp_kernel,
        grid=(n // tm,),
        in_specs=[
            pl.BlockSpec((tm, D_MODEL), row),
            pl.BlockSpec((tm, D_MODEL), row),
            pl.BlockSpec((1, 1, 6, D_MODEL), lambda i: (layer, mod_row_fn(i // tps), 0, 0)),
            _resident(w1.shape), _resident(w2.shape), _resident(g_post.shape),
        ],
        out_specs=pl.BlockSpec((tm, D_MODEL), row),
        out_shape=jax.ShapeDtypeStruct((n, D_MODEL), F32),
        scratch_shapes=[pltpu.VMEM((tm, D_FF), BF16)],
        compiler_params=_cparams(("arbitrary",)),
        name="mlp",
    )(x, hf, mod, w1, w2, g_post)


def _rope_tables(seq):
    rows = seq // GRID_W
    row = jnp.repeat(jnp.arange(rows), GRID_W).astype(F32)
    col = jnp.tile(jnp.arange(GRID_W), rows).astype(F32)
    inv = 1.0 / (ROPE_BASE ** (jnp.arange(ROPE_PAIRS_PER_AXIS, dtype=F32) * 2.0 / ROPE_AXIS_DIM))
    ang = jnp.concatenate([row[:, None] * inv, col[:, None] * inv], axis=-1)
    cos, sin = jnp.cos(ang), jnp.sin(ang)
    return (jnp.tile(cos, (1, 4)),
            jnp.concatenate([-sin, -sin, sin, sin], axis=-1))


def _pair_layout(w, member_major):
    d = w.shape[0]
    if member_major:
        w5, perm = w.reshape(d, 2, -1, HALF_HEAD, 2), (0, 2, 4, 1, 3)
    else:
        w5, perm = w.reshape(d, -1, 2, HALF_HEAD, 2), (0, 1, 4, 2, 3)
    return w5.transpose(perm).reshape(d, -1)


def _in_proj_weights(w):
    seg = lambda a, b: w[:, a:b]
    return jnp.concatenate([
        seg(OFF_POOL, OFF_DQ),
        _pair_layout(seg(OFF_DQ, OFF_GQ), False),
        _pair_layout(seg(OFF_GQ, OFF_GATE), True),
        seg(OFF_GATE, OFF_DK),
        _pair_layout(seg(OFF_DK, OFF_DV), False),
        seg(OFF_DV, OFF_GK),
        _pair_layout(seg(OFF_GK, OFF_GV), False),
        seg(OFF_GV, IN_WIDTH),
    ], axis=1).astype(BF16)


def _pair_gain(g):
    return jnp.concatenate([g[0::2], g[0::2], g[1::2], g[1::2]]).reshape(1, LANES)


def kernel(x, c, ctx, c_ctx, w_mod, b_mod, g_pre_mix, g_post_mix, g_pre_ffn, g_post_ffn,
           w_in, w_pool_grp, pool_scale, lambda_q1, lambda_k1, lambda_q2, lambda_k2,
           diff_subln, gqa_q_norm, gqa_k_norm, w_pool_out, w_diff_out, w_gqa_out, w_o,
           w_ff1, w_ff2):
    batch, seq, d = x.shape
    ctx_len = ctx.shape[1]
    depth = w_mod.shape[0]
    assert d == D_MODEL and batch < MOD_ROWS and w_in.shape[-1] == IN_WIDTH
    assert GQA_KV_HEADS == 2 and ctx_len % MXU_DIM == 0

    lat_tm, ctx_tm = 512, ctx_len
    rope_tabs = _rope_tables(seq)
    c_all = jnp.zeros((MOD_ROWS, d), F32).at[:batch].set(c).at[batch].set(c_ctx)
    mod = _modulation(c_all, w_mod, b_mod)
    lat_row = lambda s: s
    ctx_row = lambda s: batch

    xl = x.reshape(batch * seq, d)
    xc = ctx.reshape(batch * ctx_len, d)
    row = lambda a: a.reshape(1, -1)

    for l in range(depth):
        last = l == depth - 1
        lam_init = 0.8 - 0.6 * math.exp(-0.3 * l)
        w_in_b = _in_proj_weights(w_in[l])
        qg2, kg2 = _pair_gain(gqa_q_norm[l]), _pair_gain(gqa_k_norm[l])
        lam_params = jnp.stack([lambda_q1[l], lambda_k1[l], lambda_q2[l], lambda_k2[l]])
        sub_gain = diff_subln[l].reshape(-1, 1)
        merge_w = (w_pool_grp[l].astype(BF16), row(pool_scale[l]), w_pool_out[l].astype(BF16),
                   w_diff_out[l].astype(BF16), w_gqa_out[l].astype(BF16), w_o[l].astype(BF16),
                   row(g_post_mix[l]), row(g_pre_ffn[l]))
        mlp_w = (w_ff1[l].astype(BF16), w_ff2[l].astype(BF16), row(g_post_ffn[l]))

        cp = _in_proj(xc, mod, l, ctx_row, row(g_pre_mix[l]), w_in_b, qg2, kg2, None,
                      seg_len=ctx_len, tm=ctx_tm, q_side=not last)
        c_dk, c_dvt, c_gk, c_gvt = cp[-4:]
        pool_in, dqt, gqt, gates, l_dk, l_dvt, l_gk, l_gvt = _in_proj(
            xl, mod, l, lat_row, row(g_pre_mix[l]), w_in_b, qg2, kg2, rope_tabs,
            seg_len=seq, tm=lat_tm, q_side=True)

        diff_out = _diff_attn(dqt, (c_dk, l_dk), (c_dvt, l_dvt), lam_params, sub_gain,
                              lam_init, pw=PANEL)
        gqa_out = _gqa_attn(gqt, (c_gk, l_gk), (c_gvt, l_gvt), pw=PANEL // 2)
        xl_new, hf = _merge(xl, pool_in, diff_out, gqa_out, gates, mod, l, lat_row, *merge_w,
                            seg_len=seq, tm=lat_tm)
        xl_next = _mlp(xl_new, hf, mod, l, lat_row, *mlp_w, seg_len=seq, tm=lat_tm)

        if not last:
            c_pool, c_dqt, c_gqt, c_gates = cp[:4]
            c_diff = _diff_attn(c_dqt, (c_dk,), (c_dvt,), lam_params, sub_gain, lam_init,
                                pw=ctx_len)
            c_gqa = _gqa_attn(c_gqt, (c_gk,), (c_gvt,), pw=ctx_len)
            xc_new, c_hf = _merge(xc, c_pool, c_diff, c_gqa, c_gates, mod, l, ctx_row, *merge_w,
                                  seg_len=ctx_len, tm=ctx_tm)
            xc = _mlp(xc_new, c_hf, mod, l, ctx_row, *mlp_w, seg_len=ctx_len, tm=ctx_tm)
        xl = xl_next
    return xl.reshape(batch, seq, d)
```

```python
import functools
import math

import numpy as np
import jax
import jax.numpy as jnp
from jax import lax
from jax.experimental import pallas as pl
from jax.experimental.pallas import tpu as pltpu

F32 = jnp.float32
BF16 = jnp.bfloat16

D_MODEL = 1024
GRID_W = 64
HEAD_DIM = 64
HALF_HEAD = HEAD_DIM // 2
ATTN_SCALE = HEAD_DIM ** -0.5
LOG2E = math.log2(math.e)
ROPE_BASE = 10000.0
ROPE_AXIS_DIM = HEAD_DIM // 2
ROPE_PAIRS_PER_AXIS = ROPE_AXIS_DIM // 2
EPS = 1e-6

POOL_WINDOWS = (2, 4, 8, 16)
POOL_WIDTH = D_MODEL // 2
POOL_GROUP_DIM = POOL_WIDTH // len(POOL_WINDOWS)
POOL_HALO = max(POOL_WINDOWS) // 2

DIFF_HEADS = D_MODEL // 256
DIFF_WIDTH = DIFF_HEADS * 2 * HEAD_DIM
GQA_HEADS = D_MODEL // 128
GQA_KV_HEADS = GQA_HEADS // 4
GQA_GROUP = GQA_HEADS // GQA_KV_HEADS
GQA_WIDTH = GQA_HEADS * HEAD_DIM
GQA_KV_WIDTH = GQA_KV_HEADS * HEAD_DIM
N_BRANCH = 3
D_FF = 4 * D_MODEL

OFF_POOL = 0
OFF_DQ = OFF_POOL + POOL_WIDTH
OFF_GQ = OFF_DQ + DIFF_WIDTH
OFF_GATE = OFF_GQ + GQA_WIDTH
OFF_DK = OFF_GATE + N_BRANCH * D_MODEL
OFF_DV = OFF_DK + DIFF_WIDTH
OFF_GK = OFF_DV + DIFF_WIDTH
OFF_GV = OFF_GK + GQA_KV_WIDTH
IN_WIDTH = OFF_GV + GQA_KV_WIDTH

LANES = 128
MXU_DIM = 256
PANEL = 4 * MXU_DIM
BF16_ROWS = 16
PROJ_CHUNK = 512
SUB_ROWS = 256
VMEM_LIMIT = 56 * 1024 * 1024

MOD_ROWS = 8


def _cparams(sem):
    return pltpu.CompilerParams(dimension_semantics=sem, vmem_limit_bytes=VMEM_LIMIT)


def _resident(shape):
    nd = len(shape)
    return pl.BlockSpec(shape, lambda *_: (0,) * nd, pipeline_mode=pl.Buffered(1))


def _rms(x, g):
    ms = jnp.mean(x * x, axis=-1, keepdims=True)
    return x * lax.rsqrt(ms + EPS) * g


def _staggered(stages, row_starts):
    row_starts = list(row_starts)
    carry = {}
    for step in range(len(row_starts) + len(stages) - 1):
        for k in reversed(range(len(stages))):
            i = step - k
            if 0 <= i < len(row_starts):
                carry[i] = stages[k](row_starts[i], carry.get(i))


def _mod_kernel(c_ref, w_ref, b_ref, o_ref):
    c = c_ref[...]
    a = c * (1.0 / (1.0 + jnp.exp(-c)))
    o_ref[0] = jnp.dot(a.astype(BF16), w_ref[0].astype(BF16),
                       preferred_element_type=F32) + b_ref[0]


def _modulation(c_all, w_mod, b_mod):
    depth = w_mod.shape[0]
    nblk = w_mod.shape[2] // D_MODEL
    out = pl.pallas_call(
        _mod_kernel,
        grid=(depth, nblk),
        in_specs=[
            pl.BlockSpec((MOD_ROWS, D_MODEL), lambda l, j: (0, 0)),
            pl.BlockSpec((1, D_MODEL, D_MODEL), lambda l, j: (l, 0, j)),
            pl.BlockSpec((1, 1, D_MODEL), lambda l, j: (l, 0, j)),
        ],
        out_specs=pl.BlockSpec((1, MOD_ROWS, D_MODEL), lambda l, j: (l, 0, j)),
        out_shape=jax.ShapeDtypeStruct((depth, MOD_ROWS, nblk * D_MODEL), F32),
        compiler_params=_cparams(("arbitrary", "arbitrary")),
        name="modulation",
    )(c_all, w_mod, b_mod.reshape(depth, 1, nblk * D_MODEL))
    return out.reshape(depth, MOD_ROWS, nblk, D_MODEL)


def _rope(x, cos, sin_signed):
    return x * cos + pltpu.roll(x, HEAD_DIM, 1) * sin_signed


def _head_norm(x, gain2):
    lane = lax.broadcasted_iota(jnp.int32, x.shape, 1)
    first = (lane & HALF_HEAD) == 0
    x2 = x * x
    ms_a = jnp.sum(jnp.where(first, x2, 0.0), axis=-1, keepdims=True) * (1.0 / HEAD_DIM)
    ms_b = jnp.sum(jnp.where(first, 0.0, x2), axis=-1, keepdims=True) * (1.0 / HEAD_DIM)
    inv = jnp.where(first, lax.rsqrt(ms_a + EPS), lax.rsqrt(ms_b + EPS))
    return x * inv * gain2


def _in_proj_kernel(*refs, rope, q_side):
    it = iter(refs)
    x_ref, mod_ref, gpre_ref, w_ref, qg_ref, kg_ref = (next(it) for _ in range(6))
    cos_ref = next(it) if rope else None
    sin_ref = next(it) if rope else None
    if q_side:
        pool_ref, dqt_ref, gqt_ref, gate_ref = (next(it) for _ in range(4))
    dk_ref, dvt_ref, gk_ref, gvt_ref = (next(it) for _ in range(4))

    x = x_ref[...]
    mod = mod_ref[0, 0]
    h = _rms(x, gpre_ref[...]) * (1.0 + mod[1:2]) + mod[0:1]
    hb = h.astype(BF16)
    if rope:
        cos, sin = cos_ref[...], sin_ref[...]

    def proj(off, width):
        return jnp.dot(hb, w_ref[:, off:off + width], preferred_element_type=F32)

    def blocks(off, width):
        y = proj(off, width)
        return [y[:, j * LANES:(j + 1) * LANES] for j in range(width // LANES)]

    def store_heads_t(ref_a, ref_b, y):
        yt = y.T
        row = lax.broadcasted_iota(jnp.int32, yt.shape, 0)
        first = (row & HALF_HEAD) == 0
        ref_a[...] = jnp.where(first, yt, 0.0).astype(BF16)
        ref_b[...] = jnp.where(first, 0.0, yt).astype(BF16)

    q_scale = ATTN_SCALE * LOG2E

    def diff_keys():
        for hh, y in enumerate(blocks(OFF_DK, DIFF_WIDTH)):
            if rope:
                y = _rope(y, cos, sin)
            dk_ref[:, hh * LANES:(hh + 1) * LANES] = y.astype(BF16)

    def gqa_keys():
        (y,) = blocks(OFF_GK, GQA_KV_WIDTH)
        y = _head_norm(y, kg_ref[...])
        if rope:
            y = _rope(y, cos, sin)
        gk_ref[...] = y.astype(BF16)

    def diff_queries():
        for hh, y in enumerate(blocks(OFF_DQ, DIFF_WIDTH)):
            if rope:
                y = _rope(y, cos, sin)
            store_heads_t(dqt_ref.at[0, hh, 0], dqt_ref.at[0, hh, 1], y * q_scale)

    def gqa_queries():
        for jj, y in enumerate(blocks(OFF_GQ, GQA_WIDTH)):
            y = _head_norm(y, qg_ref[...])
            if rope:
                y = _rope(y, cos, sin)
            store_heads_t(gqt_ref.at[0, 0, jj], gqt_ref.at[0, 1, jj], y * q_scale)

    def values():
        for hh, y in enumerate(blocks(OFF_DV, DIFF_WIDTH)):
            dvt_ref[0, hh] = y.T.astype(BF16)
        (y,) = blocks(OFF_GV, GQA_KV_WIDTH)
        gvt_ref[0] = y.T.astype(BF16)

    def gates(j):
        gl = proj(OFF_GATE + j * PROJ_CHUNK, PROJ_CHUNK)
        gate_ref[:, j * PROJ_CHUNK:(j + 1) * PROJ_CHUNK] = (1.0 / (1.0 + jnp.exp(-gl))).astype(BF16)

    if q_side:
        n_gate = N_BRANCH * D_MODEL // PROJ_CHUNK
        order = [diff_keys, gqa_keys, 0, 1, diff_queries, 2, 3, gqa_queries, 4, 5, values]
        assert sorted(s for s in order if isinstance(s, int)) == list(range(n_gate))
        for step in order:
            if isinstance(step, int):
                gates(step)
            else:
                step()
        pool_ref[...] = proj(OFF_POOL, POOL_WIDTH)
    else:
        diff_keys()
        gqa_keys()
        values()


def _in_proj(x, mod, layer, mod_row_fn, g_pre, w_in, q_gain2, k_gain2, rope_tabs, *,
             seg_len, tm, q_side):
    n = x.shape[0]
    n_seg = n // seg_len
    tps = seg_len // tm
    rope = rope_tabs is not None
    seg = lambda i: i // tps
    pos = lambda i: i % tps
    row = lambda i: (i, 0)

    in_specs = [
        pl.BlockSpec((tm, D_MODEL), row),
        pl.BlockSpec((1, 1, 6, D_MODEL), lambda i: (layer, mod_row_fn(seg(i)), 0, 0)),
        _resident((1, D_MODEL)),
        _resident(w_in.shape),
        _resident((1, LANES)),
        _resident((1, LANES)),
    ]
    args = [x, mod, g_pre, w_in, q_gain2, k_gain2]
    if rope:
        in_specs += [pl.BlockSpec((tm, LANES), lambda i: (pos(i), 0))] * 2
        args += list(rope_tabs)

    out_shape, out_specs = [], []
    if q_side:
        out_shape += [
            jax.ShapeDtypeStruct((n, POOL_WIDTH), F32),
            jax.ShapeDtypeStruct((n_seg, DIFF_HEADS, 2, LANES, seg_len), BF16),
            jax.ShapeDtypeStruct((n_seg, GQA_KV_HEADS, GQA_GROUP, LANES, seg_len), BF16),
            jax.ShapeDtypeStruct((n, N_BRANCH * D_MODEL), BF16),
        ]
        out_specs += [
            pl.BlockSpec((tm, POOL_WIDTH), row),
            pl.BlockSpec((1, DIFF_HEADS, 2, LANES, tm), lambda i: (seg(i), 0, 0, 0, pos(i))),
            pl.BlockSpec((1, GQA_KV_HEADS, GQA_GROUP, LANES, tm),
                         lambda i: (seg(i), 0, 0, 0, pos(i))),
            pl.BlockSpec((tm, N_BRANCH * D_MODEL), row),
        ]
    out_shape += [
        jax.ShapeDtypeStruct((n, DIFF_WIDTH), BF16),
        jax.ShapeDtypeStruct((n_seg, DIFF_HEADS, 2 * HEAD_DIM, seg_len), BF16),
        jax.ShapeDtypeStruct((n, GQA_KV_WIDTH), BF16),
        jax.ShapeDtypeStruct((n_seg, GQA_KV_WIDTH, seg_len), BF16),
    ]
    out_specs += [
        pl.BlockSpec((tm, DIFF_WIDTH), row),
        pl.BlockSpec((1, DIFF_HEADS, 2 * HEAD_DIM, tm), lambda i: (seg(i), 0, 0, pos(i))),
        pl.BlockSpec((tm, GQA_KV_WIDTH), row),
        pl.BlockSpec((1, GQA_KV_WIDTH, tm), lambda i: (seg(i), 0, pos(i))),
    ]
    return pl.pallas_call(
        functools.partial(_in_proj_kernel, rope=rope, q_side=q_side),
        grid=(n // tm,),
        in_specs=in_specs,
        out_specs=out_specs,
        out_shape=out_shape,
        compiler_params=_cparams(("arbitrary",)),
        name="in_proj" + ("_rope" if rope else "") + ("" if q_side else "_kv"),
    )(*args)


class _Panels:
    def __init__(self, n_b, n_h, lq, pw, per_out):
        self.n_h, self.per_out, self.tiles = n_h, per_out, lq // pw
        self.per_head = per_out * self.tiles
        self.total = n_b * n_h * self.per_head

    def split(self, q):
        bh, loc = q // self.per_head, q % self.per_head
        return bh // self.n_h, bh % self.n_h, loc % self.per_out, loc // self.per_out

    def loading(self, g):
        return self.split(jnp.minimum(g, self.total - 1))

    def finishing(self, g):
        return self.split(jnp.maximum(g - 1, 0))


def _attn_scratch(nk, pw, v_rows, per_out, score_pad=0):
    return [
        pltpu.VMEM((2, nk, pw + score_pad), F32),
        pltpu.VMEM((2, 8, pw), F32),
        pltpu.VMEM((per_out, v_rows, pw), F32),
        pltpu.VMEM((2, v_rows + BF16_ROWS, nk), BF16),
    ]


def _attn_pipeline(qt_ref, k_refs, vt_refs, s_scr, m_scr, on_scr, vt_scr, *, panels, emit):
    g = pl.program_id(0)
    pw = qt_ref.shape[-1]
    total, per_head, per_out = panels.total, panels.per_head, panels.per_out
    q_load = jnp.minimum(g, total - 1)
    q_done = jnp.maximum(g - 1, 0)

    v_rows = on_scr.shape[1]

    @pl.when((g < total) & (q_load % per_head == 0))
    def _():
        dst = vt_scr.at[(q_load // per_head) & 1]
        c = 0
        for vt_ref in vt_refs:
            vt = vt_ref[...]
            dst[0:v_rows, c:c + vt.shape[-1]] = vt.reshape(vt.shape[-2:])
            c += vt.shape[-1]
        dst[v_rows:, :] = jnp.ones((dst.shape[0] - v_rows, dst.shape[1]), BF16)

    key_chunks = [(k_ref, c0) for k_ref in k_refs for c0 in range(0, k_ref.shape[0], MXU_DIM)]

    def phase(score_buf, value_buf):
        if score_buf is not None:
            qt = qt_ref[...].reshape(qt_ref.shape[-2:])
        if value_buf is not None:
            m_prev = m_scr[value_buf][0:1]
            vt = vt_scr.at[(q_done // per_head) & 1]
        m, acc = None, None
        for c, (k_ref, c0) in enumerate(key_chunks):
            rows = slice(c * MXU_DIM, (c + 1) * MXU_DIM)
            if value_buf is not None:
                pt = jnp.exp2(s_scr[value_buf, rows, 0:pw] - m_prev).astype(BF16)
                d = jnp.dot(vt[:, rows], pt, preferred_element_type=F32)
                acc = d if acc is None else acc + d
            if score_buf is not None:
                s = jnp.dot(k_ref[c0:c0 + MXU_DIM, :], qt, preferred_element_type=F32)
                s_scr[score_buf, rows, 0:pw] = s
                mc = jnp.max(s, axis=0, keepdims=True)
                m = mc if m is None else jnp.maximum(m, mc)
        if score_buf is not None:
            m_scr[score_buf] = jnp.broadcast_to(m, m_scr.shape[1:])
        if value_buf is not None:
            on_scr[q_done % per_out] = acc[0:v_rows] / acc[v_rows:v_rows + 1]

    @pl.when(g == 0)
    def _():
        phase(0, None)

    for parity in range(2):
        @pl.when((g > 0) & (g < total) & ((g & 1) == parity))
        def _():
            phase(parity, 1 - parity)

    @pl.when(g == total)
    def _():
        phase(None, (total - 1) % 2)

    @pl.when((g > 0) & (q_done % per_out == per_out - 1))
    def _():
        emit()


def _diff_attn_kernel(*refs, n_seg, panels, lam_init):
    it = iter(refs)
    lam_ref, subg_ref, qt_ref = next(it), next(it), next(it)
    k_refs = [next(it) for _ in range(n_seg)]
    vt_refs = [next(it) for _ in range(n_seg)]
    o_ref, s_scr, m_scr, on_scr, vt_scr = (next(it) for _ in range(5))

    def emit():
        lp = lam_ref[...]
        lam = (jnp.exp(jnp.sum(lp[0:1] * lp[1:2], axis=-1, keepdims=True))
               - jnp.exp(jnp.sum(lp[2:3] * lp[3:4], axis=-1, keepdims=True)) + lam_init)
        o = on_scr[0] - lam * on_scr[1]
        ms = jnp.mean(o * o, axis=0, keepdims=True)
        o = o * lax.rsqrt(ms + EPS) * subg_ref[...] * (1.0 - lam_init)
        o_ref[...] = o.T.astype(BF16)

    _attn_pipeline(qt_ref, k_refs, vt_refs, s_scr, m_scr, on_scr, vt_scr,
                   panels=panels, emit=emit)


def _diff_attn(dqt, ks, vts, lam_params, sub_gain_col, lam_init, *, pw):
    n_b, n_h, per_out, _, lq = dqt.shape
    lks = tuple(vt.shape[-1] for vt in vts)
    vw = 2 * HEAD_DIM
    panels = _Panels(n_b, n_h, lq, pw, per_out)

    def qt_map(g):
        b, h, j, n = panels.loading(g)
        return b, h, j, 0, n

    def k_map(g):
        b, h, _, _ = panels.loading(g)
        return b, h

    def vt_map(g):
        b, h, _, _ = panels.loading(g)
        return b, h, 0, 0

    def out_map(g):
        b, h, _, n = panels.finishing(g)
        return b * panels.tiles + n, h

    in_specs = [
        _resident(lam_params.shape),
        _resident(sub_gain_col.shape),
        pl.BlockSpec((1, 1, 1, LANES, pw), qt_map),
    ]
    in_specs += [pl.BlockSpec((lk, LANES), k_map) for lk in lks]
    in_specs += [pl.BlockSpec((1, 1, vw, lk), vt_map) for lk in lks]
    return pl.pallas_call(
        functools.partial(_diff_attn_kernel, n_seg=len(lks), panels=panels, lam_init=lam_init),
        grid=(panels.total + 1,),
        in_specs=in_specs,
        out_specs=pl.BlockSpec((pw, vw), out_map),
        out_shape=jax.ShapeDtypeStruct((n_b * lq, DIFF_WIDTH), BF16),
        scratch_shapes=_attn_scratch(sum(lks), pw, vw, per_out, score_pad=LANES),
        compiler_params=_cparams(("arbitrary",)),
        name="diff_attn",
    )(lam_params, sub_gain_col, dqt, *ks, *vts)


def _gqa_attn_kernel(*refs, n_seg, panels):
    it = iter(refs)
    qt_ref = next(it)
    k_refs = [next(it) for _ in range(n_seg)]
    vt_refs = [next(it) for _ in range(n_seg)]
    o_ref, s_scr, m_scr, on_scr, vt_scr = (next(it) for _ in range(5))

    def emit():
        pairs = [jnp.concatenate([on_scr[2 * i], on_scr[2 * i + 1]], axis=0).T
                 for i in range(GQA_GROUP // 2)]
        o_ref[...] = jnp.concatenate(pairs, axis=1).astype(BF16)

    _attn_pipeline(qt_ref, k_refs, vt_refs, s_scr, m_scr, on_scr, vt_scr,
                   panels=panels, emit=emit)


def _gqa_attn(gqt, ks, vts, *, pw):
    n_b, n_h, per_out, _, lq = gqt.shape
    lks = tuple(vt.shape[-1] for vt in vts)
    panels = _Panels(n_b, n_h, lq, pw, per_out)

    def qt_map(g):
        b, h, j, n = panels.loading(g)
        return b, h, j, 0, n

    def k_map(g):
        return panels.loading(g)[0], 0

    def vt_map(g):
        b, h, _, _ = panels.loading(g)
        return b, h, 0

    def out_map(g):
        b, h, _, n = panels.finishing(g)
        return b * panels.tiles + n, h

    in_specs = [pl.BlockSpec((1, 1, 1, LANES, pw), qt_map)]
    in_specs += [pl.BlockSpec((lk, LANES), k_map) for lk in lks]
    in_specs += [pl.BlockSpec((1, HEAD_DIM, lk), vt_map) for lk in lks]
    return pl.pallas_call(
        functools.partial(_gqa_attn_kernel, n_seg=len(lks), panels=panels),
        grid=(panels.total + 1,),
        in_specs=in_specs,
        out_specs=pl.BlockSpec((pw, GQA_GROUP * HEAD_DIM), out_map),
        out_shape=jax.ShapeDtypeStruct((n_b * lq, GQA_WIDTH), BF16),
        scratch_shapes=_attn_scratch(sum(lks), pw, HEAD_DIM, per_out),
        compiler_params=_cparams(("arbitrary",)),
        name="gqa_attn",
    )(gqt, *ks, *vts)


def _mix_ffn_kernel(x_ref, zp_ref, z_ref, zn_ref, icnt_ref, do_ref, go_ref, gate_ref, mod_ref,
                    wgrp_ref, pscale_ref, wpo_ref, wdo_ref, wgo_ref, wo_ref,
                    gpost_ref, gpre_ref, w1_ref, w2_ref, gpost_ffn_ref,
                    xo_ref, zext_scr, xn_scr, hf_scr, h_scr, *, seg_len, tm):
    i = pl.program_id(0)
    tps = seg_len // tm
    t = i % tps
    z = z_ref[...]
    zext_scr[0:POOL_HALO] = jnp.where(t > 0, zp_ref[...], 0.0)
    zext_scr[POOL_HALO:POOL_HALO + tm] = z
    zext_scr[POOL_HALO + tm:] = jnp.where(t < tps - 1, zn_ref[...], 0.0)
    mod = mod_ref[0, 0]
    sub = min(tm, SUB_ROWS)

    def pool_mixer(r0, _):
        mixed = []
        for g, w in enumerate(POOL_WINDOWS):
            cs = slice(g * POOL_GROUP_DIM, (g + 1) * POOL_GROUP_DIM)
            ssum = None
            for d in range(-(w // 2), w // 2):
                zz = zext_scr[POOL_HALO + d + r0:POOL_HALO + d + r0 + sub, cs]
                ssum = zz if ssum is None else ssum + zz
            pooled = ssum * icnt_ref[r0:r0 + sub, cs] - z_ref[r0:r0 + sub, cs]
            mixed.append(jnp.dot(pooled.astype(BF16), wgrp_ref[g], preferred_element_type=F32))
        return (jnp.concatenate(mixed, axis=1) * pscale_ref[...]).astype(BF16)

    def gated_branches(r0, pool_out):
        rs = slice(r0, r0 + sub)

        def gate(k):
            return gate_ref[rs, k * D_MODEL:(k + 1) * D_MODEL].astype(F32)

        merged = gate(0) * jnp.dot(pool_out, wpo_ref[...], preferred_element_type=F32)
        merged += gate(1) * jnp.dot(do_ref[rs, :], wdo_ref[...], preferred_element_type=F32)
        merged += gate(2) * jnp.dot(go_ref[rs, :], wgo_ref[...], preferred_element_type=F32)
        return merged.astype(BF16)

    def out_proj(r0, merged):
        return jnp.dot(merged, wo_ref[...], preferred_element_type=F32)

    def residual(r0, y):
        rs = slice(r0, r0 + sub)
        xn = x_ref[rs, :] + mod[2:3] * _rms(y, gpost_ref[...])
        xn_scr[rs, :] = xn
        hf_scr[rs, :] = (_rms(xn, gpre_ref[...]) * (1.0 + mod[4:5]) + mod[3:4]).astype(BF16)

    def ffn_up(r0, _):
        rs = slice(r0, r0 + sub)
        hf = hf_scr[rs, :]
        for j in range(D_FF // PROJ_CHUNK):
            cols = slice(j * PROJ_CHUNK, (j + 1) * PROJ_CHUNK)
            h = jnp.maximum(jnp.dot(hf, w1_ref[:, cols], preferred_element_type=F32), 0.0)
            h_scr[rs, cols] = (h * h).astype(BF16)

    def ffn_down(r0, _):
        rs = slice(r0, r0 + sub)
        y = jnp.dot(h_scr[rs, :], w2_ref[...], preferred_element_type=F32)
        xo_ref[rs, :] = xn_scr[rs, :] + mod[5:6] * _rms(y, gpost_ffn_ref[...])

    _staggered([pool_mixer, gated_branches, out_proj, residual, ffn_up, ffn_down],
               range(0, tm, sub))


def _pool_inverse_counts(seg_len, tm):
    tps = seg_len // tm
    t = jnp.arange(seg_len)
    if tps > 2:
        t = jnp.concatenate([t[:2 * tm], t[-tm:]])
    cols = []
    for w in POOL_WINDOWS:
        cnt = jnp.minimum(t + w // 2, seg_len) - jnp.maximum(t - w // 2, 0)
        cols.append(jnp.broadcast_to((1.0 / cnt.astype(F32))[:, None], (t.shape[0], POOL_GROUP_DIM)))
    table = jnp.concatenate(cols, axis=1)
    n_blocks = table.shape[0] // tm
    block = lambda pos: jnp.where(pos == tps - 1, n_blocks - 1, jnp.minimum(pos, 1))
    return table, block


def _mix_ffn(x, pool_in, diff_out, gqa_out, gates, mod, layer, mod_row_fn, w_grp, pool_scale,
             w_pool_out, w_diff_out, w_gqa_out, w_o, g_post, g_pre_ffn, w1, w2, g_post_ffn, *,
             seg_len, tm):
    n = x.shape[0]
    tps = seg_len // tm
    hb = tm // POOL_HALO
    n_hblk = n // POOL_HALO
    row = lambda i: (i, 0)
    icnt, icnt_block = _pool_inverse_counts(seg_len, tm)
    weights = (w_grp, pool_scale, w_pool_out, w_diff_out, w_gqa_out, w_o, g_post, g_pre_ffn,
               w1, w2, g_post_ffn)
    in_specs = [
        pl.BlockSpec((tm, D_MODEL), row),
        pl.BlockSpec((POOL_HALO, POOL_WIDTH), lambda i: (jnp.maximum(i * hb - 1, 0), 0)),
        pl.BlockSpec((tm, POOL_WIDTH), row),
        pl.BlockSpec((POOL_HALO, POOL_WIDTH), lambda i: (jnp.minimum((i + 1) * hb, n_hblk - 1), 0)),
        pl.BlockSpec((tm, POOL_WIDTH), lambda i: (icnt_block(i % tps), 0)),
        pl.BlockSpec((tm, DIFF_WIDTH), row),
        pl.BlockSpec((tm, GQA_WIDTH), row),
        pl.BlockSpec((tm, N_BRANCH * D_MODEL), row),
        pl.BlockSpec((1, 1, 6, D_MODEL), lambda i: (layer, mod_row_fn(i // tps), 0, 0)),
    ] + [_resident(w.shape) for w in weights]
    return pl.pallas_call(
        functools.partial(_mix_ffn_kernel, seg_len=seg_len, tm=tm),
        grid=(n // tm,),
        in_specs=in_specs,
        out_specs=pl.BlockSpec((tm, D_MODEL), row),
        out_shape=jax.ShapeDtypeStruct((n, D_MODEL), F32),
        scratch_shapes=[
            pltpu.VMEM((tm + 2 * POOL_HALO, POOL_WIDTH), F32),
            pltpu.VMEM((tm, D_MODEL), F32),
            pltpu.VMEM((tm, D_MODEL), BF16),
            pltpu.VMEM((tm, D_FF), BF16),
        ],
        compiler_params=_cparams(("arbitrary",)),
        name="mix_ffn",
    )(x, pool_in, pool_in, pool_in, icnt, diff_out, gqa_out, gates, mod, *weights)


def _rope_tables(seq):
    rows = seq // GRID_W
    row = jnp.repeat(jnp.arange(rows), GRID_W).astype(F32)
    col = jnp.tile(jnp.arange(GRID_W), rows).astype(F32)
    inv = 1.0 / (ROPE_BASE ** (jnp.arange(ROPE_PAIRS_PER_AXIS, dtype=F32) * 2.0 / ROPE_AXIS_DIM))
    ang = jnp.concatenate([row[:, None] * inv, col[:, None] * inv], axis=-1)
    cos, sin = jnp.cos(ang), jnp.sin(ang)
    return (jnp.tile(cos, (1, 4)),
            jnp.concatenate([-sin, -sin, sin, sin], axis=-1))


def _pair_layout(w, member_major):
    d = w.shape[0]
    if member_major:
        w5, perm = w.reshape(d, 2, -1, HALF_HEAD, 2), (0, 2, 4, 1, 3)
    else:
        w5, perm = w.reshape(d, -1, 2, HALF_HEAD, 2), (0, 1, 4, 2, 3)
    return w5.transpose(perm).reshape(d, -1)


def _in_proj_weights(w):
    seg = lambda a, b: w[:, a:b]
    return jnp.concatenate([
        seg(OFF_POOL, OFF_DQ),
        _pair_layout(seg(OFF_DQ, OFF_GQ), False),
        _pair_layout(seg(OFF_GQ, OFF_GATE), True),
        seg(OFF_GATE, OFF_DK),
        _pair_layout(seg(OFF_DK, OFF_DV), False),
        seg(OFF_DV, OFF_GK),
        _pair_layout(seg(OFF_GK, OFF_GV), False),
        seg(OFF_GV, IN_WIDTH),
    ], axis=1).astype(BF16)


def _pair_gain(g):
    return jnp.concatenate([g[0::2], g[0::2], g[1::2], g[1::2]]).reshape(1, LANES)


def kernel(x, c, ctx, c_ctx, w_mod, b_mod, g_pre_mix, g_post_mix, g_pre_ffn, g_post_ffn,
           w_in, w_pool_grp, pool_scale, lambda_q1, lambda_k1, lambda_q2, lambda_k2,
           diff_subln, gqa_q_norm, gqa_k_norm, w_pool_out, w_diff_out, w_gqa_out, w_o,
           w_ff1, w_ff2):
    batch, seq, d = x.shape
    ctx_len = ctx.shape[1]
    depth = w_mod.shape[0]
    assert d == D_MODEL and batch < MOD_ROWS and w_in.shape[-1] == IN_WIDTH
    assert GQA_KV_HEADS == 2 and ctx_len % MXU_DIM == 0

    lat_tm, ctx_tm = 512, ctx_len
    rope_tabs = _rope_tables(seq)
    c_all = jnp.zeros((MOD_ROWS, d), F32).at[:batch].set(c).at[batch].set(c_ctx)
    mod = _modulation(c_all, w_mod, b_mod)
    lat_row = lambda s: s
    ctx_row = lambda s: batch

    xl = x.reshape(batch * seq, d)
    xc = ctx.reshape(batch * ctx_len, d)
    row = lambda a: a.reshape(1, -1)

    for l in range(depth):
        last = l == depth - 1
        lam_init = 0.8 - 0.6 * math.exp(-0.3 * l)
        w_in_b = _in_proj_weights(w_in[l])
        qg2, kg2 = _pair_gain(gqa_q_norm[l]), _pair_gain(gqa_k_norm[l])
        lam_params = jnp.stack([lambda_q1[l], lambda_k1[l], lambda_q2[l], lambda_k2[l]])
        sub_gain = diff_subln[l].reshape(-1, 1)
        mix_w = (w_pool_grp[l].astype(BF16), row(pool_scale[l]), w_pool_out[l].astype(BF16),
                 w_diff_out[l].astype(BF16), w_gqa_out[l].astype(BF16), w_o[l].astype(BF16),
                 row(g_post_mix[l]), row(g_pre_ffn[l]),
                 w_ff1[l].astype(BF16), w_ff2[l].astype(BF16), row(g_post_ffn[l]))

        cp = _in_proj(xc, mod, l, ctx_row, row(g_pre_mix[l]), w_in_b, qg2, kg2, None,
                      seg_len=ctx_len, tm=ctx_tm, q_side=not last)
        c_dk, c_dvt, c_gk, c_gvt = cp[-4:]
        pool_in, dqt, gqt, gates, l_dk, l_dvt, l_gk, l_gvt = _in_proj(
            xl, mod, l, lat_row, row(g_pre_mix[l]), w_in_b, qg2, kg2, rope_tabs,
            seg_len=seq, tm=lat_tm, q_side=True)

        diff_out = _diff_attn(dqt, (c_dk, l_dk), (c_dvt, l_dvt), lam_params, sub_gain,
                              lam_init, pw=PANEL)
        gqa_out = _gqa_attn(gqt, (c_gk, l_gk), (c_gvt, l_gvt), pw=PANEL // 2)
        xl_next = _mix_ffn(xl, pool_in, diff_out, gqa_out, gates, mod, l, lat_row, *mix_w,
                           seg_len=seq, tm=lat_tm)

        if not last:
            c_pool, c_dqt, c_gqt, c_gates = cp[:4]
            c_diff = _diff_attn(c_dqt, (c_dk,), (c_dvt,), lam_params, sub_gain, lam_init,
                                pw=ctx_len)
            c_gqa = _gqa_attn(c_gqt, (c_gk,), (c_gvt,), pw=ctx_len)
            xc = _mix_ffn(xc, c_pool, c_diff, c_gqa, c_gates, mod, l, ctx_row, *mix_w,
                          seg_len=ctx_len, tm=ctx_tm)
        xl = xl_next
    return xl.reshape(batch, seq, d)
```

```python
import functools
import math

import jax
import jax.numpy as jnp
from jax import lax
from jax.experimental import pallas as pl
from jax.experimental.pallas import tpu as pltpu

F32 = jnp.float32
BF16 = jnp.bfloat16

D_MODEL = 1024
GRID_W = 64
HEAD_DIM = 64
HALF_HEAD = HEAD_DIM // 2
ATTN_SCALE = HEAD_DIM ** -0.5
LOG2E = math.log2(math.e)
ROPE_BASE = 10000.0
ROPE_AXIS_DIM = HEAD_DIM // 2
ROPE_PAIRS_PER_AXIS = ROPE_AXIS_DIM // 2
EPS = 1e-6

POOL_WINDOWS = (2, 4, 8, 16)
POOL_WIDTH = D_MODEL // 2
POOL_GROUP_DIM = POOL_WIDTH // len(POOL_WINDOWS)
POOL_HALO = max(POOL_WINDOWS) // 2

DIFF_HEADS = D_MODEL // 256
DIFF_WIDTH = DIFF_HEADS * 2 * HEAD_DIM
GQA_HEADS = D_MODEL // 128
GQA_KV_HEADS = GQA_HEADS // 4
GQA_GROUP = GQA_HEADS // GQA_KV_HEADS
GQA_WIDTH = GQA_HEADS * HEAD_DIM
GQA_KV_WIDTH = GQA_KV_HEADS * HEAD_DIM
N_BRANCH = 3
D_FF = 4 * D_MODEL

OFF_POOL = 0
OFF_DQ = OFF_POOL + POOL_WIDTH
OFF_GQ = OFF_DQ + DIFF_WIDTH
OFF_GATE = OFF_GQ + GQA_WIDTH
OFF_DK = OFF_GATE + N_BRANCH * D_MODEL
OFF_DV = OFF_DK + DIFF_WIDTH
OFF_GK = OFF_DV + DIFF_WIDTH
OFF_GV = OFF_GK + GQA_KV_WIDTH
IN_WIDTH = OFF_GV + GQA_KV_WIDTH

LANES = 128
MXU_DIM = 256
PANEL = 4 * MXU_DIM
BF16_ROWS = 16
PROJ_CHUNK = 512
SUB_ROWS = 256
VMEM_LIMIT = 56 * 1024 * 1024

MOD_ROWS = 8


def _cparams(sem):
    return pltpu.CompilerParams(dimension_semantics=sem, vmem_limit_bytes=VMEM_LIMIT)


def _layer_param(arr, layer):
    tail = (0,) * (arr.ndim - 1)
    return pl.BlockSpec((1,) + arr.shape[1:], lambda *_: (layer,) + tail,
                        pipeline_mode=pl.Buffered(1))


def _rms(x, g):
    ms = jnp.mean(x * x, axis=-1, keepdims=True)
    return x * lax.rsqrt(ms + EPS) * g


def _staggered(stages, row_starts):
    row_starts = list(row_starts)
    carry = {}
    for step in range(len(row_starts) + len(stages) - 1):
        for k in reversed(range(len(stages))):
            i = step - k
            if 0 <= i < len(row_starts):
                carry[i] = stages[k](row_starts[i], carry.get(i))


def _mod_kernel(c_ref, w_ref, b_ref, o_ref):
    c = c_ref[...]
    a = c * (1.0 / (1.0 + jnp.exp(-c)))
    o_ref[0] = jnp.dot(a.astype(BF16), w_ref[0].astype(BF16),
                       preferred_element_type=F32) + b_ref[0]


def _modulation(c_all, w_mod, b_mod):
    depth = w_mod.shape[0]
    nblk = w_mod.shape[2] // D_MODEL
    out = pl.pallas_call(
        _mod_kernel,
        grid=(depth, nblk),
        in_specs=[
            pl.BlockSpec((MOD_ROWS, D_MODEL), lambda l, j: (0, 0)),
            pl.BlockSpec((1, D_MODEL, D_MODEL), lambda l, j: (l, 0, j)),
            pl.BlockSpec((1, 1, D_MODEL), lambda l, j: (l, 0, j)),
        ],
        out_specs=pl.BlockSpec((1, MOD_ROWS, D_MODEL), lambda l, j: (l, 0, j)),
        out_shape=jax.ShapeDtypeStruct((depth, MOD_ROWS, nblk * D_MODEL), F32),
        compiler_params=_cparams(("arbitrary", "arbitrary")),
        name="modulation",
    )(c_all, w_mod, b_mod.reshape(depth, 1, nblk * D_MODEL))
    return out.reshape(depth, MOD_ROWS, nblk, D_MODEL)


def _rope(x, cos, sin_signed):
    return x * cos + pltpu.roll(x, HEAD_DIM, 1) * sin_signed


def _head_norm(x, gain2):
    lane = lax.broadcasted_iota(jnp.int32, x.shape, 1)
    first = (lane & HALF_HEAD) == 0
    x2 = x * x
    ms_a = jnp.sum(jnp.where(first, x2, 0.0), axis=-1, keepdims=True) * (1.0 / HEAD_DIM)
    ms_b = jnp.sum(jnp.where(first, 0.0, x2), axis=-1, keepdims=True) * (1.0 / HEAD_DIM)
    inv = jnp.where(first, lax.rsqrt(ms_a + EPS), lax.rsqrt(ms_b + EPS))
    return x * inv * gain2


def _in_proj_kernel(*refs, rope, q_side):
    it = iter(refs)
    x_ref, mod_ref = next(it), next(it)
    gpre_ref, w_ref, qg_ref, kg_ref = (next(it).at[0] for _ in range(4))
    cos_ref = next(it) if rope else None
    sin_ref = next(it) if rope else None
    if q_side:
        pool_ref, dqt_ref, gqt_ref, gate_ref = (next(it) for _ in range(4))
    dk_ref, dvt_ref, gk_ref, gvt_ref = (next(it) for _ in range(4))

    x = x_ref[...]
    mod = mod_ref[0, 0]
    h = _rms(x, gpre_ref[...]) * (1.0 + mod[1:2]) + mod[0:1]
    hb = h.astype(BF16)
    if rope:
        cos, sin = cos_ref[...], sin_ref[...]

    def proj(off, width):
        return jnp.dot(hb, w_ref[:, off:off + width], preferred_element_type=F32)

    def blocks(off, width):
        y = proj(off, width)
        return [y[:, j * LANES:(j + 1) * LANES] for j in range(width // LANES)]

    def store_heads_t(ref_a, ref_b, y):
        yt = y.T
        row = lax.broadcasted_iota(jnp.int32, yt.shape, 0)
        first = (row & HALF_HEAD) == 0
        ref_a[...] = jnp.where(first, yt, 0.0).astype(BF16)
        ref_b[...] = jnp.where(first, 0.0, yt).astype(BF16)

    q_scale = ATTN_SCALE * LOG2E

    def diff_keys():
        for hh, y in enumerate(blocks(OFF_DK, DIFF_WIDTH)):
            if rope:
                y = _rope(y, cos, sin)
            dk_ref[:, hh * LANES:(hh + 1) * LANES] = y.astype(BF16)

    def gqa_keys():
        (y,) = blocks(OFF_GK, GQA_KV_WIDTH)
        y = _head_norm(y, kg_ref[...])
        if rope:
            y = _rope(y, cos, sin)
        gk_ref[...] = y.astype(BF16)

    def diff_queries():
        for hh, y in enumerate(blocks(OFF_DQ, DIFF_WIDTH)):
            if rope:
                y = _rope(y, cos, sin)
            store_heads_t(dqt_ref.at[0, hh, 0], dqt_ref.at[0, hh, 1], y * q_scale)

    def gqa_queries():
        for jj, y in enumerate(blocks(OFF_GQ, GQA_WIDTH)):
            y = _head_norm(y, qg_ref[...])
            if rope:
                y = _rope(y, cos, sin)
            store_heads_t(gqt_ref.at[0, 0, jj], gqt_ref.at[0, 1, jj], y * q_scale)

    def values():
        for hh, y in enumerate(blocks(OFF_DV, DIFF_WIDTH)):
            dvt_ref[0, hh] = y.T.astype(BF16)
        (y,) = blocks(OFF_GV, GQA_KV_WIDTH)
        gvt_ref[0] = y.T.astype(BF16)

    def gates(j):
        gl = proj(OFF_GATE + j * PROJ_CHUNK, PROJ_CHUNK)
        gate_ref[:, j * PROJ_CHUNK:(j + 1) * PROJ_CHUNK] = (1.0 / (1.0 + jnp.exp(-gl))).astype(BF16)

    if q_side:
        n_gate = N_BRANCH * D_MODEL // PROJ_CHUNK
        order = [diff_keys, gqa_keys, 0, 1, diff_queries, 2, 3, gqa_queries, 4, 5, values]
        assert sorted(s for s in order if isinstance(s, int)) == list(range(n_gate))
        for step in order:
            if isinstance(step, int):
                gates(step)
            else:
                step()
        pool_ref[...] = proj(OFF_POOL, POOL_WIDTH)
    else:
        diff_keys()
        gqa_keys()
        values()


def _in_proj(x, mod, layer, mod_row_fn, g_pre, w_in, q_gain2, k_gain2, rope_tabs, *,
             seg_len, tm, q_side):
    n = x.shape[0]
    n_seg = n // seg_len
    tps = seg_len // tm
    rope = rope_tabs is not None
    seg = lambda i: i // tps
    pos = lambda i: i % tps
    row = lambda i: (i, 0)

    in_specs = [
        pl.BlockSpec((tm, D_MODEL), row),
        pl.BlockSpec((1, 1, 6, D_MODEL), lambda i: (layer, mod_row_fn(seg(i)), 0, 0)),
    ] + [_layer_param(a, layer) for a in (g_pre, w_in, q_gain2, k_gain2)]
    args = [x, mod, g_pre, w_in, q_gain2, k_gain2]
    if rope:
        in_specs += [pl.BlockSpec((tm, LANES), lambda i: (pos(i), 0))] * 2
        args += list(rope_tabs)

    out_shape, out_specs = [], []
    if q_side:
        out_shape += [
            jax.ShapeDtypeStruct((n, POOL_WIDTH), F32),
            jax.ShapeDtypeStruct((n_seg, DIFF_HEADS, 2, LANES, seg_len), BF16),
            jax.ShapeDtypeStruct((n_seg, GQA_KV_HEADS, GQA_GROUP, LANES, seg_len), BF16),
            jax.ShapeDtypeStruct((n, N_BRANCH * D_MODEL), BF16),
        ]
        out_specs += [
            pl.BlockSpec((tm, POOL_WIDTH), row),
            pl.BlockSpec((1, DIFF_HEADS, 2, LANES, tm), lambda i: (seg(i), 0, 0, 0, pos(i))),
            pl.BlockSpec((1, GQA_KV_HEADS, GQA_GROUP, LANES, tm),
                         lambda i: (seg(i), 0, 0, 0, pos(i))),
            pl.BlockSpec((tm, N_BRANCH * D_MODEL), row),
        ]
    out_shape += [
        jax.ShapeDtypeStruct((n, DIFF_WIDTH), BF16),
        jax.ShapeDtypeStruct((n_seg, DIFF_HEADS, 2 * HEAD_DIM, seg_len), BF16),
        jax.ShapeDtypeStruct((n, GQA_KV_WIDTH), BF16),
        jax.ShapeDtypeStruct((n_seg, GQA_KV_WIDTH, seg_len), BF16),
    ]
    out_specs += [
        pl.BlockSpec((tm, DIFF_WIDTH), row),
        pl.BlockSpec((1, DIFF_HEADS, 2 * HEAD_DIM, tm), lambda i: (seg(i), 0, 0, pos(i))),
        pl.BlockSpec((tm, GQA_KV_WIDTH), row),
        pl.BlockSpec((1, GQA_KV_WIDTH, tm), lambda i: (seg(i), 0, pos(i))),
    ]
    return pl.pallas_call(
        functools.partial(_in_proj_kernel, rope=rope, q_side=q_side),
        grid=(n // tm,),
        in_specs=in_specs,
        out_specs=out_specs,
        out_shape=out_shape,
        compiler_params=_cparams(("arbitrary",)),
        name="in_proj" + ("_rope" if rope else "") + ("" if q_side else "_kv"),
    )(*args)


class _Panels:
    def __init__(self, n_b, n_h, lq, pw, per_out):
        self.n_h, self.per_out, self.tiles = n_h, per_out, lq // pw
        self.per_head = per_out * self.tiles
        self.total = n_b * n_h * self.per_head

    def split(self, q):
        bh, loc = q // self.per_head, q % self.per_head
        return bh // self.n_h, bh % self.n_h, loc % self.per_out, loc // self.per_out

    def loading(self, g):
        return self.split(jnp.minimum(g, self.total - 1))

    def finishing(self, g):
        return self.split(jnp.maximum(g - 1, 0))


def _attn_scratch(nk, pw, v_rows, per_out):
    return [
        pltpu.VMEM((2, nk, pw), F32),
        pltpu.VMEM((2, 8, pw), F32),
        pltpu.VMEM((per_out, v_rows, pw), F32),
        pltpu.VMEM((2, v_rows + BF16_ROWS, nk), BF16),
    ]


def _attn_pipeline(qt_ref, k_refs, vt_refs, s_scr, m_scr, on_scr, vt_scr, *, panels, emit):
    g = pl.program_id(0)
    pw = qt_ref.shape[-1]
    total, per_head, per_out = panels.total, panels.per_head, panels.per_out
    q_load = jnp.minimum(g, total - 1)
    q_done = jnp.maximum(g - 1, 0)

    v_rows = on_scr.shape[1]

    @pl.when((g < total) & (q_load % per_head == 0))
    def _():
        dst = vt_scr.at[(q_load // per_head) & 1]
        c = 0
        for vt_ref in vt_refs:
            vt = vt_ref[...]
            dst[0:v_rows, c:c + vt.shape[-1]] = vt.reshape(vt.shape[-2:])
            c += vt.shape[-1]
        dst[v_rows:, :] = jnp.ones((dst.shape[0] - v_rows, dst.shape[1]), BF16)

    key_chunks = [(k_ref, c0) for k_ref in k_refs for c0 in range(0, k_ref.shape[0], MXU_DIM)]

    def phase(score_buf, value_buf):
        if score_buf is not None:
            qt = qt_ref[...].reshape(qt_ref.shape[-2:])
        if value_buf is not None:
            m_prev = m_scr[value_buf][0:1]
            vt = vt_scr.at[(q_done // per_head) & 1]
        m, acc = None, None
        for c, (k_ref, c0) in enumerate(key_chunks):
            rows = slice(c * MXU_DIM, (c + 1) * MXU_DIM)
            if value_buf is not None:
                pt = jnp.exp2(s_scr[value_buf, rows, 0:pw] - m_prev).astype(BF16)
                d = jnp.dot(vt[:, rows], pt, preferred_element_type=F32)
                acc = d if acc is None else acc + d
            if score_buf is not None:
                s = jnp.dot(k_ref[c0:c0 + MXU_DIM, :], qt, preferred_element_type=F32)
                s_scr[score_buf, rows, 0:pw] = s
                mc = jnp.max(s, axis=0, keepdims=True)
                m = mc if m is None else jnp.maximum(m, mc)
        if score_buf is not None:
            m_scr[score_buf] = jnp.broadcast_to(m, m_scr.shape[1:])
        if value_buf is not None:
            on_scr[q_done % per_out] = acc[0:v_rows] / acc[v_rows:v_rows + 1]

    @pl.when(g == 0)
    def _():
        phase(0, None)

    for parity in range(2):
        @pl.when((g > 0) & (g < total) & ((g & 1) == parity))
        def _():
            phase(parity, 1 - parity)

    @pl.when(g == total)
    def _():
        phase(None, (total - 1) % 2)

    @pl.when((g > 0) & (q_done % per_out == per_out - 1))
    def _():
        emit()


def _diff_attn_kernel(*refs, n_seg, panels, lam_init):
    it = iter(refs)
    lam_ref, subg_ref, qt_ref = next(it).at[0], next(it).at[0], next(it)
    k_refs = [next(it) for _ in range(n_seg)]
    vt_refs = [next(it) for _ in range(n_seg)]
    o_ref, s_scr, m_scr, on_scr, vt_scr = (next(it) for _ in range(5))

    def emit():
        lp = lam_ref[...]
        lam = (jnp.exp(jnp.sum(lp[0:1] * lp[1:2], axis=-1, keepdims=True))
               - jnp.exp(jnp.sum(lp[2:3] * lp[3:4], axis=-1, keepdims=True)) + lam_init)
        o = on_scr[0] - lam * on_scr[1]
        ms = jnp.mean(o * o, axis=0, keepdims=True)
        o = o * lax.rsqrt(ms + EPS) * subg_ref[...] * (1.0 - lam_init)
        o_ref[...] = o.T.astype(BF16)

    _attn_pipeline(qt_ref, k_refs, vt_refs, s_scr, m_scr, on_scr, vt_scr,
                   panels=panels, emit=emit)


def _diff_attn(dqt, ks, vts, lam_params, sub_gain_col, layer, lam_init, *, pw):
    n_b, n_h, per_out, _, lq = dqt.shape
    lks = tuple(vt.shape[-1] for vt in vts)
    vw = 2 * HEAD_DIM
    panels = _Panels(n_b, n_h, lq, pw, per_out)

    def qt_map(g):
        b, h, j, n = panels.loading(g)
        return b, h, j, 0, n

    def k_map(g):
        b, h, _, _ = panels.loading(g)
        return b, h

    def vt_map(g):
        b, h, _, _ = panels.loading(g)
        return b, h, 0, 0

    def out_map(g):
        b, h, _, n = panels.finishing(g)
        return b * panels.tiles + n, h

    in_specs = [
        _layer_param(lam_params, layer),
        _layer_param(sub_gain_col, layer),
        pl.BlockSpec((1, 1, 1, LANES, pw), qt_map),
    ]
    in_specs += [pl.BlockSpec((lk, LANES), k_map) for lk in lks]
    in_specs += [pl.BlockSpec((1, 1, vw, lk), vt_map) for lk in lks]
    return pl.pallas_call(
        functools.partial(_diff_attn_kernel, n_seg=len(lks), panels=panels, lam_init=lam_init),
        grid=(panels.total + 1,),
        in_specs=in_specs,
        out_specs=pl.BlockSpec((pw, vw), out_map),
        out_shape=jax.ShapeDtypeStruct((n_b * lq, DIFF_WIDTH), BF16),
        scratch_shapes=_attn_scratch(sum(lks), pw, vw, per_out),
        compiler_params=_cparams(("arbitrary",)),
        name="diff_attn",
    )(lam_params, sub_gain_col, dqt, *ks, *vts)


def _gqa_attn_kernel(*refs, n_seg, panels):
    it = iter(refs)
    qt_ref = next(it)
    k_refs = [next(it) for _ in range(n_seg)]
    vt_refs = [next(it) for _ in range(n_seg)]
    o_ref, s_scr, m_scr, on_scr, vt_scr = (next(it) for _ in range(5))

    def emit():
        pairs = [jnp.concatenate([on_scr[2 * i], on_scr[2 * i + 1]], axis=0).T
                 for i in range(GQA_GROUP // 2)]
        o_ref[...] = jnp.concatenate(pairs, axis=1).astype(BF16)

    _attn_pipeline(qt_ref, k_refs, vt_refs, s_scr, m_scr, on_scr, vt_scr,
                   panels=panels, emit=emit)


def _gqa_attn(gqt, ks, vts, *, pw):
    n_b, n_h, per_out, _, lq = gqt.shape
    lks = tuple(vt.shape[-1] for vt in vts)
    panels = _Panels(n_b, n_h, lq, pw, per_out)

    def qt_map(g):
        b, h, j, n = panels.loading(g)
        return b, h, j, 0, n

    def k_map(g):
        return panels.loading(g)[0], 0

    def vt_map(g):
        b, h, _, _ = panels.loading(g)
        return b, h, 0

    def out_map(g):
        b, h, _, n = panels.finishing(g)
        return b * panels.tiles + n, h

    in_specs = [pl.BlockSpec((1, 1, 1, LANES, pw), qt_map)]
    in_specs += [pl.BlockSpec((lk, LANES), k_map) for lk in lks]
    in_specs += [pl.BlockSpec((1, HEAD_DIM, lk), vt_map) for lk in lks]
    return pl.pallas_call(
        functools.partial(_gqa_attn_kernel, n_seg=len(lks), panels=panels),
        grid=(panels.total + 1,),
        in_specs=in_specs,
        out_specs=pl.BlockSpec((pw, GQA_GROUP * HEAD_DIM), out_map),
        out_shape=jax.ShapeDtypeStruct((n_b * lq, GQA_WIDTH), BF16),
        scratch_shapes=_attn_scratch(sum(lks), pw, HEAD_DIM, per_out),
        compiler_params=_cparams(("arbitrary",)),
        name="gqa_attn",
    )(gqt, *ks, *vts)


def _mix_ffn_kernel(x_ref, zp_ref, z_ref, zn_ref, icnt_ref, do_ref, go_ref, gate_ref, mod_ref,
                    wgrp_ref, pscale_ref, wpo_ref, wdo_ref, wgo_ref, wo_ref,
                    gpost_ref, gpre_ref, w1_ref, w2_ref, gpost_ffn_ref,
                    xo_ref, zext_scr, xn_scr, hf_scr, h_scr, *, seg_len, tm):
    (wgrp_ref, pscale_ref, wpo_ref, wdo_ref, wgo_ref, wo_ref, gpost_ref, gpre_ref, w1_ref,
     w2_ref, gpost_ffn_ref) = (r.at[0] for r in (
         wgrp_ref, pscale_ref, wpo_ref, wdo_ref, wgo_ref, wo_ref, gpost_ref, gpre_ref, w1_ref,
         w2_ref, gpost_ffn_ref))
    i = pl.program_id(0)
    tps = seg_len // tm
    t = i % tps
    z = z_ref[...]
    zext_scr[0:POOL_HALO] = jnp.where(t > 0, zp_ref[...], 0.0)
    zext_scr[POOL_HALO:POOL_HALO + tm] = z
    zext_scr[POOL_HALO + tm:] = jnp.where(t < tps - 1, zn_ref[...], 0.0)
    mod = mod_ref[0, 0]
    sub = min(tm, SUB_ROWS)

    def pool_mixer(r0, _):
        mixed = []
        for g, w in enumerate(POOL_WINDOWS):
            cs = slice(g * POOL_GROUP_DIM, (g + 1) * POOL_GROUP_DIM)
            ssum = None
            for d in range(-(w // 2), w // 2):
                zz = zext_scr[POOL_HALO + d + r0:POOL_HALO + d + r0 + sub, cs]
                ssum = zz if ssum is None else ssum + zz
            pooled = ssum * icnt_ref[r0:r0 + sub, cs] - z_ref[r0:r0 + sub, cs]
            mixed.append(jnp.dot(pooled.astype(BF16), wgrp_ref[g], preferred_element_type=F32))
        return (jnp.concatenate(mixed, axis=1) * pscale_ref[...]).astype(BF16)

    def gated_branches(r0, pool_out):
        rs = slice(r0, r0 + sub)

        def gate(k):
            return gate_ref[rs, k * D_MODEL:(k + 1) * D_MODEL].astype(F32)

        merged = gate(0) * jnp.dot(pool_out, wpo_ref[...], preferred_element_type=F32)
        merged += gate(1) * jnp.dot(do_ref[rs, :], wdo_ref[...], preferred_element_type=F32)
        merged += gate(2) * jnp.dot(go_ref[rs, :], wgo_ref[...], preferred_element_type=F32)
        return merged.astype(BF16)

    def out_proj(r0, merged):
        return jnp.dot(merged, wo_ref[...], preferred_element_type=F32)

    def residual(r0, y):
        rs = slice(r0, r0 + sub)
        xn = x_ref[rs, :] + mod[2:3] * _rms(y, gpost_ref[...])
        xn_scr[rs, :] = xn
        hf_scr[rs, :] = (_rms(xn, gpre_ref[...]) * (1.0 + mod[4:5]) + mod[3:4]).astype(BF16)

    def ffn_up(r0, _):
        rs = slice(r0, r0 + sub)
        hf = hf_scr[rs, :]
        for j in range(D_FF // PROJ_CHUNK):
            cols = slice(j * PROJ_CHUNK, (j + 1) * PROJ_CHUNK)
            h = jnp.maximum(jnp.dot(hf, w1_ref[:, cols], preferred_element_type=F32), 0.0)
            h_scr[rs, cols] = (h * h).astype(BF16)

    def ffn_down(r0, _):
        rs = slice(r0, r0 + sub)
        y = jnp.dot(h_scr[rs, :], w2_ref[...], preferred_element_type=F32)
        xo_ref[rs, :] = xn_scr[rs, :] + mod[5:6] * _rms(y, gpost_ffn_ref[...])

    _staggered([pool_mixer, gated_branches, out_proj, residual, ffn_up, ffn_down],
               range(0, tm, sub))


def _pool_inverse_counts(seg_len, tm):
    tps = seg_len // tm
    t = jnp.arange(seg_len)
    if tps > 2:
        t = jnp.concatenate([t[:2 * tm], t[-tm:]])
    cols = []
    for w in POOL_WINDOWS:
        cnt = jnp.minimum(t + w // 2, seg_len) - jnp.maximum(t - w // 2, 0)
        cols.append(jnp.broadcast_to((1.0 / cnt.astype(F32))[:, None], (t.shape[0], POOL_GROUP_DIM)))
    table = jnp.concatenate(cols, axis=1)
    n_blocks = table.shape[0] // tm
    block = lambda pos: jnp.where(pos == tps - 1, n_blocks - 1, jnp.minimum(pos, 1))
    return table, block


def _mix_ffn(x, pool_in, diff_out, gqa_out, gates, mod, layer, mod_row_fn, inv_counts, weights,
             *, seg_len, tm):
    n = x.shape[0]
    tps = seg_len // tm
    hb = tm // POOL_HALO
    n_hblk = n // POOL_HALO
    row = lambda i: (i, 0)
    icnt, icnt_block = inv_counts
    in_specs = [
        pl.BlockSpec((tm, D_MODEL), row),
        pl.BlockSpec((POOL_HALO, POOL_WIDTH), lambda i: (jnp.maximum(i * hb - 1, 0), 0)),
        pl.BlockSpec((tm, POOL_WIDTH), row),
        pl.BlockSpec((POOL_HALO, POOL_WIDTH), lambda i: (jnp.minimum((i + 1) * hb, n_hblk - 1), 0)),
        pl.BlockSpec((tm, POOL_WIDTH), lambda i: (icnt_block(i % tps), 0)),
        pl.BlockSpec((tm, DIFF_WIDTH), row),
        pl.BlockSpec((tm, GQA_WIDTH), row),
        pl.BlockSpec((tm, N_BRANCH * D_MODEL), row),
        pl.BlockSpec((1, 1, 6, D_MODEL), lambda i: (layer, mod_row_fn(i // tps), 0, 0)),
    ] + [_layer_param(w, layer) for w in weights]
    return pl.pallas_call(
        functools.partial(_mix_ffn_kernel, seg_len=seg_len, tm=tm),
        grid=(n // tm,),
        in_specs=in_specs,
        out_specs=pl.BlockSpec((tm, D_MODEL), row),
        out_shape=jax.ShapeDtypeStruct((n, D_MODEL), F32),
        scratch_shapes=[
            pltpu.VMEM((tm + 2 * POOL_HALO, POOL_WIDTH), F32),
            pltpu.VMEM((tm, D_MODEL), F32),
            pltpu.VMEM((tm, D_MODEL), BF16),
            pltpu.VMEM((tm, D_FF), BF16),
        ],
        compiler_params=_cparams(("arbitrary",)),
        name="mix_ffn",
    )(x, pool_in, pool_in, pool_in, icnt, diff_out, gqa_out, gates, mod, *weights)


def _rope_tables(seq):
    rows = seq // GRID_W
    row = jnp.repeat(jnp.arange(rows), GRID_W).astype(F32)
    col = jnp.tile(jnp.arange(GRID_W), rows).astype(F32)
    inv = 1.0 / (ROPE_BASE ** (jnp.arange(ROPE_PAIRS_PER_AXIS, dtype=F32) * 2.0 / ROPE_AXIS_DIM))
    ang = jnp.concatenate([row[:, None] * inv, col[:, None] * inv], axis=-1)
    cos, sin = jnp.cos(ang), jnp.sin(ang)
    return (jnp.tile(cos, (1, 4)),
            jnp.concatenate([-sin, -sin, sin, sin], axis=-1))


def _pair_layout(w, member_major):
    lead = w.shape[:-1]
    if member_major:
        w5, perm = w.reshape(-1, 2, w.shape[-1] // LANES, HALF_HEAD, 2), (0, 2, 4, 1, 3)
    else:
        w5, perm = w.reshape(-1, w.shape[-1] // LANES, 2, HALF_HEAD, 2), (0, 1, 4, 2, 3)
    return w5.transpose(perm).reshape(lead + (-1,))


def _in_proj_weights(w):
    seg = lambda a, b: w[..., a:b]
    return jnp.concatenate([
        seg(OFF_POOL, OFF_DQ),
        _pair_layout(seg(OFF_DQ, OFF_GQ), False),
        _pair_layout(seg(OFF_GQ, OFF_GATE), True),
        seg(OFF_GATE, OFF_DK),
        _pair_layout(seg(OFF_DK, OFF_DV), False),
        seg(OFF_DV, OFF_GK),
        _pair_layout(seg(OFF_GK, OFF_GV), False),
        seg(OFF_GV, IN_WIDTH),
    ], axis=-1).astype(BF16)


def _pair_gain(g):
    ev, od = g[:, 0::2], g[:, 1::2]
    return jnp.concatenate([ev, ev, od, od], axis=-1)[:, None, :]


def kernel(x, c, ctx, c_ctx, w_mod, b_mod, g_pre_mix, g_post_mix, g_pre_ffn, g_post_ffn,
           w_in, w_pool_grp, pool_scale, lambda_q1, lambda_k1, lambda_q2, lambda_k2,
           diff_subln, gqa_q_norm, gqa_k_norm, w_pool_out, w_diff_out, w_gqa_out, w_o,
           w_ff1, w_ff2):
    batch, seq, d = x.shape
    ctx_len = ctx.shape[1]
    depth = w_mod.shape[0]
    assert d == D_MODEL and batch < MOD_ROWS and w_in.shape[-1] == IN_WIDTH
    assert GQA_KV_HEADS == 2 and ctx_len % MXU_DIM == 0

    lat_tm, ctx_tm = 512, ctx_len
    rope_tabs = _rope_tables(seq)
    c_all = jnp.zeros((MOD_ROWS, d), F32).at[:batch].set(c).at[batch].set(c_ctx)
    mod = _modulation(c_all, w_mod, b_mod)
    lat_row = lambda s: s
    ctx_row = lambda s: batch

    xl = x.reshape(batch * seq, d)
    xc = ctx.reshape(batch * ctx_len, d)

    rows = lambda a: a[:, None, :]
    w_in_b = _in_proj_weights(w_in)
    g_pre = rows(g_pre_mix)
    qg2, kg2 = _pair_gain(gqa_q_norm), _pair_gain(gqa_k_norm)
    lam_params = jnp.stack([lambda_q1, lambda_k1, lambda_q2, lambda_k2], axis=1)
    sub_gain = diff_subln[:, :, None]
    mix_w = (w_pool_grp.astype(BF16), rows(pool_scale), w_pool_out.astype(BF16),
             w_diff_out.astype(BF16), w_gqa_out.astype(BF16), w_o.astype(BF16),
             rows(g_post_mix), rows(g_pre_ffn),
             w_ff1.astype(BF16), w_ff2.astype(BF16), rows(g_post_ffn))
    lat_counts = _pool_inverse_counts(seq, lat_tm)
    ctx_counts = _pool_inverse_counts(ctx_len, ctx_tm)

    for l in range(depth):
        last = l == depth - 1
        lam_init = 0.8 - 0.6 * math.exp(-0.3 * l)

        cp = _in_proj(xc, mod, l, ctx_row, g_pre, w_in_b, qg2, kg2, None,
                      seg_len=ctx_len, tm=ctx_tm, q_side=not last)
        c_dk, c_dvt, c_gk, c_gvt = cp[-4:]
        pool_in, dqt, gqt, gates, l_dk, l_dvt, l_gk, l_gvt = _in_proj(
            xl, mod, l, lat_row, g_pre, w_in_b, qg2, kg2, rope_tabs,
            seg_len=seq, tm=lat_tm, q_side=True)

        diff_out = _diff_attn(dqt, (c_dk, l_dk), (c_dvt, l_dvt), lam_params, sub_gain, l,
                              lam_init, pw=PANEL)
        gqa_out = _gqa_attn(gqt, (c_gk, l_gk), (c_gvt, l_gvt), pw=PANEL // 2)
        xl_next = _mix_ffn(xl, pool_in, diff_out, gqa_out, gates, mod, l, lat_row, lat_counts,
                           mix_w, seg_len=seq, tm=lat_tm)

        if not last:
            c_pool, c_dqt, c_gqt, c_gates = cp[:4]
            c_diff = _diff_attn(c_dqt, (c_dk,), (c_dvt,), lam_params, sub_gain, l, lam_init,
                                pw=ctx_len)
            c_gqa = _gqa_attn(c_gqt, (c_gk,), (c_gvt,), pw=ctx_len)
            xc = _mix_ffn(xc, c_pool, c_diff, c_gqa, c_gates, mod, l, ctx_row, ctx_counts,
                          mix_w, seg_len=ctx_len, tm=ctx_tm)
        xl = xl_next
    return xl.reshape(batch, seq, d)
```

```python
import functools
import math

import jax
import jax.numpy as jnp
from jax import lax
from jax.experimental import pallas as pl
from jax.experimental.pallas import tpu as pltpu

F32 = jnp.float32
BF16 = jnp.bfloat16

D_MODEL = 1024
GRID_W = 64
HEAD_DIM = 64
HALF_HEAD = HEAD_DIM // 2
ATTN_SCALE = HEAD_DIM ** -0.5
LOG2E = math.log2(math.e)
ROPE_BASE = 10000.0
ROPE_AXIS_DIM = HEAD_DIM // 2
ROPE_PAIRS_PER_AXIS = ROPE_AXIS_DIM // 2
EPS = 1e-6

POOL_WINDOWS = (2, 4, 8, 16)
POOL_WIDTH = D_MODEL // 2
POOL_GROUP_DIM = POOL_WIDTH // len(POOL_WINDOWS)
POOL_HALO = max(POOL_WINDOWS) // 2

DIFF_HEADS = D_MODEL // 256
DIFF_WIDTH = DIFF_HEADS * 2 * HEAD_DIM
GQA_HEADS = D_MODEL // 128
GQA_KV_HEADS = GQA_HEADS // 4
GQA_GROUP = GQA_HEADS // GQA_KV_HEADS
GQA_WIDTH = GQA_HEADS * HEAD_DIM
GQA_KV_WIDTH = GQA_KV_HEADS * HEAD_DIM
N_BRANCH = 3
D_FF = 4 * D_MODEL

OFF_POOL = 0
OFF_DQ = OFF_POOL + POOL_WIDTH
OFF_GQ = OFF_DQ + DIFF_WIDTH
OFF_GATE = OFF_GQ + GQA_WIDTH
OFF_DK = OFF_GATE + N_BRANCH * D_MODEL
OFF_DV = OFF_DK + DIFF_WIDTH
OFF_GK = OFF_DV + DIFF_WIDTH
OFF_GV = OFF_GK + GQA_KV_WIDTH
IN_WIDTH = OFF_GV + GQA_KV_WIDTH

LANES = 128
MXU_DIM = 256
PANEL = 4 * MXU_DIM
BF16_ROWS = 16
PROJ_CHUNK = 512
SUB_ROWS = 256
VMEM_LIMIT = 56 * 1024 * 1024

MOD_ROWS = 8


def _cparams(sem):
    return pltpu.CompilerParams(dimension_semantics=sem, vmem_limit_bytes=VMEM_LIMIT)


def _resident(shape):
    nd = len(shape)
    return pl.BlockSpec(shape, lambda *_: (0,) * nd, pipeline_mode=pl.Buffered(1))


def _rms(x, g):
    ms = jnp.mean(x * x, axis=-1, keepdims=True)
    return x * lax.rsqrt(ms + EPS) * g


def _staggered(stages, row_starts):
    row_starts = list(row_starts)
    carry = {}
    for step in range(len(row_starts) + len(stages) - 1):
        for k in reversed(range(len(stages))):
            i = step - k
            if 0 <= i < len(row_starts):
                carry[i] = stages[k](row_starts[i], carry.get(i))


def _mod_kernel(c_ref, w_ref, b_ref, o_ref):
    c = c_ref[...]
    a = c * (1.0 / (1.0 + jnp.exp(-c)))
    o_ref[0] = jnp.dot(a.astype(BF16), w_ref[0].astype(BF16),
                       preferred_element_type=F32) + b_ref[0]


def _modulation(c_all, w_mod, b_mod):
    depth = w_mod.shape[0]
    nblk = w_mod.shape[2] // D_MODEL
    out = pl.pallas_call(
        _mod_kernel,
        grid=(depth, nblk),
        in_specs=[
            pl.BlockSpec((MOD_ROWS, D_MODEL), lambda l, j: (0, 0)),
            pl.BlockSpec((1, D_MODEL, D_MODEL), lambda l, j: (l, 0, j)),
            pl.BlockSpec((1, 1, D_MODEL), lambda l, j: (l, 0, j)),
        ],
        out_specs=pl.BlockSpec((1, MOD_ROWS, D_MODEL), lambda l, j: (l, 0, j)),
        out_shape=jax.ShapeDtypeStruct((depth, MOD_ROWS, nblk * D_MODEL), F32),
        compiler_params=_cparams(("arbitrary", "arbitrary")),
        name="modulation",
    )(c_all, w_mod, b_mod.reshape(depth, 1, nblk * D_MODEL))
    return out.reshape(depth, MOD_ROWS, nblk, D_MODEL)


def _rope(x, cos, sin_signed):
    return x * cos + pltpu.roll(x, HEAD_DIM, 1) * sin_signed


def _head_norm(x, gain2):
    lane = lax.broadcasted_iota(jnp.int32, x.shape, 1)
    first = (lane & HALF_HEAD) == 0
    x2 = x * x
    ms_a = jnp.sum(jnp.where(first, x2, 0.0), axis=-1, keepdims=True) * (1.0 / HEAD_DIM)
    ms_b = jnp.sum(jnp.where(first, 0.0, x2), axis=-1, keepdims=True) * (1.0 / HEAD_DIM)
    inv = jnp.where(first, lax.rsqrt(ms_a + EPS), lax.rsqrt(ms_b + EPS))
    return x * inv * gain2


def _in_proj_kernel(*refs, rope, q_side):
    it = iter(refs)
    x_ref, mod_ref, gpre_ref, w_ref, qg_ref, kg_ref = (next(it) for _ in range(6))
    cos_ref = next(it) if rope else None
    sin_ref = next(it) if rope else None
    if q_side:
        pool_ref, dqt_ref, gqt_ref, gate_ref = (next(it) for _ in range(4))
    dk_ref, dvt_ref, gk_ref, gvt_ref = (next(it) for _ in range(4))
    hb_scr = next(it)

    tm = x_ref.shape[0]
    sub = min(tm, SUB_ROWS)
    mod = mod_ref[0, 0]
    q_scale = ATTN_SCALE * LOG2E

    def pre_norm(r0, _):
        rs = slice(r0, r0 + sub)
        h = _rms(x_ref[rs, :], gpre_ref[...]) * (1.0 + mod[1:2]) + mod[0:1]
        hb_scr[rs, :] = h.astype(BF16)

    def project(r0, _):
        rs = slice(r0, r0 + sub)
        hb = hb_scr[rs, :]
        if rope:
            cos, sin = cos_ref[rs, :], sin_ref[rs, :]

        def proj(off, width):
            return jnp.dot(hb, w_ref[:, off:off + width], preferred_element_type=F32)

        def blocks(off, width):
            y = proj(off, width)
            return [y[:, j * LANES:(j + 1) * LANES] for j in range(width // LANES)]

        def store_heads_t(ref, idx_a, idx_b, y):
            yt = y.T
            row = lax.broadcasted_iota(jnp.int32, yt.shape, 0)
            first = (row & HALF_HEAD) == 0
            ref[idx_a + (slice(None), rs)] = jnp.where(first, yt, 0.0).astype(BF16)
            ref[idx_b + (slice(None), rs)] = jnp.where(first, 0.0, yt).astype(BF16)

        def diff_keys():
            for hh, y in enumerate(blocks(OFF_DK, DIFF_WIDTH)):
                if rope:
                    y = _rope(y, cos, sin)
                dk_ref[rs, hh * LANES:(hh + 1) * LANES] = y.astype(BF16)

        def gqa_keys():
            (y,) = blocks(OFF_GK, GQA_KV_WIDTH)
            y = _head_norm(y, kg_ref[...])
            if rope:
                y = _rope(y, cos, sin)
            gk_ref[rs, :] = y.astype(BF16)

        def diff_queries():
            for hh, y in enumerate(blocks(OFF_DQ, DIFF_WIDTH)):
                if rope:
                    y = _rope(y, cos, sin)
                store_heads_t(dqt_ref, (0, hh, 0), (0, hh, 1), y * q_scale)

        def gqa_queries():
            for jj, y in enumerate(blocks(OFF_GQ, GQA_WIDTH)):
                y = _head_norm(y, qg_ref[...])
                if rope:
                    y = _rope(y, cos, sin)
                store_heads_t(gqt_ref, (0, 0, jj), (0, 1, jj), y * q_scale)

        def values():
            for hh, y in enumerate(blocks(OFF_DV, DIFF_WIDTH)):
                dvt_ref[0, hh, :, rs] = y.T.astype(BF16)
            (y,) = blocks(OFF_GV, GQA_KV_WIDTH)
            gvt_ref[0, :, rs] = y.T.astype(BF16)

        def gates(j):
            gl = proj(OFF_GATE + j * PROJ_CHUNK, PROJ_CHUNK)
            gate_ref[rs, j * PROJ_CHUNK:(j + 1) * PROJ_CHUNK] = (
                1.0 / (1.0 + jnp.exp(-gl))).astype(BF16)

        if q_side:
            n_gate = N_BRANCH * D_MODEL // PROJ_CHUNK
            order = [diff_keys, gqa_keys, 0, 1, diff_queries, 2, 3, gqa_queries, 4, 5, values]
            assert sorted(s for s in order if isinstance(s, int)) == list(range(n_gate))
            for step in order:
                if isinstance(step, int):
                    gates(step)
                else:
                    step()
            pool_ref[rs, :] = proj(OFF_POOL, POOL_WIDTH)
        else:
            diff_keys()
            gqa_keys()
            values()

    _staggered([pre_norm, project], range(0, tm, sub))


def _in_proj(x, mod, layer, mod_row_fn, g_pre, w_in, q_gain2, k_gain2, rope_tabs, *,
             seg_len, tm, q_side):
    n = x.shape[0]
    n_seg = n // seg_len
    tps = seg_len // tm
    rope = rope_tabs is not None
    seg = lambda i: i // tps
    pos = lambda i: i % tps
    row = lambda i: (i, 0)

    in_specs = [
        pl.BlockSpec((tm, D_MODEL), row),
        pl.BlockSpec((1, 1, 6, D_MODEL), lambda i: (layer, mod_row_fn(seg(i)), 0, 0)),
        _resident((1, D_MODEL)),
        _resident(w_in.shape),
        _resident((1, LANES)),
        _resident((1, LANES)),
    ]
    args = [x, mod, g_pre, w_in, q_gain2, k_gain2]
    if rope:
        in_specs += [pl.BlockSpec((tm, LANES), lambda i: (pos(i), 0))] * 2
        args += list(rope_tabs)

    out_shape, out_specs = [], []
    if q_side:
        out_shape += [
            jax.ShapeDtypeStruct((n, POOL_WIDTH), F32),
            jax.ShapeDtypeStruct((n_seg, DIFF_HEADS, 2, LANES, seg_len), BF16),
            jax.ShapeDtypeStruct((n_seg, GQA_KV_HEADS, GQA_GROUP, LANES, seg_len), BF16),
            jax.ShapeDtypeStruct((n, N_BRANCH * D_MODEL), BF16),
        ]
        out_specs += [
            pl.BlockSpec((tm, POOL_WIDTH), row),
            pl.BlockSpec((1, DIFF_HEADS, 2, LANES, tm), lambda i: (seg(i), 0, 0, 0, pos(i))),
            pl.BlockSpec((1, GQA_KV_HEADS, GQA_GROUP, LANES, tm),
                         lambda i: (seg(i), 0, 0, 0, pos(i))),
            pl.BlockSpec((tm, N_BRANCH * D_MODEL), row),
        ]
    out_shape += [
        jax.ShapeDtypeStruct((n, DIFF_WIDTH), BF16),
        jax.ShapeDtypeStruct((n_seg, DIFF_HEADS, 2 * HEAD_DIM, seg_len), BF16),
        jax.ShapeDtypeStruct((n, GQA_KV_WIDTH), BF16),
        jax.ShapeDtypeStruct((n_seg, GQA_KV_WIDTH, seg_len), BF16),
    ]
    out_specs += [
        pl.BlockSpec((tm, DIFF_WIDTH), row),
        pl.BlockSpec((1, DIFF_HEADS, 2 * HEAD_DIM, tm), lambda i: (seg(i), 0, 0, pos(i))),
        pl.BlockSpec((tm, GQA_KV_WIDTH), row),
        pl.BlockSpec((1, GQA_KV_WIDTH, tm), lambda i: (seg(i), 0, pos(i))),
    ]
    return pl.pallas_call(
        functools.partial(_in_proj_kernel, rope=rope, q_side=q_side),
        grid=(n // tm,),
        in_specs=in_specs,
        out_specs=out_specs,
        out_shape=out_shape,
        scratch_shapes=[pltpu.VMEM((tm, D_MODEL), BF16)],
        compiler_params=_cparams(("arbitrary",)),
        name="in_proj" + ("_rope" if rope else "") + ("" if q_side else "_kv"),
    )(*args)


class _Panels:
    def __init__(self, n_b, n_h, lq, pw, per_out):
        self.n_h, self.per_out, self.tiles = n_h, per_out, lq // pw
        self.per_head = per_out * self.tiles
        self.total = n_b * n_h * self.per_head

    def split(self, q):
        bh, loc = q // self.per_head, q % self.per_head
        return bh // self.n_h, bh % self.n_h, loc % self.per_out, loc // self.per_out

    @property
    def steps(self):
        return self.total + 2

    def loading(self, g):
        return self.split(jnp.minimum(g, self.total - 1))

    def emitting(self, g):
        return self.split(jnp.clip(g - 2, 0, self.total - 1))


def _attn_scratch(nk, pw, v_rows, per_out):
    return [
        pltpu.VMEM((2, nk, pw), F32),
        pltpu.VMEM((2, 8, pw), F32),
        pltpu.VMEM((per_out, v_rows, pw), F32),
        pltpu.VMEM((2, v_rows + BF16_ROWS, nk), BF16),
    ]


def _attn_pipeline(qt_ref, k_refs, vt_refs, s_scr, m_scr, on_scr, vt_scr, *, panels, emit):
    g = pl.program_id(0)
    total, per_head, per_out = panels.total, panels.per_head, panels.per_out
    q_load = jnp.minimum(g, total - 1)
    q_done = jnp.maximum(g - 1, 0)
    v_rows = on_scr.shape[1]

    @pl.when((g < total) & (q_load % per_head == 0))
    def _():
        dst = vt_scr.at[(q_load // per_head) & 1]
        c = 0
        for vt_ref in vt_refs:
            vt = vt_ref[...]
            dst[0:v_rows, c:c + vt.shape[-1]] = vt.reshape(vt.shape[-2:])
            c += vt.shape[-1]
        dst[v_rows:, :] = jnp.ones((dst.shape[0] - v_rows, dst.shape[1]), BF16)

    key_chunks = [(k_ref, c0) for k_ref in k_refs for c0 in range(0, k_ref.shape[0], MXU_DIM)]

    def phase(score_buf, value_buf):
        if score_buf is not None:
            qt = qt_ref[...].reshape(qt_ref.shape[-2:])
        if value_buf is not None:
            m_prev = m_scr[value_buf][0:1]
            vt = vt_scr.at[(q_done // per_head) & 1]
        m, acc = None, None
        for c, (k_ref, c0) in enumerate(key_chunks):
            rows = slice(c * MXU_DIM, (c + 1) * MXU_DIM)
            if value_buf is not None:
                pt = jnp.exp2(s_scr[value_buf, rows, :] - m_prev).astype(BF16)
                d = jnp.dot(vt[:, rows], pt, preferred_element_type=F32)
                acc = d if acc is None else acc + d
            if score_buf is not None:
                s = jnp.dot(k_ref[c0:c0 + MXU_DIM, :], qt, preferred_element_type=F32)
                s_scr[score_buf, rows, :] = s
                mc = jnp.max(s, axis=0, keepdims=True)
                m = mc if m is None else jnp.maximum(m, mc)
        if score_buf is not None:
            m_scr[score_buf] = jnp.broadcast_to(m, m_scr.shape[1:])
        if value_buf is not None:
            on_scr[q_done % per_out] = acc[0:v_rows] / acc[v_rows:v_rows + 1]

    @pl.when(g == 0)
    def _():
        on_scr[...] = jnp.zeros(on_scr.shape, F32)
        phase(0, None)

    for parity in range(2):
        @pl.when((g > 0) & (g < total) & ((g & 1) == parity))
        def _():
            emit()
            phase(parity, 1 - parity)

    @pl.when(g == total)
    def _():
        emit()
        phase(None, (total - 1) % 2)

    @pl.when(g == total + 1)
    def _():
        emit()


def _diff_attn_kernel(*refs, n_seg, panels, lam_init):
    it = iter(refs)
    lam_ref, subg_ref, qt_ref = next(it), next(it), next(it)
    k_refs = [next(it) for _ in range(n_seg)]
    vt_refs = [next(it) for _ in range(n_seg)]
    o_ref, s_scr, m_scr, on_scr, vt_scr = (next(it) for _ in range(5))

    def emit():
        lp = lam_ref[...]
        lam = (jnp.exp(jnp.sum(lp[0:1] * lp[1:2], axis=-1, keepdims=True))
               - jnp.exp(jnp.sum(lp[2:3] * lp[3:4], axis=-1, keepdims=True)) + lam_init)
        o = on_scr[0] - lam * on_scr[1]
        ms = jnp.mean(o * o, axis=0, keepdims=True)
        o = o * lax.rsqrt(ms + EPS) * subg_ref[...] * (1.0 - lam_init)
        o_ref[...] = o.T.astype(BF16)

    _attn_pipeline(qt_ref, k_refs, vt_refs, s_scr, m_scr, on_scr, vt_scr,
                   panels=panels, emit=emit)


def _diff_attn(dqt, ks, vts, lam_params, sub_gain_col, lam_init, *, pw):
    n_b, n_h, per_out, _, lq = dqt.shape
    lks = tuple(vt.shape[-1] for vt in vts)
    vw = 2 * HEAD_DIM
    panels = _Panels(n_b, n_h, lq, pw, per_out)

    def qt_map(g):
        b, h, j, n = panels.loading(g)
        return b, h, j, 0, n

    def k_map(g):
        b, h, _, _ = panels.loading(g)
        return b, h

    def vt_map(g):
        b, h, _, _ = panels.loading(g)
        return b, h, 0, 0

    def out_map(g):
        b, h, _, n = panels.emitting(g)
        return b * panels.tiles + n, h

    in_specs = [
        _resident(lam_params.shape),
        _resident(sub_gain_col.shape),
        pl.BlockSpec((1, 1, 1, LANES, pw), qt_map),
    ]
    in_specs += [pl.BlockSpec((lk, LANES), k_map) for lk in lks]
    in_specs += [pl.BlockSpec((1, 1, vw, lk), vt_map) for lk in lks]
    return pl.pallas_call(
        functools.partial(_diff_attn_kernel, n_seg=len(lks), panels=panels, lam_init=lam_init),
        grid=(panels.steps,),
        in_specs=in_specs,
        out_specs=pl.BlockSpec((pw, vw), out_map),
        out_shape=jax.ShapeDtypeStruct((n_b * lq, DIFF_WIDTH), BF16),
        scratch_shapes=_attn_scratch(sum(lks), pw, vw, per_out),
        compiler_params=_cparams(("arbitrary",)),
        name="diff_attn",
    )(lam_params, sub_gain_col, dqt, *ks, *vts)


def _gqa_attn_kernel(*refs, n_seg, panels):
    it = iter(refs)
    qt_ref = next(it)
    k_refs = [next(it) for _ in range(n_seg)]
    vt_refs = [next(it) for _ in range(n_seg)]
    o_ref, s_scr, m_scr, on_scr, vt_scr = (next(it) for _ in range(5))

    def emit():
        pairs = [jnp.concatenate([on_scr[2 * i], on_scr[2 * i + 1]], axis=0).T
                 for i in range(GQA_GROUP // 2)]
        o_ref[...] = jnp.concatenate(pairs, axis=1).astype(BF16)

    _attn_pipeline(qt_ref, k_refs, vt_refs, s_scr, m_scr, on_scr, vt_scr,
                   panels=panels, emit=emit)


def _gqa_attn(gqt, ks, vts, *, pw):
    n_b, n_h, per_out, _, lq = gqt.shape
    lks = tuple(vt.shape[-1] for vt in vts)
    panels = _Panels(n_b, n_h, lq, pw, per_out)

    def qt_map(g):
        b, h, j, n = panels.loading(g)
        return b, h, j, 0, n

    def k_map(g):
        return panels.loading(g)[0], 0

    def vt_map(g):
        b, h, _, _ = panels.loading(g)
        return b, h, 0

    def out_map(g):
        b, h, _, n = panels.emitting(g)
        return b * panels.tiles + n, h

    in_specs = [pl.BlockSpec((1, 1, 1, LANES, pw), qt_map)]
    in_specs += [pl.BlockSpec((lk, LANES), k_map) for lk in lks]
    in_specs += [pl.BlockSpec((1, HEAD_DIM, lk), vt_map) for lk in lks]
    return pl.pallas_call(
        functools.partial(_gqa_attn_kernel, n_seg=len(lks), panels=panels),
        grid=(panels.steps,),
        in_specs=in_specs,
        out_specs=pl.BlockSpec((pw, GQA_GROUP * HEAD_DIM), out_map),
        out_shape=jax.ShapeDtypeStruct((n_b * lq, GQA_WIDTH), BF16),
        scratch_shapes=_attn_scratch(sum(lks), pw, HEAD_DIM, per_out),
        compiler_params=_cparams(("arbitrary",)),
        name="gqa_attn",
    )(gqt, *ks, *vts)


def _mix_ffn_kernel(x_ref, zp_ref, z_ref, zn_ref, icnt_ref, do_ref, go_ref, gate_ref, mod_ref,
                    wgrp_ref, pscale_ref, wpo_ref, wdo_ref, wgo_ref, wo_ref,
                    gpost_ref, gpre_ref, w1_ref, w2_ref, gpost_ffn_ref,
                    xo_ref, zext_scr, xn_scr, hf_scr, h_scr, *, seg_len, tm):
    i = pl.program_id(0)
    tps = seg_len // tm
    t = i % tps
    z = z_ref[...]
    zext_scr[0:POOL_HALO] = jnp.where(t > 0, zp_ref[...], 0.0)
    zext_scr[POOL_HALO:POOL_HALO + tm] = z
    zext_scr[POOL_HALO + tm:] = jnp.where(t < tps - 1, zn_ref[...], 0.0)
    mod = mod_ref[0, 0]
    sub = min(tm, SUB_ROWS)

    def pool_mixer(r0, _):
        mixed = []
        for g, w in enumerate(POOL_WINDOWS):
            cs = slice(g * POOL_GROUP_DIM, (g + 1) * POOL_GROUP_DIM)
            ssum = None
            for d in range(-(w // 2), w // 2):
                zz = zext_scr[POOL_HALO + d + r0:POOL_HALO + d + r0 + sub, cs]
                ssum = zz if ssum is None else ssum + zz
            pooled = ssum * icnt_ref[r0:r0 + sub, cs] - z_ref[r0:r0 + sub, cs]
            mixed.append(jnp.dot(pooled.astype(BF16), wgrp_ref[g], preferred_element_type=F32))
        return (jnp.concatenate(mixed, axis=1) * pscale_ref[...]).astype(BF16)

    def gated_branches(r0, pool_out):
        rs = slice(r0, r0 + sub)

        def gate(k):
            return gate_ref[rs, k * D_MODEL:(k + 1) * D_MODEL].astype(F32)

        merged = gate(0) * jnp.dot(pool_out, wpo_ref[...], preferred_element_type=F32)
        merged += gate(1) * jnp.dot(do_ref[rs, :], wdo_ref[...], preferred_element_type=F32)
        merged += gate(2) * jnp.dot(go_ref[rs, :], wgo_ref[...], preferred_element_type=F32)
        return merged.astype(BF16)

    def out_proj(r0, merged):
        return jnp.dot(merged, wo_ref[...], preferred_element_type=F32)

    def residual(r0, y):
        rs = slice(r0, r0 + sub)
        xn = x_ref[rs, :] + mod[2:3] * _rms(y, gpost_ref[...])
        xn_scr[rs, :] = xn
        hf_scr[rs, :] = (_rms(xn, gpre_ref[...]) * (1.0 + mod[4:5]) + mod[3:4]).astype(BF16)

    def ffn_up(r0, _):
        rs = slice(r0, r0 + sub)
        hf = hf_scr[rs, :]
        for j in range(D_FF // PROJ_CHUNK):
            cols = slice(j * PROJ_CHUNK, (j + 1) * PROJ_CHUNK)
            h = jnp.maximum(jnp.dot(hf, w1_ref[:, cols], preferred_element_type=F32), 0.0)
            h_scr[rs, cols] = (h * h).astype(BF16)

    def ffn_down(r0, _):
        rs = slice(r0, r0 + sub)
        y = jnp.dot(h_scr[rs, :], w2_ref[...], preferred_element_type=F32)
        xo_ref[rs, :] = xn_scr[rs, :] + mod[5:6] * _rms(y, gpost_ffn_ref[...])

    _staggered([pool_mixer, gated_branches, out_proj, residual, ffn_up, ffn_down],
               range(0, tm, sub))


def _pool_inverse_counts(seg_len, tm):
    tps = seg_len // tm
    t = jnp.arange(seg_len)
    if tps > 2:
        t = jnp.concatenate([t[:2 * tm], t[-tm:]])
    cols = []
    for w in POOL_WINDOWS:
        cnt = jnp.minimum(t + w // 2, seg_len) - jnp.maximum(t - w // 2, 0)
        cols.append(jnp.broadcast_to((1.0 / cnt.astype(F32))[:, None], (t.shape[0], POOL_GROUP_DIM)))
    table = jnp.concatenate(cols, axis=1)
    n_blocks = table.shape[0] // tm
    block = lambda pos: jnp.where(pos == tps - 1, n_blocks - 1, jnp.minimum(pos, 1))
    return table, block


def _mix_ffn(x, pool_in, diff_out, gqa_out, gates, mod, layer, mod_row_fn, inv_counts, w_grp,
             pool_scale, w_pool_out, w_diff_out, w_gqa_out, w_o, g_post, g_pre_ffn, w1, w2,
             g_post_ffn, *, seg_len, tm):
    n = x.shape[0]
    tps = seg_len // tm
    hb = tm // POOL_HALO
    n_hblk = n // POOL_HALO
    row = lambda i: (i, 0)
    icnt, icnt_block = inv_counts
    weights = (w_grp, pool_scale, w_pool_out, w_diff_out, w_gqa_out, w_o, g_post, g_pre_ffn,
               w1, w2, g_post_ffn)
    in_specs = [
        pl.BlockSpec((tm, D_MODEL), row),
        pl.BlockSpec((POOL_HALO, POOL_WIDTH), lambda i: (jnp.maximum(i * hb - 1, 0), 0)),
        pl.BlockSpec((tm, POOL_WIDTH), row),
        pl.BlockSpec((POOL_HALO, POOL_WIDTH), lambda i: (jnp.minimum((i + 1) * hb, n_hblk - 1), 0)),
        pl.BlockSpec((tm, POOL_WIDTH), lambda i: (icnt_block(i % tps), 0)),
        pl.BlockSpec((tm, DIFF_WIDTH), row),
        pl.BlockSpec((tm, GQA_WIDTH), row),
        pl.BlockSpec((tm, N_BRANCH * D_MODEL), row),
        pl.BlockSpec((1, 1, 6, D_MODEL), lambda i: (layer, mod_row_fn(i // tps), 0, 0)),
    ] + [_resident(w.shape) for w in weights]
    return pl.pallas_call(
        functools.partial(_mix_ffn_kernel, seg_len=seg_len, tm=tm),
        grid=(n // tm,),
        in_specs=in_specs,
        out_specs=pl.BlockSpec((tm, D_MODEL), row),
        out_shape=jax.ShapeDtypeStruct((n, D_MODEL), F32),
        scratch_shapes=[
            pltpu.VMEM((tm + 2 * POOL_HALO, POOL_WIDTH), F32),
            pltpu.VMEM((tm, D_MODEL), F32),
            pltpu.VMEM((tm, D_MODEL), BF16),
            pltpu.VMEM((tm, D_FF), BF16),
        ],
        compiler_params=_cparams(("arbitrary",)),
        name="mix_ffn",
    )(x, pool_in, pool_in, pool_in, icnt, diff_out, gqa_out, gates, mod, *weights)


def _rope_tables(seq):
    rows = seq // GRID_W
    row = jnp.repeat(jnp.arange(rows), GRID_W).astype(F32)
    col = jnp.tile(jnp.arange(GRID_W), rows).astype(F32)
    inv = 1.0 / (ROPE_BASE ** (jnp.arange(ROPE_PAIRS_PER_AXIS, dtype=F32) * 2.0 / ROPE_AXIS_DIM))
    ang = jnp.concatenate([row[:, None] * inv, col[:, None] * inv], axis=-1)
    cos, sin = jnp.cos(ang), jnp.sin(ang)
    return (jnp.tile(cos, (1, 4)),
            jnp.concatenate([-sin, -sin, sin, sin], axis=-1))


def _pair_layout(w, member_major):
    d = w.shape[0]
    if member_major:
        w5, perm = w.reshape(d, 2, -1, HALF_HEAD, 2), (0, 2, 4, 1, 3)
    else:
        w5, perm = w.reshape(d, -1, 2, HALF_HEAD, 2), (0, 1, 4, 2, 3)
    return w5.transpose(perm).reshape(d, -1)


def _in_proj_weights(w):
    seg = lambda a, b: w[:, a:b]
    return jnp.concatenate([
        seg(OFF_POOL, OFF_DQ),
        _pair_layout(seg(OFF_DQ, OFF_GQ), False),
        _pair_layout(seg(OFF_GQ, OFF_GATE), True),
        seg(OFF_GATE, OFF_DK),
        _pair_layout(seg(OFF_DK, OFF_DV), False),
        seg(OFF_DV, OFF_GK),
        _pair_layout(seg(OFF_GK, OFF_GV), False),
        seg(OFF_GV, IN_WIDTH),
    ], axis=1).astype(BF16)


def _pair_gain(g):
    return jnp.concatenate([g[0::2], g[0::2], g[1::2], g[1::2]]).reshape(1, LANES)


def kernel(x, c, ctx, c_ctx, w_mod, b_mod, g_pre_mix, g_post_mix, g_pre_ffn, g_post_ffn,
           w_in, w_pool_grp, pool_scale, lambda_q1, lambda_k1, lambda_q2, lambda_k2,
           diff_subln, gqa_q_norm, gqa_k_norm, w_pool_out, w_diff_out, w_gqa_out, w_o,
           w_ff1, w_ff2):
    batch, seq, d = x.shape
    ctx_len = ctx.shape[1]
    depth = w_mod.shape[0]
    assert d == D_MODEL and batch < MOD_ROWS and w_in.shape[-1] == IN_WIDTH
    assert GQA_KV_HEADS == 2 and ctx_len % MXU_DIM == 0

    lat_tm, ctx_tm = 512, ctx_len
    rope_tabs = _rope_tables(seq)
    lat_counts = _pool_inverse_counts(seq, lat_tm)
    ctx_counts = _pool_inverse_counts(ctx_len, ctx_tm)
    c_all = jnp.zeros((MOD_ROWS, d), F32).at[:batch].set(c).at[batch].set(c_ctx)
    mod = _modulation(c_all, w_mod, b_mod)
    lat_row = lambda s: s
    ctx_row = lambda s: batch

    xl = x.reshape(batch * seq, d)
    xc = ctx.reshape(batch * ctx_len, d)
    row = lambda a: a.reshape(1, -1)

    for l in range(depth):
        last = l == depth - 1
        lam_init = 0.8 - 0.6 * math.exp(-0.3 * l)
        w_in_b = _in_proj_weights(w_in[l])
        qg2, kg2 = _pair_gain(gqa_q_norm[l]), _pair_gain(gqa_k_norm[l])
        lam_params = jnp.stack([lambda_q1[l], lambda_k1[l], lambda_q2[l], lambda_k2[l]])
        sub_gain = diff_subln[l].reshape(-1, 1)
        mix_w = (w_pool_grp[l].astype(BF16), row(pool_scale[l]), w_pool_out[l].astype(BF16),
                 w_diff_out[l].astype(BF16), w_gqa_out[l].astype(BF16), w_o[l].astype(BF16),
                 row(g_post_mix[l]), row(g_pre_ffn[l]),
                 w_ff1[l].astype(BF16), w_ff2[l].astype(BF16), row(g_post_ffn[l]))

        cp = _in_proj(xc, mod, l, ctx_row, row(g_pre_mix[l]), w_in_b, qg2, kg2, None,
                      seg_len=ctx_len, tm=ctx_tm, q_side=not last)
        c_dk, c_dvt, c_gk, c_gvt = cp[-4:]
        pool_in, dqt, gqt, gates, l_dk, l_dvt, l_gk, l_gvt = _in_proj(
            xl, mod, l, lat_row, row(g_pre_mix[l]), w_in_b, qg2, kg2, rope_tabs,
            seg_len=seq, tm=lat_tm, q_side=True)

        diff_out = _diff_attn(dqt, (c_dk, l_dk), (c_dvt, l_dvt), lam_params, sub_gain,
                              lam_init, pw=PANEL)
        gqa_out = _gqa_attn(gqt, (c_gk, l_gk), (c_gvt, l_gvt), pw=PANEL // 2)
        xl_next = _mix_ffn(xl, pool_in, diff_out, gqa_out, gates, mod, l, lat_row, lat_counts,
                           *mix_w, seg_len=seq, tm=lat_tm)

        if not last:
            c_pool, c_dqt, c_gqt, c_gates = cp[:4]
            c_diff = _diff_attn(c_dqt, (c_dk,), (c_dvt,), lam_params, sub_gain, lam_init,
                                pw=ctx_len)
            c_gqa = _gqa_attn(c_gqt, (c_gk,), (c_gvt,), pw=ctx_len)
            xc = _mix_ffn(xc, c_pool, c_diff, c_gqa, c_gates, mod, l, ctx_row, ctx_counts,
                          *mix_w, seg_len=ctx_len, tm=ctx_tm)
        xl = xl_next
    return xl.reshape(batch, seq, d)
```

```python
import functools
import math

import jax
import jax.numpy as jnp
from jax import lax
from jax.experimental import pallas as pl
from jax.experimental.pallas import tpu as pltpu

F32 = jnp.float32
BF16 = jnp.bfloat16

D_MODEL = 1024
GRID_W = 64
HEAD_DIM = 64
HALF_HEAD = HEAD_DIM // 2
ATTN_SCALE = HEAD_DIM ** -0.5
LOG2E = math.log2(math.e)
ROPE_BASE = 10000.0
ROPE_AXIS_DIM = HEAD_DIM // 2
ROPE_PAIRS_PER_AXIS = ROPE_AXIS_DIM // 2
EPS = 1e-6

POOL_WINDOWS = (2, 4, 8, 16)
POOL_WIDTH = D_MODEL // 2
POOL_GROUP_DIM = POOL_WIDTH // len(POOL_WINDOWS)
POOL_HALO = max(POOL_WINDOWS) // 2

DIFF_HEADS = D_MODEL // 256
DIFF_WIDTH = DIFF_HEADS * 2 * HEAD_DIM
GQA_HEADS = D_MODEL // 128
GQA_KV_HEADS = GQA_HEADS // 4
GQA_GROUP = GQA_HEADS // GQA_KV_HEADS
GQA_WIDTH = GQA_HEADS * HEAD_DIM
GQA_KV_WIDTH = GQA_KV_HEADS * HEAD_DIM
N_BRANCH = 3
D_FF = 4 * D_MODEL

OFF_POOL = 0
OFF_DQ = OFF_POOL + POOL_WIDTH
OFF_GQ = OFF_DQ + DIFF_WIDTH
OFF_GATE = OFF_GQ + GQA_WIDTH
OFF_DK = OFF_GATE + N_BRANCH * D_MODEL
OFF_DV = OFF_DK + DIFF_WIDTH
OFF_GK = OFF_DV + DIFF_WIDTH
OFF_GV = OFF_GK + GQA_KV_WIDTH
IN_WIDTH = OFF_GV + GQA_KV_WIDTH

LANES = 128
MXU_DIM = 256
PANEL = 4 * MXU_DIM
BF16_ROWS = 16
PROJ_CHUNK = 512
SUB_ROWS = 256
VMEM_LIMIT = 56 * 1024 * 1024

MOD_ROWS = 8


def _cparams(sem):
    return pltpu.CompilerParams(dimension_semantics=sem, vmem_limit_bytes=VMEM_LIMIT)


def _resident(shape):
    nd = len(shape)
    return pl.BlockSpec(shape, lambda *_: (0,) * nd, pipeline_mode=pl.Buffered(1))


def _rms(x, g):
    ms = jnp.mean(x * x, axis=-1, keepdims=True)
    return x * lax.rsqrt(ms + EPS) * g


def _staggered(stages, row_starts):
    row_starts = list(row_starts)
    carry = {}
    for step in range(len(row_starts) + len(stages) - 1):
        for k in reversed(range(len(stages))):
            i = step - k
            if 0 <= i < len(row_starts):
                carry[i] = stages[k](row_starts[i], carry.get(i))


def _mod_kernel(c_ref, w_ref, b_ref, o_ref):
    c = c_ref[...]
    a = c * (1.0 / (1.0 + jnp.exp(-c)))
    o_ref[0] = jnp.dot(a.astype(BF16), w_ref[0].astype(BF16),
                       preferred_element_type=F32) + b_ref[0]


def _modulation(c_all, w_mod, b_mod):
    depth = w_mod.shape[0]
    nblk = w_mod.shape[2] // D_MODEL
    out = pl.pallas_call(
        _mod_kernel,
        grid=(depth, nblk),
        in_specs=[
            pl.BlockSpec((MOD_ROWS, D_MODEL), lambda l, j: (0, 0)),
            pl.BlockSpec((1, D_MODEL, D_MODEL), lambda l, j: (l, 0, j)),
            pl.BlockSpec((1, 1, D_MODEL), lambda l, j: (l, 0, j)),
        ],
        out_specs=pl.BlockSpec((1, MOD_ROWS, D_MODEL), lambda l, j: (l, 0, j)),
        out_shape=jax.ShapeDtypeStruct((depth, MOD_ROWS, nblk * D_MODEL), F32),
        compiler_params=_cparams(("arbitrary", "arbitrary")),
        name="modulation",
    )(c_all, w_mod, b_mod.reshape(depth, 1, nblk * D_MODEL))
    return out.reshape(depth, MOD_ROWS, nblk, D_MODEL)


def _rope(x, cos, sin_signed):
    return x * cos + pltpu.roll(x, HEAD_DIM, 1) * sin_signed


def _head_norm(x, gain2):
    lane = lax.broadcasted_iota(jnp.int32, x.shape, 1)
    first = (lane & HALF_HEAD) == 0
    x2 = x * x
    ms_a = jnp.sum(jnp.where(first, x2, 0.0), axis=-1, keepdims=True) * (1.0 / HEAD_DIM)
    ms_b = jnp.sum(jnp.where(first, 0.0, x2), axis=-1, keepdims=True) * (1.0 / HEAD_DIM)
    inv = jnp.where(first, lax.rsqrt(ms_a + EPS), lax.rsqrt(ms_b + EPS))
    return x * inv * gain2


def _in_proj_kernel(*refs, rope, q_side):
    it = iter(refs)
    x_ref, mod_ref, gpre_ref, w_ref, qg_ref, kg_ref = (next(it) for _ in range(6))
    cos_ref = next(it) if rope else None
    sin_ref = next(it) if rope else None
    if q_side:
        pool_ref, dqt_ref, gqt_ref, gate_ref = (next(it) for _ in range(4))
    dk_ref, dvt_ref, gk_ref, gvt_ref = (next(it) for _ in range(4))
    hb_scr = next(it)

    tm = x_ref.shape[0]
    sub = min(tm, SUB_ROWS)
    mod = mod_ref[0, 0]
    q_scale = ATTN_SCALE * LOG2E

    def pre_norm(r0, _):
        rs = slice(r0, r0 + sub)
        h = _rms(x_ref[rs, :], gpre_ref[...]) * (1.0 + mod[1:2]) + mod[0:1]
        hb_scr[rs, :] = h.astype(BF16)

    def project(r0, _):
        rs = slice(r0, r0 + sub)
        hb = hb_scr[rs, :]
        if rope:
            cos, sin = cos_ref[rs, :], sin_ref[rs, :]

        def proj(off, width):
            return jnp.dot(hb, w_ref[:, off:off + width], preferred_element_type=F32)

        def blocks(off, width):
            y = proj(off, width)
            return [y[:, j * LANES:(j + 1) * LANES] for j in range(width // LANES)]

        def store_heads_t(ref, idx_a, idx_b, y):
            yt = y.T
            row = lax.broadcasted_iota(jnp.int32, yt.shape, 0)
            first = (row & HALF_HEAD) == 0
            ref[idx_a + (slice(None), rs)] = jnp.where(first, yt, 0.0).astype(BF16)
            ref[idx_b + (slice(None), rs)] = jnp.where(first, 0.0, yt).astype(BF16)

        def diff_keys():
            for hh, y in enumerate(blocks(OFF_DK, DIFF_WIDTH)):
                if rope:
                    y = _rope(y, cos, sin)
                dk_ref[rs, hh * LANES:(hh + 1) * LANES] = y.astype(BF16)

        def gqa_keys():
            (y,) = blocks(OFF_GK, GQA_KV_WIDTH)
            y = _head_norm(y, kg_ref[...])
            if rope:
                y = _rope(y, cos, sin)
            gk_ref[rs, :] = y.astype(BF16)

        def diff_queries():
            for hh, y in enumerate(blocks(OFF_DQ, DIFF_WIDTH)):
                if rope:
                    y = _rope(y, cos, sin)
                store_heads_t(dqt_ref, (0, hh, 0), (0, hh, 1), y * q_scale)

        def gqa_queries():
            for jj, y in enumerate(blocks(OFF_GQ, GQA_WIDTH)):
                y = _head_norm(y, qg_ref[...])
                if rope:
                    y = _rope(y, cos, sin)
                store_heads_t(gqt_ref, (0, 0, jj), (0, 1, jj), y * q_scale)

        def values():
            for hh, y in enumerate(blocks(OFF_DV, DIFF_WIDTH)):
                dvt_ref[0, hh, :, rs] = y.T.astype(BF16)
            (y,) = blocks(OFF_GV, GQA_KV_WIDTH)
            gvt_ref[0, :, rs] = y.T.astype(BF16)

        def gates(j):
            gl = proj(OFF_GATE + j * PROJ_CHUNK, PROJ_CHUNK)
            gate_ref[rs, j * PROJ_CHUNK:(j + 1) * PROJ_CHUNK] = (
                1.0 / (1.0 + jnp.exp(-gl))).astype(BF16)

        if q_side:
            n_gate = N_BRANCH * D_MODEL // PROJ_CHUNK
            order = [diff_keys, gqa_keys, 0, 1, diff_queries, 2, 3, gqa_queries, 4, 5, values]
            assert sorted(s for s in order if isinstance(s, int)) == list(range(n_gate))
            for step in order:
                if isinstance(step, int):
                    gates(step)
                else:
                    step()
            pool_ref[rs, :] = proj(OFF_POOL, POOL_WIDTH)
        else:
            diff_keys()
            gqa_keys()
            values()

    _staggered([pre_norm, project], range(0, tm, sub))


def _in_proj(x, mod, layer, mod_row_fn, g_pre, w_in, q_gain2, k_gain2, rope_tabs, *,
             seg_len, tm, q_side):
    n = x.shape[0]
    n_seg = n // seg_len
    tps = seg_len // tm
    rope = rope_tabs is not None
    seg = lambda i: i // tps
    pos = lambda i: i % tps
    row = lambda i: (i, 0)

    in_specs = [
        pl.BlockSpec((tm, D_MODEL), row),
        pl.BlockSpec((1, 1, 6, D_MODEL), lambda i: (layer, mod_row_fn(seg(i)), 0, 0)),
        _resident((1, D_MODEL)),
        _resident(w_in.shape),
        _resident((1, LANES)),
        _resident((1, LANES)),
    ]
    args = [x, mod, g_pre, w_in, q_gain2, k_gain2]
    if rope:
        in_specs += [pl.BlockSpec((tm, LANES), lambda i: (pos(i), 0))] * 2
        args += list(rope_tabs)

    out_shape, out_specs = [], []
    if q_side:
        out_shape += [
            jax.ShapeDtypeStruct((n, POOL_WIDTH), F32),
            jax.ShapeDtypeStruct((n_seg, DIFF_HEADS, 2, LANES, seg_len), BF16),
            jax.ShapeDtypeStruct((n_seg, GQA_KV_HEADS, GQA_GROUP, LANES, seg_len), BF16),
            jax.ShapeDtypeStruct((n, N_BRANCH * D_MODEL), BF16),
        ]
        out_specs += [
            pl.BlockSpec((tm, POOL_WIDTH), row),
            pl.BlockSpec((1, DIFF_HEADS, 2, LANES, tm), lambda i: (seg(i), 0, 0, 0, pos(i))),
            pl.BlockSpec((1, GQA_KV_HEADS, GQA_GROUP, LANES, tm),
                         lambda i: (seg(i), 0, 0, 0, pos(i))),
            pl.BlockSpec((tm, N_BRANCH * D_MODEL), row),
        ]
    out_shape += [
        jax.ShapeDtypeStruct((n, DIFF_WIDTH), BF16),
        jax.ShapeDtypeStruct((n_seg, DIFF_HEADS, 2 * HEAD_DIM, seg_len), BF16),
        jax.ShapeDtypeStruct((n, GQA_KV_WIDTH), BF16),
        jax.ShapeDtypeStruct((n_seg, GQA_KV_WIDTH, seg_len), BF16),
    ]
    out_specs += [
        pl.BlockSpec((tm, DIFF_WIDTH), row),
        pl.BlockSpec((1, DIFF_HEADS, 2 * HEAD_DIM, tm), lambda i: (seg(i), 0, 0, pos(i))),
        pl.BlockSpec((tm, GQA_KV_WIDTH), row),
        pl.BlockSpec((1, GQA_KV_WIDTH, tm), lambda i: (seg(i), 0, pos(i))),
    ]
    return pl.pallas_call(
        functools.partial(_in_proj_kernel, rope=rope, q_side=q_side),
        grid=(n // tm,),
        in_specs=in_specs,
        out_specs=out_specs,
        out_shape=out_shape,
        scratch_shapes=[pltpu.VMEM((tm, D_MODEL), BF16)],
        compiler_params=_cparams(("arbitrary",)),
        name="in_proj" + ("_rope" if rope else "") + ("" if q_side else "_kv"),
    )(*args)


class _Panels:
    def __init__(self, n_b, n_h, lq, pw, per_out):
        self.n_h, self.per_out, self.tiles = n_h, per_out, lq // pw
        self.per_head = per_out * self.tiles
        self.total = n_b * n_h * self.per_head

    def split(self, q):
        bh, loc = q // self.per_head, q % self.per_head
        return bh // self.n_h, bh % self.n_h, loc % self.per_out, loc // self.per_out

    @property
    def steps(self):
        return self.total + 2

    def loading(self, g):
        return self.split(jnp.minimum(g, self.total - 1))

    def emitting(self, g):
        return self.split(jnp.clip(g - 2, 0, self.total - 1))


def _attn_scratch(nk, pw, v_rows, per_out):
    return [
        pltpu.VMEM((2, nk, pw), F32),
        pltpu.VMEM((2, 8, pw), F32),
        pltpu.VMEM((per_out, v_rows, pw), F32),
        pltpu.VMEM((2, v_rows + BF16_ROWS, nk), BF16),
    ]


def _attn_pipeline(qt_ref, k_refs, vt_refs, s_scr, m_scr, on_scr, vt_scr, *, panels, emit):
    g = pl.program_id(0)
    total, per_head, per_out = panels.total, panels.per_head, panels.per_out
    q_load = jnp.minimum(g, total - 1)
    q_done = jnp.maximum(g - 1, 0)
    v_rows = on_scr.shape[1]

    @pl.when((g < total) & (q_load % per_head == 0))
    def _():
        dst = vt_scr.at[(q_load // per_head) & 1]
        c = 0
        for vt_ref in vt_refs:
            vt = vt_ref[...]
            dst[0:v_rows, c:c + vt.shape[-1]] = vt.reshape(vt.shape[-2:])
            c += vt.shape[-1]
        dst[v_rows:, :] = jnp.ones((dst.shape[0] - v_rows, dst.shape[1]), BF16)

    key_chunks = [(k_ref, c0) for k_ref in k_refs for c0 in range(0, k_ref.shape[0], MXU_DIM)]

    def phase(score_buf, value_buf):
        if score_buf is not None:
            qt = qt_ref[...].reshape(qt_ref.shape[-2:])
        if value_buf is not None:
            m_prev = m_scr[value_buf][0:1]
            vt = vt_scr.at[(q_done // per_head) & 1]
        m, acc = None, None
        for c, (k_ref, c0) in enumerate(key_chunks):
            rows = slice(c * MXU_DIM, (c + 1) * MXU_DIM)
            if value_buf is not None:
                pt = jnp.exp2(s_scr[value_buf, rows, :] - m_prev).astype(BF16)
                d = jnp.dot(vt[:, rows], pt, preferred_element_type=F32)
                acc = d if acc is None else acc + d
            if score_buf is not None:
                s = jnp.dot(k_ref[c0:c0 + MXU_DIM, :], qt, preferred_element_type=F32)
                s_scr[score_buf, rows, :] = s
                mc = jnp.max(s, axis=0, keepdims=True)
                m = mc if m is None else jnp.maximum(m, mc)
        if score_buf is not None:
            m_scr[score_buf] = jnp.broadcast_to(m, m_scr.shape[1:])
        if value_buf is not None:
            on_scr[q_done % per_out] = acc[0:v_rows] / acc[v_rows:v_rows + 1]

    @pl.when(g == 0)
    def _():
        on_scr[...] = jnp.zeros(on_scr.shape, F32)
        phase(0, None)

    for parity in range(2):
        @pl.when((g > 0) & (g < total) & ((g & 1) == parity))
        def _():
            emit()
            phase(parity, 1 - parity)

    @pl.when(g == total)
    def _():
        emit()
        phase(None, (total - 1) % 2)

    @pl.when(g == total + 1)
    def _():
        emit()


def _diff_attn_kernel(*refs, n_seg, panels, lam_init):
    it = iter(refs)
    lam_ref, subg_ref, qt_ref = next(it), next(it), next(it)
    k_refs = [next(it) for _ in range(n_seg)]
    vt_refs = [next(it) for _ in range(n_seg)]
    o_ref, s_scr, m_scr, on_scr, vt_scr = (next(it) for _ in range(5))

    def emit():
        lp = lam_ref[...]
        lam = (jnp.exp(jnp.sum(lp[0:1] * lp[1:2], axis=-1, keepdims=True))
               - jnp.exp(jnp.sum(lp[2:3] * lp[3:4], axis=-1, keepdims=True)) + lam_init)
        o = on_scr[0] - lam * on_scr[1]
        ms = jnp.mean(o * o, axis=0, keepdims=True)
        o = o * lax.rsqrt(ms + EPS) * subg_ref[...] * (1.0 - lam_init)
        o_ref[...] = o.T.astype(BF16)

    _attn_pipeline(qt_ref, k_refs, vt_refs, s_scr, m_scr, on_scr, vt_scr,
                   panels=panels, emit=emit)


def _diff_attn(dqt, ks, vts, lam_params, sub_gain_col, lam_init, *, pw):
    n_b, n_h, per_out, _, lq = dqt.shape
    lks = tuple(vt.shape[-1] for vt in vts)
    vw = 2 * HEAD_DIM
    panels = _Panels(n_b, n_h, lq, pw, per_out)

    def qt_map(g):
        b, h, j, n = panels.loading(g)
        return b, h, j, 0, n

    def k_map(g):
        b, h, _, _ = panels.loading(g)
        return b, h

    def vt_map(g):
        b, h, _, _ = panels.loading(g)
        return b, h, 0, 0

    def out_map(g):
        b, h, _, n = panels.emitting(g)
        return b * panels.tiles + n, h

    in_specs = [
        _resident(lam_params.shape),
        _resident(sub_gain_col.shape),
        pl.BlockSpec((1, 1, 1, LANES, pw), qt_map),
    ]
    in_specs += [pl.BlockSpec((lk, LANES), k_map) for lk in lks]
    in_specs += [pl.BlockSpec((1, 1, vw, lk), vt_map) for lk in lks]
    return pl.pallas_call(
        functools.partial(_diff_attn_kernel, n_seg=len(lks), panels=panels, lam_init=lam_init),
        grid=(panels.steps,),
        in_specs=in_specs,
        out_specs=pl.BlockSpec((pw, vw), out_map),
        out_shape=jax.ShapeDtypeStruct((n_b * lq, DIFF_WIDTH), BF16),
        scratch_shapes=_attn_scratch(sum(lks), pw, vw, per_out),
        compiler_params=_cparams(("arbitrary",)),
        name="diff_attn",
    )(lam_params, sub_gain_col, dqt, *ks, *vts)


def _gqa_attn_kernel(*refs, n_seg, panels):
    it = iter(refs)
    qt_ref = next(it)
    k_refs = [next(it) for _ in range(n_seg)]
    vt_refs = [next(it) for _ in range(n_seg)]
    o_ref, s_scr, m_scr, on_scr, vt_scr = (next(it) for _ in range(5))

    def emit():
        pairs = [jnp.concatenate([on_scr[2 * i], on_scr[2 * i + 1]], axis=0).T
                 for i in range(GQA_GROUP // 2)]
        o_ref[...] = jnp.concatenate(pairs, axis=1).astype(BF16)

    _attn_pipeline(qt_ref, k_refs, vt_refs, s_scr, m_scr, on_scr, vt_scr,
                   panels=panels, emit=emit)


def _gqa_attn(gqt, ks, vts, *, pw):
    n_b, n_h, per_out, _, lq = gqt.shape
    lks = tuple(vt.shape[-1] for vt in vts)
    panels = _Panels(n_b, n_h, lq, pw, per_out)

    def qt_map(g):
        b, h, j, n = panels.loading(g)
        return b, h, j, 0, n

    def k_map(g):
        return panels.loading(g)[0], 0

    def vt_map(g):
        b, h, _, _ = panels.loading(g)
        return b, h, 0

    def out_map(g):
        b, h, _, n = panels.emitting(g)
        return b * panels.tiles + n, h

    in_specs = [pl.BlockSpec((1, 1, 1, LANES, pw), qt_map)]
    in_specs += [pl.BlockSpec((lk, LANES), k_map) for lk in lks]
    in_specs += [pl.BlockSpec((1, HEAD_DIM, lk), vt_map) for lk in lks]
    return pl.pallas_call(
        functools.partial(_gqa_attn_kernel, n_seg=len(lks), panels=panels),
        grid=(panels.steps,),
        in_specs=in_specs,
        out_specs=pl.BlockSpec((pw, GQA_GROUP * HEAD_DIM), out_map),
        out_shape=jax.ShapeDtypeStruct((n_b * lq, GQA_WIDTH), BF16),
        scratch_shapes=_attn_scratch(sum(lks), pw, HEAD_DIM, per_out),
        compiler_params=_cparams(("arbitrary",)),
        name="gqa_attn",
    )(gqt, *ks, *vts)


def _mix_ffn_kernel(x_ref, zp_ref, z_ref, zn_ref, icnt_ref, do_ref, go_ref, gate_ref, mod_ref,
                    wgrp_ref, pscale_ref, wpo_ref, wdo_ref, wgo_ref, wo_ref,
                    gpost_ref, gpre_ref, w1_ref, w2_ref, gpost_ffn_ref,
                    xo_ref, zext_scr, xn_scr, hf_scr, h_scr, *, seg_len, tm):
    i = pl.program_id(0)
    tps = seg_len // tm
    t = i % tps
    z = z_ref[...]
    zext_scr[0:POOL_HALO] = jnp.where(t > 0, zp_ref[...], 0.0)
    zext_scr[POOL_HALO:POOL_HALO + tm] = z
    zext_scr[POOL_HALO + tm:] = jnp.where(t < tps - 1, zn_ref[...], 0.0)
    mod = mod_ref[0, 0]
    sub = min(tm, SUB_ROWS)

    def pool_mixer(r0, _):
        mixed = []
        for g, w in enumerate(POOL_WINDOWS):
            cs = slice(g * POOL_GROUP_DIM, (g + 1) * POOL_GROUP_DIM)
            ssum = None
            for d in range(-(w // 2), w // 2):
                zz = zext_scr[POOL_HALO + d + r0:POOL_HALO + d + r0 + sub, cs]
                ssum = zz if ssum is None else ssum + zz
            pooled = ssum * icnt_ref[r0:r0 + sub, cs] - z_ref[r0:r0 + sub, cs]
            mixed.append(jnp.dot(pooled.astype(BF16), wgrp_ref[g], preferred_element_type=F32))
        return (jnp.concatenate(mixed, axis=1) * pscale_ref[...]).astype(BF16)

    def gated_branches(r0, pool_out):
        rs = slice(r0, r0 + sub)

        def gate(k):
            return gate_ref[rs, k * D_MODEL:(k + 1) * D_MODEL].astype(F32)

        merged = gate(0) * jnp.dot(pool_out, wpo_ref[...], preferred_element_type=F32)
        merged += gate(1) * jnp.dot(do_ref[rs, :], wdo_ref[...], preferred_element_type=F32)
        merged += gate(2) * jnp.dot(go_ref[rs, :], wgo_ref[...], preferred_element_type=F32)
        return merged.astype(BF16)

    def out_proj(r0, merged):
        return jnp.dot(merged, wo_ref[...], preferred_element_type=F32)

    def residual(r0, y):
        rs = slice(r0, r0 + sub)
        xn = x_ref[rs, :] + mod[2:3] * _rms(y, gpost_ref[...])
        xn_scr[rs, :] = xn
        hf_scr[rs, :] = (_rms(xn, gpre_ref[...]) * (1.0 + mod[4:5]) + mod[3:4]).astype(BF16)

    def ffn_up(r0, _):
        rs = slice(r0, r0 + sub)
        hf = hf_scr[rs, :]
        for j in range(D_FF // PROJ_CHUNK):
            cols = slice(j * PROJ_CHUNK, (j + 1) * PROJ_CHUNK)
            h = jnp.maximum(jnp.dot(hf, w1_ref[:, cols], preferred_element_type=F32), 0.0)
            h_scr[rs, cols] = (h * h).astype(BF16)

    def ffn_down(r0, _):
        rs = slice(r0, r0 + sub)
        y = jnp.dot(h_scr[rs, :], w2_ref[...], preferred_element_type=F32)
        xo_ref[rs, :] = xn_scr[rs, :] + mod[5:6] * _rms(y, gpost_ffn_ref[...])

    _staggered([pool_mixer, gated_branches, out_proj, residual, ffn_up, ffn_down],
               range(0, tm, sub))


def _pool_inverse_counts(seg_len, tm):
    tps = seg_len // tm
    t = jnp.arange(seg_len)
    if tps > 2:
        t = jnp.concatenate([t[:2 * tm], t[-tm:]])
    cols = []
    for w in POOL_WINDOWS:
        cnt = jnp.minimum(t + w // 2, seg_len) - jnp.maximum(t - w // 2, 0)
        cols.append(jnp.broadcast_to((1.0 / cnt.astype(F32))[:, None], (t.shape[0], POOL_GROUP_DIM)))
    table = jnp.concatenate(cols, axis=1)
    n_blocks = table.shape[0] // tm
    block = lambda pos: jnp.where(pos == tps - 1, n_blocks - 1, jnp.minimum(pos, 1))
    return table, block


def _mix_ffn(x, pool_in, diff_out, gqa_out, gates, mod, layer, mod_row_fn, inv_counts, w_grp,
             pool_scale, w_pool_out, w_diff_out, w_gqa_out, w_o, g_post, g_pre_ffn, w1, w2,
             g_post_ffn, *, seg_len, tm):
    n = x.shape[0]
    tps = seg_len // tm
    hb = tm // POOL_HALO
    n_hblk = n // POOL_HALO
    row = lambda i: (i, 0)
    icnt, icnt_block = inv_counts
    weights = (w_grp, pool_scale, w_pool_out, w_diff_out, w_gqa_out, w_o, g_post, g_pre_ffn,
               w1, w2, g_post_ffn)
    in_specs = [
        pl.BlockSpec((tm, D_MODEL), row),
        pl.BlockSpec((POOL_HALO, POOL_WIDTH), lambda i: (jnp.maximum(i * hb - 1, 0), 0)),
        pl.BlockSpec((tm, POOL_WIDTH), row),
        pl.BlockSpec((POOL_HALO, POOL_WIDTH), lambda i: (jnp.minimum((i + 1) * hb, n_hblk - 1), 0)),
        pl.BlockSpec((tm, POOL_WIDTH), lambda i: (icnt_block(i % tps), 0)),
        pl.BlockSpec((tm, DIFF_WIDTH), row),
        pl.BlockSpec((tm, GQA_WIDTH), row),
        pl.BlockSpec((tm, N_BRANCH * D_MODEL), row),
        pl.BlockSpec((1, 1, 6, D_MODEL), lambda i: (layer, mod_row_fn(i // tps), 0, 0)),
    ] + [_resident(w.shape) for w in weights]
    return pl.pallas_call(
        functools.partial(_mix_ffn_kernel, seg_len=seg_len, tm=tm),
        grid=(n // tm,),
        in_specs=in_specs,
        out_specs=pl.BlockSpec((tm, D_MODEL), row),
        out_shape=jax.ShapeDtypeStruct((n, D_MODEL), F32),
        scratch_shapes=[
            pltpu.VMEM((tm + 2 * POOL_HALO, POOL_WIDTH), F32),
            pltpu.VMEM((tm, D_MODEL), F32),
            pltpu.VMEM((tm, D_MODEL), BF16),
            pltpu.VMEM((tm, D_FF), BF16),
        ],
        compiler_params=_cparams(("arbitrary",)),
        name="mix_ffn",
    )(x, pool_in, pool_in, pool_in, icnt, diff_out, gqa_out, gates, mod, *weights)


def _rope_tables(seq):
    rows = seq // GRID_W
    row = jnp.repeat(jnp.arange(rows), GRID_W).astype(F32)
    col = jnp.tile(jnp.arange(GRID_W), rows).astype(F32)
    inv = 1.0 / (ROPE_BASE ** (jnp.arange(ROPE_PAIRS_PER_AXIS, dtype=F32) * 2.0 / ROPE_AXIS_DIM))
    ang = jnp.concatenate([row[:, None] * inv, col[:, None] * inv], axis=-1)
    cos, sin = jnp.cos(ang), jnp.sin(ang)
    return (jnp.tile(cos, (1, 4)),
            jnp.concatenate([-sin, -sin, sin, sin], axis=-1))


def _pair_layout(w, member_major):
    d = w.shape[0]
    if member_major:
        w5, perm = w.reshape(d, 2, -1, HALF_HEAD, 2), (0, 2, 4, 1, 3)
    else:
        w5, perm = w.reshape(d, -1, 2, HALF_HEAD, 2), (0, 1, 4, 2, 3)
    return w5.transpose(perm).reshape(d, -1)


def _in_proj_weights(w):
    seg = lambda a, b: w[:, a:b]
    return jnp.concatenate([
        seg(OFF_POOL, OFF_DQ),
        _pair_layout(seg(OFF_DQ, OFF_GQ), False),
        _pair_layout(seg(OFF_GQ, OFF_GATE), True),
        seg(OFF_GATE, OFF_DK),
        _pair_layout(seg(OFF_DK, OFF_DV), False),
        seg(OFF_DV, OFF_GK),
        _pair_layout(seg(OFF_GK, OFF_GV), False),
        seg(OFF_GV, IN_WIDTH),
    ], axis=1).astype(BF16)


def _pair_gain(g):
    return jnp.concatenate([g[0::2], g[0::2], g[1::2], g[1::2]]).reshape(1, LANES)


def kernel(x, c, ctx, c_ctx, w_mod, b_mod, g_pre_mix, g_post_mix, g_pre_ffn, g_post_ffn,
           w_in, w_pool_grp, pool_scale, lambda_q1, lambda_k1, lambda_q2, lambda_k2,
           diff_subln, gqa_q_norm, gqa_k_norm, w_pool_out, w_diff_out, w_gqa_out, w_o,
           w_ff1, w_ff2):
    batch, seq, d = x.shape
    ctx_len = ctx.shape[1]
    depth = w_mod.shape[0]
    assert d == D_MODEL and batch < MOD_ROWS and w_in.shape[-1] == IN_WIDTH
    assert GQA_KV_HEADS == 2 and ctx_len % MXU_DIM == 0

    lat_tm, ctx_tm = 512, ctx_len
    rope_tabs = _rope_tables(seq)
    lat_counts = _pool_inverse_counts(seq, lat_tm)
    ctx_counts = _pool_inverse_counts(ctx_len, ctx_tm)
    c_all = jnp.zeros((MOD_ROWS, d), F32).at[:batch].set(c).at[batch].set(c_ctx)
    mod = _modulation(c_all, w_mod, b_mod)
    lat_row = lambda s: s
    ctx_row = lambda s: batch

    xl = x.reshape(batch * seq, d)
    xc = ctx.reshape(batch * ctx_len, d)
    row = lambda a: a.reshape(1, -1)

    for l in range(depth):
        last = l == depth - 1
        lam_init = 0.8 - 0.6 * math.exp(-0.3 * l)
        w_in_b = _in_proj_weights(w_in[l])
        qg2, kg2 = _pair_gain(gqa_q_norm[l]), _pair_gain(gqa_k_norm[l])
        lam_params = jnp.stack([lambda_q1[l], lambda_k1[l], lambda_q2[l], lambda_k2[l]])
        sub_gain = diff_subln[l].reshape(-1, 1)
        mix_w = (w_pool_grp[l].astype(BF16), row(pool_scale[l]), w_pool_out[l].astype(BF16),
                 w_diff_out[l].astype(BF16), w_gqa_out[l].astype(BF16), w_o[l].astype(BF16),
                 row(g_post_mix[l]), row(g_pre_ffn[l]),
                 w_ff1[l].astype(BF16), w_ff2[l].astype(BF16), row(g_post_ffn[l]))

        cp = _in_proj(xc, mod, l, ctx_row, row(g_pre_mix[l]), w_in_b, qg2, kg2, None,
                      seg_len=ctx_len, tm=ctx_tm, q_side=not last)
        c_dk, c_dvt, c_gk, c_gvt = cp[-4:]
        pool_in, dqt, gqt, gates, l_dk, l_dvt, l_gk, l_gvt = _in_proj(
            xl, mod, l, lat_row, row(g_pre_mix[l]), w_in_b, qg2, kg2, rope_tabs,
            seg_len=seq, tm=lat_tm, q_side=True)

        diff_out = _diff_attn(dqt, (c_dk, l_dk), (c_dvt, l_dvt), lam_params, sub_gain,
                              lam_init, pw=PANEL)
        gqa_out = _gqa_attn(gqt, (c_gk, l_gk), (c_gvt, l_gvt), pw=PANEL)
        xl_next = _mix_ffn(xl, pool_in, diff_out, gqa_out, gates, mod, l, lat_row, lat_counts,
                           *mix_w, seg_len=seq, tm=lat_tm)

        if not last:
            c_pool, c_dqt, c_gqt, c_gates = cp[:4]
            c_diff = _diff_attn(c_dqt, (c_dk,), (c_dvt,), lam_params, sub_gain, lam_init,
                                pw=ctx_len)
            c_gqa = _gqa_attn(c_gqt, (c_gk,), (c_gvt,), pw=ctx_len)
            xc = _mix_ffn(xc, c_pool, c_diff, c_gqa, c_gates, mod, l, ctx_row, ctx_counts,
                          *mix_w, seg_len=ctx_len, tm=ctx_tm)
        xl = xl_next
    return xl.reshape(batch, seq, d)
```

```python
import functools
import math

import jax
import jax.numpy as jnp
from jax import lax
from jax.experimental import pallas as pl
from jax.experimental.pallas import tpu as pltpu

F32 = jnp.float32
BF16 = jnp.bfloat16

D_MODEL = 1024
GRID_W = 64
HEAD_DIM = 64
HALF_HEAD = HEAD_DIM // 2
ATTN_SCALE = HEAD_DIM ** -0.5
LOG2E = math.log2(math.e)
ROPE_BASE = 10000.0
ROPE_AXIS_DIM = HEAD_DIM // 2
ROPE_PAIRS_PER_AXIS = ROPE_AXIS_DIM // 2
EPS = 1e-6

POOL_WINDOWS = (2, 4, 8, 16)
POOL_WIDTH = D_MODEL // 2
POOL_GROUP_DIM = POOL_WIDTH // len(POOL_WINDOWS)
POOL_HALO = max(POOL_WINDOWS) // 2

DIFF_HEADS = D_MODEL // 256
DIFF_WIDTH = DIFF_HEADS * 2 * HEAD_DIM
GQA_HEADS = D_MODEL // 128
GQA_KV_HEADS = GQA_HEADS // 4
GQA_GROUP = GQA_HEADS // GQA_KV_HEADS
GQA_WIDTH = GQA_HEADS * HEAD_DIM
GQA_KV_WIDTH = GQA_KV_HEADS * HEAD_DIM
N_BRANCH = 3
D_FF = 4 * D_MODEL

OFF_POOL = 0
OFF_DQ = OFF_POOL + POOL_WIDTH
OFF_GQ = OFF_DQ + DIFF_WIDTH
OFF_GATE = OFF_GQ + GQA_WIDTH
OFF_DK = OFF_GATE + N_BRANCH * D_MODEL
OFF_DV = OFF_DK + DIFF_WIDTH
OFF_GK = OFF_DV + DIFF_WIDTH
OFF_GV = OFF_GK + GQA_KV_WIDTH
IN_WIDTH = OFF_GV + GQA_KV_WIDTH

LANES = 128
MXU_DIM = 256
PANEL = 4 * MXU_DIM
BF16_ROWS = 16
PROJ_CHUNK = 512
SUB_ROWS = 256
VMEM_LIMIT = 56 * 1024 * 1024

MOD_ROWS = 8


def _cparams(sem):
    return pltpu.CompilerParams(dimension_semantics=sem, vmem_limit_bytes=VMEM_LIMIT)


def _resident(shape):
    nd = len(shape)
    return pl.BlockSpec(shape, lambda *_: (0,) * nd, pipeline_mode=pl.Buffered(1))


def _rms(x, g):
    ms = jnp.mean(x * x, axis=-1, keepdims=True)
    return x * lax.rsqrt(ms + EPS) * g


def _staggered(stages, row_starts):
    row_starts = list(row_starts)
    carry = {}
    for step in range(len(row_starts) + len(stages) - 1):
        for k in reversed(range(len(stages))):
            i = step - k
            if 0 <= i < len(row_starts):
                carry[i] = stages[k](row_starts[i], carry.get(i))


def _mod_kernel(c_ref, w_ref, b_ref, o_ref):
    c = c_ref[...]
    a = c * (1.0 / (1.0 + jnp.exp(-c)))
    o_ref[0] = jnp.dot(a.astype(BF16), w_ref[0].astype(BF16),
                       preferred_element_type=F32) + b_ref[0]


def _modulation(c_all, w_mod, b_mod):
    depth = w_mod.shape[0]
    nblk = w_mod.shape[2] // D_MODEL
    out = pl.pallas_call(
        _mod_kernel,
        grid=(depth, nblk),
        in_specs=[
            pl.BlockSpec((MOD_ROWS, D_MODEL), lambda l, j: (0, 0)),
            pl.BlockSpec((1, D_MODEL, D_MODEL), lambda l, j: (l, 0, j)),
            pl.BlockSpec((1, 1, D_MODEL), lambda l, j: (l, 0, j)),
        ],
        out_specs=pl.BlockSpec((1, MOD_ROWS, D_MODEL), lambda l, j: (l, 0, j)),
        out_shape=jax.ShapeDtypeStruct((depth, MOD_ROWS, nblk * D_MODEL), F32),
        compiler_params=_cparams(("arbitrary", "arbitrary")),
        name="modulation",
    )(c_all, w_mod, b_mod.reshape(depth, 1, nblk * D_MODEL))
    return out.reshape(depth, MOD_ROWS, nblk, D_MODEL)


def _rope(x, cos, sin_signed):
    return x * cos + pltpu.roll(x, HEAD_DIM, 1) * sin_signed


def _head_norm(x, gain2):
    lane = lax.broadcasted_iota(jnp.int32, x.shape, 1)
    first = (lane & HALF_HEAD) == 0
    x2 = x * x
    ms_a = jnp.sum(jnp.where(first, x2, 0.0), axis=-1, keepdims=True) * (1.0 / HEAD_DIM)
    ms_b = jnp.sum(jnp.where(first, 0.0, x2), axis=-1, keepdims=True) * (1.0 / HEAD_DIM)
    inv = jnp.where(first, lax.rsqrt(ms_a + EPS), lax.rsqrt(ms_b + EPS))
    return x * inv * gain2


def _in_proj_kernel(*refs, rope, q_side):
    it = iter(refs)
    x_ref, mod_ref, gpre_ref, w_ref, qg_ref, kg_ref = (next(it) for _ in range(6))
    cos_ref = next(it) if rope else None
    sin_ref = next(it) if rope else None
    if q_side:
        pool_ref, dqt_ref, gqt_ref, gate_ref = (next(it) for _ in range(4))
    dk_ref, dvt_ref, gk_ref, gvt_ref = (next(it) for _ in range(4))
    hb_scr = next(it)

    tm = x_ref.shape[0]
    sub = min(tm, SUB_ROWS)
    mod = mod_ref[0, 0]
    q_scale = ATTN_SCALE * LOG2E

    def pre_norm(r0, _):
        rs = slice(r0, r0 + sub)
        h = _rms(x_ref[rs, :], gpre_ref[...]) * (1.0 + mod[1:2]) + mod[0:1]
        hb_scr[rs, :] = h.astype(BF16)

    def project(r0, _):
        rs = slice(r0, r0 + sub)
        hb = hb_scr[rs, :]
        if rope:
            cos, sin = cos_ref[rs, :], sin_ref[rs, :]

        def proj(off, width):
            return jnp.dot(hb, w_ref[:, off:off + width], preferred_element_type=F32)

        def blocks(off, width):
            y = proj(off, width)
            return [y[:, j * LANES:(j + 1) * LANES] for j in range(width // LANES)]

        def store_heads_t(ref, idx_a, idx_b, y):
            yt = y.T
            row = lax.broadcasted_iota(jnp.int32, yt.shape, 0)
            first = (row & HALF_HEAD) == 0
            ref[idx_a + (slice(None), rs)] = jnp.where(first, yt, 0.0).astype(BF16)
            ref[idx_b + (slice(None), rs)] = jnp.where(first, 0.0, yt).astype(BF16)

        def diff_keys():
            for hh, y in enumerate(blocks(OFF_DK, DIFF_WIDTH)):
                if rope:
                    y = _rope(y, cos, sin)
                dk_ref[rs, hh * LANES:(hh + 1) * LANES] = y.astype(BF16)

        def gqa_keys():
            (y,) = blocks(OFF_GK, GQA_KV_WIDTH)
            y = _head_norm(y, kg_ref[...])
            if rope:
                y = _rope(y, cos, sin)
            gk_ref[rs, :] = y.astype(BF16)

        def diff_queries():
            for hh, y in enumerate(blocks(OFF_DQ, DIFF_WIDTH)):
                if rope:
                    y = _rope(y, cos, sin)
                store_heads_t(dqt_ref, (0, hh, 0), (0, hh, 1), y * q_scale)

        def gqa_queries():
            for jj, y in enumerate(blocks(OFF_GQ, GQA_WIDTH)):
                y = _head_norm(y, qg_ref[...])
                if rope:
                    y = _rope(y, cos, sin)
                store_heads_t(gqt_ref, (0, 0, jj), (0, 1, jj), y * q_scale)

        def values():
            for hh, y in enumerate(blocks(OFF_DV, DIFF_WIDTH)):
                dvt_ref[0, hh, :, rs] = y.T.astype(BF16)
            (y,) = blocks(OFF_GV, GQA_KV_WIDTH)
            gvt_ref[0, :, rs] = y.T.astype(BF16)

        def gates(j):
            gl = proj(OFF_GATE + j * PROJ_CHUNK, PROJ_CHUNK)
            gate_ref[rs, j * PROJ_CHUNK:(j + 1) * PROJ_CHUNK] = (
                1.0 / (1.0 + jnp.exp(-gl))).astype(BF16)

        if q_side:
            n_gate = N_BRANCH * D_MODEL // PROJ_CHUNK
            order = [diff_keys, gqa_keys, 0, 1, diff_queries, 2, 3, gqa_queries, 4, 5, values]
            assert sorted(s for s in order if isinstance(s, int)) == list(range(n_gate))
            for step in order:
                if isinstance(step, int):
                    gates(step)
                else:
                    step()
            pool_ref[rs, :] = proj(OFF_POOL, POOL_WIDTH)
        else:
            diff_keys()
            gqa_keys()
            values()

    _staggered([pre_norm, project], range(0, tm, sub))


def _in_proj(x, mod, layer, mod_row_fn, g_pre, w_in, q_gain2, k_gain2, rope_tabs, *,
             seg_len, tm, q_side):
    n = x.shape[0]
    n_seg = n // seg_len
    tps = seg_len // tm
    rope = rope_tabs is not None
    seg = lambda i: i // tps
    pos = lambda i: i % tps
    row = lambda i: (i, 0)

    in_specs = [
        pl.BlockSpec((tm, D_MODEL), row),
        pl.BlockSpec((1, 1, 6, D_MODEL), lambda i: (layer, mod_row_fn(seg(i)), 0, 0)),
        _resident((1, D_MODEL)),
        _resident(w_in.shape),
        _resident((1, LANES)),
        _resident((1, LANES)),
    ]
    args = [x, mod, g_pre, w_in, q_gain2, k_gain2]
    if rope:
        in_specs += [pl.BlockSpec((tm, LANES), lambda i: (pos(i), 0))] * 2
        args += list(rope_tabs)

    out_shape, out_specs = [], []
    if q_side:
        out_shape += [
            jax.ShapeDtypeStruct((n, POOL_WIDTH), F32),
            jax.ShapeDtypeStruct((n_seg, DIFF_HEADS, 2, LANES, seg_len), BF16),
            jax.ShapeDtypeStruct((n_seg, GQA_KV_HEADS, GQA_GROUP, LANES, seg_len), BF16),
            jax.ShapeDtypeStruct((n, N_BRANCH * D_MODEL), BF16),
        ]
        out_specs += [
            pl.BlockSpec((tm, POOL_WIDTH), row),
            pl.BlockSpec((1, DIFF_HEADS, 2, LANES, tm), lambda i: (seg(i), 0, 0, 0, pos(i))),
            pl.BlockSpec((1, GQA_KV_HEADS, GQA_GROUP, LANES, tm),
                         lambda i: (seg(i), 0, 0, 0, pos(i))),
            pl.BlockSpec((tm, N_BRANCH * D_MODEL), row),
        ]
    out_shape += [
        jax.ShapeDtypeStruct((n, DIFF_WIDTH), BF16),
        jax.ShapeDtypeStruct((n_seg, DIFF_HEADS, 2 * HEAD_DIM, seg_len), BF16),
        jax.ShapeDtypeStruct((n, GQA_KV_WIDTH), BF16),
        jax.ShapeDtypeStruct((n_seg, GQA_KV_WIDTH, seg_len), BF16),
    ]
    out_specs += [
        pl.BlockSpec((tm, DIFF_WIDTH), row),
        pl.BlockSpec((1, DIFF_HEADS, 2 * HEAD_DIM, tm), lambda i: (seg(i), 0, 0, pos(i))),
        pl.BlockSpec((tm, GQA_KV_WIDTH), row),
        pl.BlockSpec((1, GQA_KV_WIDTH, tm), lambda i: (seg(i), 0, pos(i))),
    ]
    return pl.pallas_call(
        functools.partial(_in_proj_kernel, rope=rope, q_side=q_side),
        grid=(n // tm,),
        in_specs=in_specs,
        out_specs=out_specs,
        out_shape=out_shape,
        scratch_shapes=[pltpu.VMEM((tm, D_MODEL), BF16)],
        compiler_params=_cparams(("arbitrary",)),
        name="in_proj" + ("_rope" if rope else "") + ("" if q_side else "_kv"),
    )(*args)


class _Panels:
    def __init__(self, n_b, n_h, lq, pw, per_out):
        self.n_h, self.per_out, self.tiles = n_h, per_out, lq // pw
        self.per_head = per_out * self.tiles
        self.total = n_b * n_h * self.per_head

    def split(self, q):
        bh, loc = q // self.per_head, q % self.per_head
        return bh // self.n_h, bh % self.n_h, loc % self.per_out, loc // self.per_out

    @property
    def steps(self):
        return self.total + 2

    def loading(self, g):
        return self.split(jnp.minimum(g, self.total - 1))

    def emitting(self, g):
        return self.split(jnp.clip(g - 2, 0, self.total - 1))


def _attn_scratch(nk, pw, v_rows, per_out):
    return [
        pltpu.VMEM((2, nk, pw), F32),
        pltpu.VMEM((2, 8, pw), F32),
        pltpu.VMEM((per_out, v_rows, pw), F32),
        pltpu.VMEM((2, v_rows + BF16_ROWS, nk), BF16),
    ]


def _attn_pipeline(qt_ref, k_refs, vt_refs, s_scr, m_scr, on_scr, vt_scr, *, panels, emit):
    g = pl.program_id(0)
    total, per_head, per_out = panels.total, panels.per_head, panels.per_out
    q_load = jnp.minimum(g, total - 1)
    q_done = jnp.maximum(g - 1, 0)
    v_rows = on_scr.shape[1]

    @pl.when((g < total) & (q_load % per_head == 0))
    def _():
        dst = vt_scr.at[(q_load // per_head) & 1]
        c = 0
        for vt_ref in vt_refs:
            vt = vt_ref[...]
            dst[0:v_rows, c:c + vt.shape[-1]] = vt.reshape(vt.shape[-2:])
            c += vt.shape[-1]
        dst[v_rows:, :] = jnp.ones((dst.shape[0] - v_rows, dst.shape[1]), BF16)

    key_chunks = [(k_ref, c0) for k_ref in k_refs for c0 in range(0, k_ref.shape[0], MXU_DIM)]

    def phase(score_buf, value_buf):
        if score_buf is not None:
            qt = qt_ref[...].reshape(qt_ref.shape[-2:])
        if value_buf is not None:
            m_prev = m_scr[value_buf][0:1]
            vt = vt_scr.at[(q_done // per_head) & 1]
        m, acc = None, None
        for c, (k_ref, c0) in enumerate(key_chunks):
            rows = slice(c * MXU_DIM, (c + 1) * MXU_DIM)
            if value_buf is not None:
                pt = jnp.exp2(s_scr[value_buf, rows, :] - m_prev).astype(BF16)
                d = jnp.dot(vt[:, rows], pt, preferred_element_type=F32)
                acc = d if acc is None else acc + d
            if score_buf is not None:
                s = jnp.dot(k_ref[c0:c0 + MXU_DIM, :], qt, preferred_element_type=F32)
                s_scr[score_buf, rows, :] = s
                mc = jnp.max(s, axis=0, keepdims=True)
                m = mc if m is None else jnp.maximum(m, mc)
        if score_buf is not None:
            m_scr[score_buf] = jnp.broadcast_to(m, m_scr.shape[1:])
        if value_buf is not None:
            on_scr[q_done % per_out] = acc[0:v_rows] / acc[v_rows:v_rows + 1]

    @pl.when(g == 0)
    def _():
        on_scr[...] = jnp.zeros(on_scr.shape, F32)
        phase(0, None)

    for parity in range(2):
        @pl.when((g > 0) & (g < total) & ((g & 1) == parity))
        def _():
            emit()
            phase(parity, 1 - parity)

    @pl.when(g == total)
    def _():
        emit()
        phase(None, (total - 1) % 2)

    @pl.when(g == total + 1)
    def _():
        emit()


def _diff_attn_kernel(*refs, n_seg, panels, lam_init):
    it = iter(refs)
    lam_ref, subg_ref, qt_ref = next(it), next(it), next(it)
    k_refs = [next(it) for _ in range(n_seg)]
    vt_refs = [next(it) for _ in range(n_seg)]
    o_ref, s_scr, m_scr, on_scr, vt_scr = (next(it) for _ in range(5))

    def emit():
        lp = lam_ref[...]
        lam = (jnp.exp(jnp.sum(lp[0:1] * lp[1:2], axis=-1, keepdims=True))
               - jnp.exp(jnp.sum(lp[2:3] * lp[3:4], axis=-1, keepdims=True)) + lam_init)
        o = on_scr[0] - lam * on_scr[1]
        ms = jnp.mean(o * o, axis=0, keepdims=True)
        o = o * lax.rsqrt(ms + EPS) * subg_ref[...] * (1.0 - lam_init)
        o_ref[...] = o.T.astype(BF16)

    _attn_pipeline(qt_ref, k_refs, vt_refs, s_scr, m_scr, on_scr, vt_scr,
                   panels=panels, emit=emit)


def _diff_attn(dqt, ks, vts, lam_params, sub_gain_col, lam_init, *, pw):
    n_b, n_h, per_out, _, lq = dqt.shape
    lks = tuple(vt.shape[-1] for vt in vts)
    vw = 2 * HEAD_DIM
    panels = _Panels(n_b, n_h, lq, pw, per_out)

    def qt_map(g):
        b, h, j, n = panels.loading(g)
        return b, h, j, 0, n

    def k_map(g):
        b, h, _, _ = panels.loading(g)
        return b, h

    def vt_map(g):
        b, h, _, _ = panels.loading(g)
        return b, h, 0, 0

    def out_map(g):
        b, h, _, n = panels.emitting(g)
        return b * panels.tiles + n, h

    in_specs = [
        _resident(lam_params.shape),
        _resident(sub_gain_col.shape),
        pl.BlockSpec((1, 1, 1, LANES, pw), qt_map),
    ]
    in_specs += [pl.BlockSpec((lk, LANES), k_map) for lk in lks]
    in_specs += [pl.BlockSpec((1, 1, vw, lk), vt_map) for lk in lks]
    return pl.pallas_call(
        functools.partial(_diff_attn_kernel, n_seg=len(lks), panels=panels, lam_init=lam_init),
        grid=(panels.steps,),
        in_specs=in_specs,
        out_specs=pl.BlockSpec((pw, vw), out_map),
        out_shape=jax.ShapeDtypeStruct((n_b * lq, DIFF_WIDTH), BF16),
        scratch_shapes=_attn_scratch(sum(lks), pw, vw, per_out),
        compiler_params=_cparams(("arbitrary",)),
        name="diff_attn",
    )(lam_params, sub_gain_col, dqt, *ks, *vts)


def _gqa_attn_kernel(*refs, n_seg, panels):
    it = iter(refs)
    qt_ref = next(it)
    k_refs = [next(it) for _ in range(n_seg)]
    vt_refs = [next(it) for _ in range(n_seg)]
    o_ref, s_scr, m_scr, on_scr, vt_scr = (next(it) for _ in range(5))

    def emit():
        pairs = [jnp.concatenate([on_scr[2 * i], on_scr[2 * i + 1]], axis=0).T
                 for i in range(GQA_GROUP // 2)]
        o_ref[...] = jnp.concatenate(pairs, axis=1).astype(BF16)

    _attn_pipeline(qt_ref, k_refs, vt_refs, s_scr, m_scr, on_scr, vt_scr,
                   panels=panels, emit=emit)


def _gqa_attn(gqt, ks, vts, *, pw):
    n_b, n_h, per_out, _, lq = gqt.shape
    lks = tuple(vt.shape[-1] for vt in vts)
    panels = _Panels(n_b, n_h, lq, pw, per_out)

    def qt_map(g):
        b, h, j, n = panels.loading(g)
        return b, h, j, 0, n

    def k_map(g):
        return panels.loading(g)[0], 0

    def vt_map(g):
        b, h, _, _ = panels.loading(g)
        return b, h, 0

    def out_map(g):
        b, h, _, n = panels.emitting(g)
        return b * panels.tiles + n, h

    in_specs = [pl.BlockSpec((1, 1, 1, LANES, pw), qt_map)]
    in_specs += [pl.BlockSpec((lk, LANES), k_map) for lk in lks]
    in_specs += [pl.BlockSpec((1, HEAD_DIM, lk), vt_map) for lk in lks]
    return pl.pallas_call(
        functools.partial(_gqa_attn_kernel, n_seg=len(lks), panels=panels),
        grid=(panels.steps,),
        in_specs=in_specs,
        out_specs=pl.BlockSpec((pw, GQA_GROUP * HEAD_DIM), out_map),
        out_shape=jax.ShapeDtypeStruct((n_b * lq, GQA_WIDTH), BF16),
        scratch_shapes=_attn_scratch(sum(lks), pw, HEAD_DIM, per_out),
        compiler_params=_cparams(("arbitrary",)),
        name="gqa_attn",
    )(gqt, *ks, *vts)


def _mix_ffn_kernel(x_ref, zp_ref, z_ref, zn_ref, icnt_ref, do_ref, go_ref, gate_ref, mod_ref,
                    wgrp_ref, pscale_ref, wpo_ref, wdo_ref, wgo_ref, wo_ref,
                    gpost_ref, gpre_ref, w1_ref, w2_ref, gpost_ffn_ref,
                    xo_ref, zext_scr, xn_scr, hf_scr, h_scr, *, seg_len, tm):
    i = pl.program_id(0)
    tps = seg_len // tm
    t = i % tps
    z = z_ref[...]
    zext_scr[0:POOL_HALO] = jnp.where(t > 0, zp_ref[...], 0.0)
    zext_scr[POOL_HALO:POOL_HALO + tm] = z
    zext_scr[POOL_HALO + tm:] = jnp.where(t < tps - 1, zn_ref[...], 0.0)
    mod = mod_ref[0, 0]
    sub = min(tm, SUB_ROWS)

    def pool_mixer(r0, _):
        mixed = []
        for g, w in enumerate(POOL_WINDOWS):
            cs = slice(g * POOL_GROUP_DIM, (g + 1) * POOL_GROUP_DIM)
            ssum = None
            for d in range(-(w // 2), w // 2):
                zz = zext_scr[POOL_HALO + d + r0:POOL_HALO + d + r0 + sub, cs]
                ssum = zz if ssum is None else ssum + zz
            pooled = ssum * icnt_ref[r0:r0 + sub, cs] - z_ref[r0:r0 + sub, cs]
            mixed.append(jnp.dot(pooled.astype(BF16), wgrp_ref[g], preferred_element_type=F32))
        return (jnp.concatenate(mixed, axis=1) * pscale_ref[...]).astype(BF16)

    def gated_branches(r0, pool_out):
        rs = slice(r0, r0 + sub)

        def gate(k):
            return gate_ref[rs, k * D_MODEL:(k + 1) * D_MODEL].astype(F32)

        merged = gate(0) * jnp.dot(pool_out, wpo_ref[...], preferred_element_type=F32)
        merged += gate(1) * jnp.dot(do_ref[rs, :], wdo_ref[...], preferred_element_type=F32)
        merged += gate(2) * jnp.dot(go_ref[rs, :], wgo_ref[...], preferred_element_type=F32)
        return merged.astype(BF16)

    def out_proj(r0, merged):
        return jnp.dot(merged, wo_ref[...], preferred_element_type=F32)

    def residual(r0, y):
        rs = slice(r0, r0 + sub)
        xn = x_ref[rs, :] + mod[2:3] * _rms(y, gpost_ref[...])
        xn_scr[rs, :] = xn
        hf_scr[rs, :] = (_rms(xn, gpre_ref[...]) * (1.0 + mod[4:5]) + mod[3:4]).astype(BF16)

    def ffn_up(r0, _):
        rs = slice(r0, r0 + sub)
        hf = hf_scr[rs, :]
        for j in range(D_FF // PROJ_CHUNK):
            cols = slice(j * PROJ_CHUNK, (j + 1) * PROJ_CHUNK)
            h = jnp.maximum(jnp.dot(hf, w1_ref[:, cols], preferred_element_type=F32), 0.0)
            h_scr[rs, cols] = (h * h).astype(BF16)

    def ffn_down(r0, _):
        rs = slice(r0, r0 + sub)
        y = jnp.dot(h_scr[rs, :], w2_ref[...], preferred_element_type=F32)
        xo_ref[rs, :] = xn_scr[rs, :] + mod[5:6] * _rms(y, gpost_ffn_ref[...])

    _staggered([pool_mixer, gated_branches, out_proj, residual, ffn_up, ffn_down],
               range(0, tm, sub))


def _pool_inverse_counts(seg_len, tm):
    tps = seg_len // tm
    t = jnp.arange(seg_len)
    if tps > 2:
        t = jnp.concatenate([t[:2 * tm], t[-tm:]])
    cols = []
    for w in POOL_WINDOWS:
        cnt = jnp.minimum(t + w // 2, seg_len) - jnp.maximum(t - w // 2, 0)
        cols.append(jnp.broadcast_to((1.0 / cnt.astype(F32))[:, None], (t.shape[0], POOL_GROUP_DIM)))
    table = jnp.concatenate(cols, axis=1)
    n_blocks = table.shape[0] // tm
    block = lambda pos: jnp.where(pos == tps - 1, n_blocks - 1, jnp.minimum(pos, 1))
    return table, block


def _mix_ffn(x, pool_in, diff_out, gqa_out, gates, mod, layer, mod_row_fn, inv_counts, w_grp,
             pool_scale, w_pool_out, w_diff_out, w_gqa_out, w_o, g_post, g_pre_ffn, w1, w2,
             g_post_ffn, *, seg_len, tm):
    n = x.shape[0]
    tps = seg_len // tm
    hb = tm // POOL_HALO
    n_hblk = n // POOL_HALO
    row = lambda i: (i, 0)
    icnt, icnt_block = inv_counts
    weights = (w_grp, pool_scale, w_pool_out, w_diff_out, w_gqa_out, w_o, g_post, g_pre_ffn,
               w1, w2, g_post_ffn)
    in_specs = [
        pl.BlockSpec((tm, D_MODEL), row),
        pl.BlockSpec((POOL_HALO, POOL_WIDTH), lambda i: (jnp.maximum(i * hb - 1, 0), 0)),
        pl.BlockSpec((tm, POOL_WIDTH), row),
        pl.BlockSpec((POOL_HALO, POOL_WIDTH), lambda i: (jnp.minimum((i + 1) * hb, n_hblk - 1), 0)),
        pl.BlockSpec((tm, POOL_WIDTH), lambda i: (icnt_block(i % tps), 0)),
        pl.BlockSpec((tm, DIFF_WIDTH), row),
        pl.BlockSpec((tm, GQA_WIDTH), row),
        pl.BlockSpec((tm, N_BRANCH * D_MODEL), row),
        pl.BlockSpec((1, 1, 6, D_MODEL), lambda i: (layer, mod_row_fn(i // tps), 0, 0)),
    ] + [_resident(w.shape) for w in weights]
    return pl.pallas_call(
        functools.partial(_mix_ffn_kernel, seg_len=seg_len, tm=tm),
        grid=(n // tm,),
        in_specs=in_specs,
        out_specs=pl.BlockSpec((tm, D_MODEL), row),
        out_shape=jax.ShapeDtypeStruct((n, D_MODEL), F32),
        scratch_shapes=[
            pltpu.VMEM((tm + 2 * POOL_HALO, POOL_WIDTH), F32),
            pltpu.VMEM((tm, D_MODEL), F32),
            pltpu.VMEM((tm, D_MODEL), BF16),
            pltpu.VMEM((tm, D_FF), BF16),
        ],
        compiler_params=_cparams(("arbitrary",)),
        name="mix_ffn",
    )(x, pool_in, pool_in, pool_in, icnt, diff_out, gqa_out, gates, mod, *weights)


def _rope_tables(seq):
    rows = seq // GRID_W
    row = jnp.repeat(jnp.arange(rows), GRID_W).astype(F32)
    col = jnp.tile(jnp.arange(GRID_W), rows).astype(F32)
    inv = 1.0 / (ROPE_BASE ** (jnp.arange(ROPE_PAIRS_PER_AXIS, dtype=F32) * 2.0 / ROPE_AXIS_DIM))
    ang = jnp.concatenate([row[:, None] * inv, col[:, None] * inv], axis=-1)
    cos, sin = jnp.cos(ang), jnp.sin(ang)
    return (jnp.tile(cos, (1, 4)),
            jnp.concatenate([-sin, -sin, sin, sin], axis=-1))


def _pair_layout(w, member_major):
    d = w.shape[0]
    if member_major:
        w5, perm = w.reshape(d, 2, -1, HALF_HEAD, 2), (0, 2, 4, 1, 3)
    else:
        w5, perm = w.reshape(d, -1, 2, HALF_HEAD, 2), (0, 1, 4, 2, 3)
    return w5.transpose(perm).reshape(d, -1)


def _in_proj_weights(w):
    seg = lambda a, b: w[:, a:b]
    return jnp.concatenate([
        seg(OFF_POOL, OFF_DQ),
        _pair_layout(seg(OFF_DQ, OFF_GQ), False),
        _pair_layout(seg(OFF_GQ, OFF_GATE), True),
        seg(OFF_GATE, OFF_DK),
        _pair_layout(seg(OFF_DK, OFF_DV), False),
        seg(OFF_DV, OFF_GK),
        _pair_layout(seg(OFF_GK, OFF_GV), False),
        seg(OFF_GV, IN_WIDTH),
    ], axis=1).astype(BF16)


def _pair_gain(g):
    return jnp.concatenate([g[0::2], g[0::2], g[1::2], g[1::2]]).reshape(1, LANES)


def kernel(x, c, ctx, c_ctx, w_mod, b_mod, g_pre_mix, g_post_mix, g_pre_ffn, g_post_ffn,
           w_in, w_pool_grp, pool_scale, lambda_q1, lambda_k1, lambda_q2, lambda_k2,
           diff_subln, gqa_q_norm, gqa_k_norm, w_pool_out, w_diff_out, w_gqa_out, w_o,
           w_ff1, w_ff2):
    batch, seq, d = x.shape
    ctx_len = ctx.shape[1]
    depth = w_mod.shape[0]
    assert d == D_MODEL and batch < MOD_ROWS and w_in.shape[-1] == IN_WIDTH
    assert GQA_KV_HEADS == 2 and ctx_len % MXU_DIM == 0

    lat_tm, ctx_tm = 512, ctx_len
    rope_tabs = _rope_tables(seq)
    lat_counts = _pool_inverse_counts(seq, lat_tm)
    ctx_counts = _pool_inverse_counts(ctx_len, ctx_tm)
    c_all = jnp.zeros((MOD_ROWS, d), F32).at[:batch].set(c).at[batch].set(c_ctx)
    mod = _modulation(c_all, w_mod, b_mod)
    lat_row = lambda s: s
    ctx_row = lambda s: batch

    xl = x.reshape(batch * seq, d)
    xc = ctx.reshape(batch * ctx_len, d)
    row = lambda a: a.reshape(1, -1)

    for l in range(depth):
        last = l == depth - 1
        lam_init = 0.8 - 0.6 * math.exp(-0.3 * l)
        w_in_b = _in_proj_weights(w_in[l])
        qg2, kg2 = _pair_gain(gqa_q_norm[l]), _pair_gain(gqa_k_norm[l])
        lam_params = jnp.stack([lambda_q1[l], lambda_k1[l], lambda_q2[l], lambda_k2[l]])
        sub_gain = diff_subln[l].reshape(-1, 1)
        mix_w = (w_pool_grp[l].astype(BF16), row(pool_scale[l]), w_pool_out[l].astype(BF16),
                 w_diff_out[l].astype(BF16), w_gqa_out[l].astype(BF16), w_o[l].astype(BF16),
                 row(g_post_mix[l]), row(g_pre_ffn[l]),
                 w_ff1[l].astype(BF16), w_ff2[l].astype(BF16), row(g_post_ffn[l]))

        cp = _in_proj(xc, mod, l, ctx_row, row(g_pre_mix[l]), w_in_b, qg2, kg2, None,
                      seg_len=ctx_len, tm=ctx_tm, q_side=not last)
        c_dk, c_dvt, c_gk, c_gvt = cp[-4:]
        pool_in, dqt, gqt, gates, l_dk, l_dvt, l_gk, l_gvt = _in_proj(
            xl, mod, l, lat_row, row(g_pre_mix[l]), w_in_b, qg2, kg2, rope_tabs,
            seg_len=seq, tm=lat_tm, q_side=True)

        diff_out = _diff_attn(dqt, (c_dk, l_dk), (c_dvt, l_dvt), lam_params, sub_gain,
                              lam_init, pw=PANEL // 2)
        gqa_out = _gqa_attn(gqt, (c_gk, l_gk), (c_gvt, l_gvt), pw=PANEL // 2)
        xl_next = _mix_ffn(xl, pool_in, diff_out, gqa_out, gates, mod, l, lat_row, lat_counts,
                           *mix_w, seg_len=seq, tm=lat_tm)

        if not last:
            c_pool, c_dqt, c_gqt, c_gates = cp[:4]
            c_diff = _diff_attn(c_dqt, (c_dk,), (c_dvt,), lam_params, sub_gain, lam_init,
                                pw=ctx_len)
            c_gqa = _gqa_attn(c_gqt, (c_gk,), (c_gvt,), pw=ctx_len)
            xc = _mix_ffn(xc, c_pool, c_diff, c_gqa, c_gates, mod, l, ctx_row, ctx_counts,
                          *mix_w, seg_len=ctx_len, tm=ctx_tm)
        xl = xl_next
    return xl.reshape(batch, seq, d)
```

```python
import functools
import math

import jax
import jax.numpy as jnp
from jax import lax
from jax.experimental import pallas as pl
from jax.experimental.pallas import tpu as pltpu

F32 = jnp.float32
BF16 = jnp.bfloat16

D_MODEL = 1024
GRID_W = 64
HEAD_DIM = 64
HALF_HEAD = HEAD_DIM // 2
ATTN_SCALE = HEAD_DIM ** -0.5
LOG2E = math.log2(math.e)
ROPE_BASE = 10000.0
ROPE_AXIS_DIM = HEAD_DIM // 2
ROPE_PAIRS_PER_AXIS = ROPE_AXIS_DIM // 2
EPS = 1e-6

POOL_WINDOWS = (2, 4, 8, 16)
POOL_WIDTH = D_MODEL // 2
POOL_GROUP_DIM = POOL_WIDTH // len(POOL_WINDOWS)
POOL_HALO = max(POOL_WINDOWS) // 2

DIFF_HEADS = D_MODEL // 256
DIFF_WIDTH = DIFF_HEADS * 2 * HEAD_DIM
GQA_HEADS = D_MODEL // 128
GQA_KV_HEADS = GQA_HEADS // 4
GQA_GROUP = GQA_HEADS // GQA_KV_HEADS
GQA_WIDTH = GQA_HEADS * HEAD_DIM
GQA_KV_WIDTH = GQA_KV_HEADS * HEAD_DIM
N_BRANCH = 3
D_FF = 4 * D_MODEL

OFF_POOL = 0
OFF_DQ = OFF_POOL + POOL_WIDTH
OFF_GQ = OFF_DQ + DIFF_WIDTH
OFF_GATE = OFF_GQ + GQA_WIDTH
OFF_DK = OFF_GATE + N_BRANCH * D_MODEL
OFF_DV = OFF_DK + DIFF_WIDTH
OFF_GK = OFF_DV + DIFF_WIDTH
OFF_GV = OFF_GK + GQA_KV_WIDTH
IN_WIDTH = OFF_GV + GQA_KV_WIDTH

LANES = 128
MXU_DIM = 256
PANEL = 4 * MXU_DIM
BF16_ROWS = 16
PROJ_CHUNK = 512
SUB_ROWS = 256
VMEM_LIMIT = 56 * 1024 * 1024

MOD_ROWS = 8


def _cparams(sem):
    return pltpu.CompilerParams(dimension_semantics=sem, vmem_limit_bytes=VMEM_LIMIT)


def _resident(shape):
    nd = len(shape)
    return pl.BlockSpec(shape, lambda *_: (0,) * nd, pipeline_mode=pl.Buffered(1))


def _rms(x, g):
    ms = jnp.mean(x * x, axis=-1, keepdims=True)
    return x * lax.rsqrt(ms + EPS) * g


def _staggered(stages, row_starts):
    row_starts = list(row_starts)
    carry = {}
    for step in range(len(row_starts) + len(stages) - 1):
        for k in reversed(range(len(stages))):
            i = step - k
            if 0 <= i < len(row_starts):
                carry[i] = stages[k](row_starts[i], carry.get(i))


def _mod_kernel(c_ref, w_ref, b_ref, o_ref):
    c = c_ref[...]
    a = c * (1.0 / (1.0 + jnp.exp(-c)))
    o_ref[0] = jnp.dot(a.astype(BF16), w_ref[0].astype(BF16),
                       preferred_element_type=F32) + b_ref[0]


def _modulation(c_all, w_mod, b_mod):
    depth = w_mod.shape[0]
    nblk = w_mod.shape[2] // D_MODEL
    out = pl.pallas_call(
        _mod_kernel,
        grid=(depth, nblk),
        in_specs=[
            pl.BlockSpec((MOD_ROWS, D_MODEL), lambda l, j: (0, 0)),
            pl.BlockSpec((1, D_MODEL, D_MODEL), lambda l, j: (l, 0, j)),
            pl.BlockSpec((1, 1, D_MODEL), lambda l, j: (l, 0, j)),
        ],
        out_specs=pl.BlockSpec((1, MOD_ROWS, D_MODEL), lambda l, j: (l, 0, j)),
        out_shape=jax.ShapeDtypeStruct((depth, MOD_ROWS, nblk * D_MODEL), F32),
        compiler_params=_cparams(("arbitrary", "arbitrary")),
        name="modulation",
    )(c_all, w_mod, b_mod.reshape(depth, 1, nblk * D_MODEL))
    return out.reshape(depth, MOD_ROWS, nblk, D_MODEL)


def _rope(x, cos, sin_signed):
    return x * cos + pltpu.roll(x, HEAD_DIM, 1) * sin_signed


def _head_norm(x, gain2):
    lane = lax.broadcasted_iota(jnp.int32, x.shape, 1)
    first = (lane & HALF_HEAD) == 0
    x2 = x * x
    ms_a = jnp.sum(jnp.where(first, x2, 0.0), axis=-1, keepdims=True) * (1.0 / HEAD_DIM)
    ms_b = jnp.sum(jnp.where(first, 0.0, x2), axis=-1, keepdims=True) * (1.0 / HEAD_DIM)
    inv = jnp.where(first, lax.rsqrt(ms_a + EPS), lax.rsqrt(ms_b + EPS))
    return x * inv * gain2


def _in_proj_kernel(*refs, rope, q_side):
    it = iter(refs)
    x_ref, mod_ref, gpre_ref, w_ref, qg_ref, kg_ref = (next(it) for _ in range(6))
    cos_ref = next(it) if rope else None
    sin_ref = next(it) if rope else None
    if q_side:
        pool_ref, dqt_ref, gqt_ref, gate_ref = (next(it) for _ in range(4))
    dk_ref, dvt_ref, gk_ref, gvt_ref = (next(it) for _ in range(4))
    hb_scr = next(it)

    tm = x_ref.shape[0]
    sub = min(tm, SUB_ROWS)
    mod = mod_ref[0, 0]
    q_scale = ATTN_SCALE * LOG2E

    def pre_norm(r0, _):
        rs = slice(r0, r0 + sub)
        h = _rms(x_ref[rs, :], gpre_ref[...]) * (1.0 + mod[1:2]) + mod[0:1]
        hb_scr[rs, :] = h.astype(BF16)

    def project(r0, _):
        rs = slice(r0, r0 + sub)
        hb = hb_scr[rs, :]
        if rope:
            cos, sin = cos_ref[rs, :], sin_ref[rs, :]

        def proj(off, width):
            return jnp.dot(hb, w_ref[:, off:off + width], preferred_element_type=F32)

        def blocks(off, width):
            y = proj(off, width)
            return [y[:, j * LANES:(j + 1) * LANES] for j in range(width // LANES)]

        def store_heads_t(ref, idx_a, idx_b, y):
            yt = y.T
            row = lax.broadcasted_iota(jnp.int32, yt.shape, 0)
            first = (row & HALF_HEAD) == 0
            ref[idx_a + (slice(None), rs)] = jnp.where(first, yt, 0.0).astype(BF16)
            ref[idx_b + (slice(None), rs)] = jnp.where(first, 0.0, yt).astype(BF16)

        def diff_keys():
            for hh, y in enumerate(blocks(OFF_DK, DIFF_WIDTH)):
                if rope:
                    y = _rope(y, cos, sin)
                dk_ref[rs, hh * LANES:(hh + 1) * LANES] = y.astype(BF16)

        def gqa_keys():
            (y,) = blocks(OFF_GK, GQA_KV_WIDTH)
            y = _head_norm(y, kg_ref[...])
            if rope:
                y = _rope(y, cos, sin)
            gk_ref[rs, :] = y.astype(BF16)

        def diff_queries():
            for hh, y in enumerate(blocks(OFF_DQ, DIFF_WIDTH)):
                if rope:
                    y = _rope(y, cos, sin)
                store_heads_t(dqt_ref, (0, hh, 0), (0, hh, 1), y * q_scale)

        def gqa_queries():
            for jj, y in enumerate(blocks(OFF_GQ, GQA_WIDTH)):
                y = _head_norm(y, qg_ref[...])
                if rope:
                    y = _rope(y, cos, sin)
                store_heads_t(gqt_ref, (0, 0, jj), (0, 1, jj), y * q_scale)

        def values():
            for hh, y in enumerate(blocks(OFF_DV, DIFF_WIDTH)):
                dvt_ref[0, hh, :, rs] = y.T.astype(BF16)
            (y,) = blocks(OFF_GV, GQA_KV_WIDTH)
            gvt_ref[0, :, rs] = y.T.astype(BF16)

        def gates(j):
            gl = proj(OFF_GATE + j * PROJ_CHUNK, PROJ_CHUNK)
            gate_ref[rs, j * PROJ_CHUNK:(j + 1) * PROJ_CHUNK] = (
                1.0 / (1.0 + jnp.exp(-gl))).astype(BF16)

        if q_side:
            n_gate = N_BRANCH * D_MODEL // PROJ_CHUNK
            order = [diff_keys, gqa_keys, 0, 1, diff_queries, 2, 3, gqa_queries, 4, 5, values]
            assert sorted(s for s in order if isinstance(s, int)) == list(range(n_gate))
            for step in order:
                if isinstance(step, int):
                    gates(step)
                else:
                    step()
            pool_ref[rs, :] = proj(OFF_POOL, POOL_WIDTH)
        else:
            diff_keys()
            gqa_keys()
            values()

    _staggered([pre_norm, project], range(0, tm, sub))


def _in_proj(x, mod, layer, mod_row_fn, g_pre, w_in, q_gain2, k_gain2, rope_tabs, *,
             seg_len, tm, q_side):
    n = x.shape[0]
    n_seg = n // seg_len
    tps = seg_len // tm
    rope = rope_tabs is not None
    seg = lambda i: i // tps
    pos = lambda i: i % tps
    row = lambda i: (i, 0)

    in_specs = [
        pl.BlockSpec((tm, D_MODEL), row),
        pl.BlockSpec((1, 1, 6, D_MODEL), lambda i: (layer, mod_row_fn(seg(i)), 0, 0)),
        _resident((1, D_MODEL)),
        _resident(w_in.shape),
        _resident((1, LANES)),
        _resident((1, LANES)),
    ]
    args = [x, mod, g_pre, w_in, q_gain2, k_gain2]
    if rope:
        in_specs += [pl.BlockSpec((tm, LANES), lambda i: (pos(i), 0))] * 2
        args += list(rope_tabs)

    out_shape, out_specs = [], []
    if q_side:
        out_shape += [
            jax.ShapeDtypeStruct((n, POOL_WIDTH), F32),
            jax.ShapeDtypeStruct((n_seg, DIFF_HEADS, 2, LANES, seg_len), BF16),
            jax.ShapeDtypeStruct((n_seg, GQA_KV_HEADS, GQA_GROUP, LANES, seg_len), BF16),
            jax.ShapeDtypeStruct((n, N_BRANCH * D_MODEL), BF16),
        ]
        out_specs += [
            pl.BlockSpec((tm, POOL_WIDTH), row),
            pl.BlockSpec((1, DIFF_HEADS, 2, LANES, tm), lambda i: (seg(i), 0, 0, 0, pos(i))),
            pl.BlockSpec((1, GQA_KV_HEADS, GQA_GROUP, LANES, tm),
                         lambda i: (seg(i), 0, 0, 0, pos(i))),
            pl.BlockSpec((tm, N_BRANCH * D_MODEL), row),
        ]
    out_shape += [
        jax.ShapeDtypeStruct((n, DIFF_WIDTH), BF16),
        jax.ShapeDtypeStruct((n_seg, DIFF_HEADS, 2 * HEAD_DIM, seg_len), BF16),
        jax.ShapeDtypeStruct((n, GQA_KV_WIDTH), BF16),
        jax.ShapeDtypeStruct((n_seg, GQA_KV_WIDTH, seg_len), BF16),
    ]
    out_specs += [
        pl.BlockSpec((tm, DIFF_WIDTH), row),
        pl.BlockSpec((1, DIFF_HEADS, 2 * HEAD_DIM, tm), lambda i: (seg(i), 0, 0, pos(i))),
        pl.BlockSpec((tm, GQA_KV_WIDTH), row),
        pl.BlockSpec((1, GQA_KV_WIDTH, tm), lambda i: (seg(i), 0, pos(i))),
    ]
    return pl.pallas_call(
        functools.partial(_in_proj_kernel, rope=rope, q_side=q_side),
        grid=(n // tm,),
        in_specs=in_specs,
        out_specs=out_specs,
        out_shape=out_shape,
        scratch_shapes=[pltpu.VMEM((tm, D_MODEL), BF16)],
        compiler_params=_cparams(("arbitrary",)),
        name="in_proj" + ("_rope" if rope else "") + ("" if q_side else "_kv"),
    )(*args)


class _Panels:
    def __init__(self, n_b, n_h, lq, pw, per_out):
        assert per_out % 2 == 0
        self.n_h, self.per_out, self.tiles = n_h, per_out, lq // pw
        self.per_head = per_out * self.tiles
        self.total = n_b * n_h * self.per_head

    def split(self, q):
        bh, loc = q // self.per_head, q % self.per_head
        return bh // self.n_h, bh % self.n_h, loc % self.per_out, loc // self.per_out

    @property
    def steps(self):
        return self.total // 2 + 1

    def loading(self, step):
        b, h, j, n = self.split(jnp.minimum(2 * step, self.total - 2))
        return b, h, j // 2, n

    def emitting(self, step):
        return self.split(jnp.clip(2 * step - 2, 0, self.total - 1))


def _attn_scratch(nk, pw, v_rows, per_out):
    return [
        pltpu.VMEM((2, nk, pw), F32),
        pltpu.VMEM((2, 8, pw), F32),
        pltpu.VMEM((per_out, v_rows, pw), F32),
        pltpu.VMEM((2, v_rows + BF16_ROWS, nk), BF16),
    ]


def _attn_pipeline(qt_ref, k_refs, vt_refs, s_scr, m_scr, on_scr, vt_scr, *, panels, emit):
    step = pl.program_id(0)
    total, per_head, per_out = panels.total, panels.per_head, panels.per_out
    half = total // 2
    v_rows = on_scr.shape[1]

    @pl.when((step < half) & ((2 * step) % per_head == 0))
    def _():
        dst = vt_scr.at[((2 * step) // per_head) & 1]
        c = 0
        for vt_ref in vt_refs:
            vt = vt_ref[...]
            dst[0:v_rows, c:c + vt.shape[-1]] = vt.reshape(vt.shape[-2:])
            c += vt.shape[-1]
        dst[v_rows:, :] = jnp.ones((dst.shape[0] - v_rows, dst.shape[1]), BF16)

    key_chunks = [(k_ref, c0) for k_ref in k_refs for c0 in range(0, k_ref.shape[0], MXU_DIM)]

    def phase(sub, score, value):
        score_buf = sub if score else None
        value_buf = 1 - sub if value else None
        q_done = 2 * step + sub - 1
        if score_buf is not None:
            qt = qt_ref[0, 0, sub]
        if value_buf is not None:
            m_prev = m_scr[value_buf][0:1]
            vt = vt_scr.at[(q_done // per_head) & 1]
        m, acc = None, None
        for c, (k_ref, c0) in enumerate(key_chunks):
            rows = slice(c * MXU_DIM, (c + 1) * MXU_DIM)
            if value_buf is not None:
                pt = jnp.exp2(s_scr[value_buf, rows, :] - m_prev).astype(BF16)
                d = jnp.dot(vt[:, rows], pt, preferred_element_type=F32)
                acc = d if acc is None else acc + d
            if score_buf is not None:
                s = jnp.dot(k_ref[c0:c0 + MXU_DIM, :], qt, preferred_element_type=F32)
                s_scr[score_buf, rows, :] = s
                mc = jnp.max(s, axis=0, keepdims=True)
                m = mc if m is None else jnp.maximum(m, mc)
        if score_buf is not None:
            m_scr[score_buf] = jnp.broadcast_to(m, m_scr.shape[1:])
        if value_buf is not None:
            on_scr[q_done % per_out] = acc[0:v_rows] / acc[v_rows:v_rows + 1]

    @pl.when(step == 0)
    def _():
        on_scr[...] = jnp.zeros(on_scr.shape, F32)
        phase(0, True, False)

    @pl.when((step > 0) & (step < half))
    def _():
        emit()
        phase(0, True, True)

    @pl.when(step == half)
    def _():
        emit()
        phase(0, False, True)

    @pl.when(step < half)
    def _():
        emit()
        phase(1, True, True)

    @pl.when(step == half)
    def _():
        emit()


def _diff_attn_kernel(*refs, n_seg, panels, lam_init):
    it = iter(refs)
    lam_ref, subg_ref, qt_ref = next(it), next(it), next(it)
    k_refs = [next(it) for _ in range(n_seg)]
    vt_refs = [next(it) for _ in range(n_seg)]
    o_ref, s_scr, m_scr, on_scr, vt_scr = (next(it) for _ in range(5))

    def emit():
        lp = lam_ref[...]
        lam = (jnp.exp(jnp.sum(lp[0:1] * lp[1:2], axis=-1, keepdims=True))
               - jnp.exp(jnp.sum(lp[2:3] * lp[3:4], axis=-1, keepdims=True)) + lam_init)
        o = on_scr[0] - lam * on_scr[1]
        ms = jnp.mean(o * o, axis=0, keepdims=True)
        o = o * lax.rsqrt(ms + EPS) * subg_ref[...] * (1.0 - lam_init)
        o_ref[...] = o.T.astype(BF16)

    _attn_pipeline(qt_ref, k_refs, vt_refs, s_scr, m_scr, on_scr, vt_scr,
                   panels=panels, emit=emit)


def _diff_attn(dqt, ks, vts, lam_params, sub_gain_col, lam_init, *, pw):
    n_b, n_h, per_out, _, lq = dqt.shape
    lks = tuple(vt.shape[-1] for vt in vts)
    vw = 2 * HEAD_DIM
    panels = _Panels(n_b, n_h, lq, pw, per_out)

    def qt_map(g):
        b, h, j, n = panels.loading(g)
        return b, h, j, 0, n

    def k_map(g):
        b, h, _, _ = panels.loading(g)
        return b, h

    def vt_map(g):
        b, h, _, _ = panels.loading(g)
        return b, h, 0, 0

    def out_map(g):
        b, h, _, n = panels.emitting(g)
        return b * panels.tiles + n, h

    in_specs = [
        _resident(lam_params.shape),
        _resident(sub_gain_col.shape),
        pl.BlockSpec((1, 1, 2, LANES, pw), qt_map),
    ]
    in_specs += [pl.BlockSpec((lk, LANES), k_map) for lk in lks]
    in_specs += [pl.BlockSpec((1, 1, vw, lk), vt_map) for lk in lks]
    return pl.pallas_call(
        functools.partial(_diff_attn_kernel, n_seg=len(lks), panels=panels, lam_init=lam_init),
        grid=(panels.steps,),
        in_specs=in_specs,
        out_specs=pl.BlockSpec((pw, vw), out_map),
        out_shape=jax.ShapeDtypeStruct((n_b * lq, DIFF_WIDTH), BF16),
        scratch_shapes=_attn_scratch(sum(lks), pw, vw, per_out),
        compiler_params=_cparams(("arbitrary",)),
        name="diff_attn",
    )(lam_params, sub_gain_col, dqt, *ks, *vts)


def _gqa_attn_kernel(*refs, n_seg, panels):
    it = iter(refs)
    qt_ref = next(it)
    k_refs = [next(it) for _ in range(n_seg)]
    vt_refs = [next(it) for _ in range(n_seg)]
    o_ref, s_scr, m_scr, on_scr, vt_scr = (next(it) for _ in range(5))

    def emit():
        pairs = [jnp.concatenate([on_scr[2 * i], on_scr[2 * i + 1]], axis=0).T
                 for i in range(GQA_GROUP // 2)]
        o_ref[...] = jnp.concatenate(pairs, axis=1).astype(BF16)

    _attn_pipeline(qt_ref, k_refs, vt_refs, s_scr, m_scr, on_scr, vt_scr,
                   panels=panels, emit=emit)


def _gqa_attn(gqt, ks, vts, *, pw):
    n_b, n_h, per_out, _, lq = gqt.shape
    lks = tuple(vt.shape[-1] for vt in vts)
    panels = _Panels(n_b, n_h, lq, pw, per_out)

    def qt_map(g):
        b, h, j, n = panels.loading(g)
        return b, h, j, 0, n

    def k_map(g):
        return panels.loading(g)[0], 0

    def vt_map(g):
        b, h, _, _ = panels.loading(g)
        return b, h, 0

    def out_map(g):
        b, h, _, n = panels.emitting(g)
        return b * panels.tiles + n, h

    in_specs = [pl.BlockSpec((1, 1, 2, LANES, pw), qt_map)]
    in_specs += [pl.BlockSpec((lk, LANES), k_map) for lk in lks]
    in_specs += [pl.BlockSpec((1, HEAD_DIM, lk), vt_map) for lk in lks]
    return pl.pallas_call(
        functools.partial(_gqa_attn_kernel, n_seg=len(lks), panels=panels),
        grid=(panels.steps,),
        in_specs=in_specs,
        out_specs=pl.BlockSpec((pw, GQA_GROUP * HEAD_DIM), out_map),
        out_shape=jax.ShapeDtypeStruct((n_b * lq, GQA_WIDTH), BF16),
        scratch_shapes=_attn_scratch(sum(lks), pw, HEAD_DIM, per_out),
        compiler_params=_cparams(("arbitrary",)),
        name="gqa_attn",
    )(gqt, *ks, *vts)


def _mix_ffn_kernel(x_ref, zp_ref, z_ref, zn_ref, icnt_ref, do_ref, go_ref, gate_ref, mod_ref,
                    wgrp_ref, pscale_ref, wpo_ref, wdo_ref, wgo_ref, wo_ref,
                    gpost_ref, gpre_ref, w1_ref, w2_ref, gpost_ffn_ref,
                    xo_ref, zext_scr, xn_scr, hf_scr, h_scr, *, seg_len, tm):
    i = pl.program_id(0)
    tps = seg_len // tm
    t = i % tps
    z = z_ref[...]
    zext_scr[0:POOL_HALO] = jnp.where(t > 0, zp_ref[...], 0.0)
    zext_scr[POOL_HALO:POOL_HALO + tm] = z
    zext_scr[POOL_HALO + tm:] = jnp.where(t < tps - 1, zn_ref[...], 0.0)
    mod = mod_ref[0, 0]
    sub = min(tm, SUB_ROWS)

    def pool_mixer(r0, _):
        mixed = []
        for g, w in enumerate(POOL_WINDOWS):
            cs = slice(g * POOL_GROUP_DIM, (g + 1) * POOL_GROUP_DIM)
            ssum = None
            for d in range(-(w // 2), w // 2):
                zz = zext_scr[POOL_HALO + d + r0:POOL_HALO + d + r0 + sub, cs]
                ssum = zz if ssum is None else ssum + zz
            pooled = ssum * icnt_ref[r0:r0 + sub, cs] - z_ref[r0:r0 + sub, cs]
            mixed.append(jnp.dot(pooled.astype(BF16), wgrp_ref[g], preferred_element_type=F32))
        return (jnp.concatenate(mixed, axis=1) * pscale_ref[...]).astype(BF16)

    def gated_branches(r0, pool_out):
        rs = slice(r0, r0 + sub)

        def gate(k):
            return gate_ref[rs, k * D_MODEL:(k + 1) * D_MODEL].astype(F32)

        merged = gate(0) * jnp.dot(pool_out, wpo_ref[...], preferred_element_type=F32)
        merged += gate(1) * jnp.dot(do_ref[rs, :], wdo_ref[...], preferred_element_type=F32)
        merged += gate(2) * jnp.dot(go_ref[rs, :], wgo_ref[...], preferred_element_type=F32)
        return merged.astype(BF16)

    def out_proj(r0, merged):
        return jnp.dot(merged, wo_ref[...], preferred_element_type=F32)

    def residual(r0, y):
        rs = slice(r0, r0 + sub)
        xn = x_ref[rs, :] + mod[2:3] * _rms(y, gpost_ref[...])
        xn_scr[rs, :] = xn
        hf_scr[rs, :] = (_rms(xn, gpre_ref[...]) * (1.0 + mod[4:5]) + mod[3:4]).astype(BF16)

    def ffn_up(r0, _):
        rs = slice(r0, r0 + sub)
        hf = hf_scr[rs, :]
        for j in range(D_FF // PROJ_CHUNK):
            cols = slice(j * PROJ_CHUNK, (j + 1) * PROJ_CHUNK)
            h = jnp.maximum(jnp.dot(hf, w1_ref[:, cols], preferred_element_type=F32), 0.0)
            h_scr[rs, cols] = (h * h).astype(BF16)

    def ffn_down(r0, _):
        rs = slice(r0, r0 + sub)
        y = jnp.dot(h_scr[rs, :], w2_ref[...], preferred_element_type=F32)
        xo_ref[rs, :] = xn_scr[rs, :] + mod[5:6] * _rms(y, gpost_ffn_ref[...])

    _staggered([pool_mixer, gated_branches, out_proj, residual, ffn_up, ffn_down],
               range(0, tm, sub))


def _pool_inverse_counts(seg_len, tm):
    tps = seg_len // tm
    t = jnp.arange(seg_len)
    if tps > 2:
        t = jnp.concatenate([t[:2 * tm], t[-tm:]])
    cols = []
    for w in POOL_WINDOWS:
        cnt = jnp.minimum(t + w // 2, seg_len) - jnp.maximum(t - w // 2, 0)
        cols.append(jnp.broadcast_to((1.0 / cnt.astype(F32))[:, None], (t.shape[0], POOL_GROUP_DIM)))
    table = jnp.concatenate(cols, axis=1)
    n_blocks = table.shape[0] // tm
    block = lambda pos: jnp.where(pos == tps - 1, n_blocks - 1, jnp.minimum(pos, 1))
    return table, block


def _mix_ffn(x, pool_in, diff_out, gqa_out, gates, mod, layer, mod_row_fn, inv_counts, w_grp,
             pool_scale, w_pool_out, w_diff_out, w_gqa_out, w_o, g_post, g_pre_ffn, w1, w2,
             g_post_ffn, *, seg_len, tm):
    n = x.shape[0]
    tps = seg_len // tm
    hb = tm // POOL_HALO
    n_hblk = n // POOL_HALO
    row = lambda i: (i, 0)
    icnt, icnt_block = inv_counts
    weights = (w_grp, pool_scale, w_pool_out, w_diff_out, w_gqa_out, w_o, g_post, g_pre_ffn,
               w1, w2, g_post_ffn)
    in_specs = [
        pl.BlockSpec((tm, D_MODEL), row),
        pl.BlockSpec((POOL_HALO, POOL_WIDTH), lambda i: (jnp.maximum(i * hb - 1, 0), 0)),
        pl.BlockSpec((tm, POOL_WIDTH), row),
        pl.BlockSpec((POOL_HALO, POOL_WIDTH), lambda i: (jnp.minimum((i + 1) * hb, n_hblk - 1), 0)),
        pl.BlockSpec((tm, POOL_WIDTH), lambda i: (icnt_block(i % tps), 0)),
        pl.BlockSpec((tm, DIFF_WIDTH), row),
        pl.BlockSpec((tm, GQA_WIDTH), row),
        pl.BlockSpec((tm, N_BRANCH * D_MODEL), row),
        pl.BlockSpec((1, 1, 6, D_MODEL), lambda i: (layer, mod_row_fn(i // tps), 0, 0)),
    ] + [_resident(w.shape) for w in weights]
    return pl.pallas_call(
        functools.partial(_mix_ffn_kernel, seg_len=seg_len, tm=tm),
        grid=(n // tm,),
        in_specs=in_specs,
        out_specs=pl.BlockSpec((tm, D_MODEL), row),
        out_shape=jax.ShapeDtypeStruct((n, D_MODEL), F32),
        scratch_shapes=[
            pltpu.VMEM((tm + 2 * POOL_HALO, POOL_WIDTH), F32),
            pltpu.VMEM((tm, D_MODEL), F32),
            pltpu.VMEM((tm, D_MODEL), BF16),
            pltpu.VMEM((tm, D_FF), BF16),
        ],
        compiler_params=_cparams(("arbitrary",)),
        name="mix_ffn",
    )(x, pool_in, pool_in, pool_in, icnt, diff_out, gqa_out, gates, mod, *weights)


def _rope_tables(seq):
    rows = seq // GRID_W
    row = jnp.repeat(jnp.arange(rows), GRID_W).astype(F32)
    col = jnp.tile(jnp.arange(GRID_W), rows).astype(F32)
    inv = 1.0 / (ROPE_BASE ** (jnp.arange(ROPE_PAIRS_PER_AXIS, dtype=F32) * 2.0 / ROPE_AXIS_DIM))
    ang = jnp.concatenate([row[:, None] * inv, col[:, None] * inv], axis=-1)
    cos, sin = jnp.cos(ang), jnp.sin(ang)
    return (jnp.tile(cos, (1, 4)),
            jnp.concatenate([-sin, -sin, sin, sin], axis=-1))


def _pair_layout(w, member_major):
    d = w.shape[0]
    if member_major:
        w5, perm = w.reshape(d, 2, -1, HALF_HEAD, 2), (0, 2, 4, 1, 3)
    else:
        w5, perm = w.reshape(d, -1, 2, HALF_HEAD, 2), (0, 1, 4, 2, 3)
    return w5.transpose(perm).reshape(d, -1)


def _in_proj_weights(w):
    seg = lambda a, b: w[:, a:b]
    return jnp.concatenate([
        seg(OFF_POOL, OFF_DQ),
        _pair_layout(seg(OFF_DQ, OFF_GQ), False),
        _pair_layout(seg(OFF_GQ, OFF_GATE), True),
        seg(OFF_GATE, OFF_DK),
        _pair_layout(seg(OFF_DK, OFF_DV), False),
        seg(OFF_DV, OFF_GK),
        _pair_layout(seg(OFF_GK, OFF_GV), False),
        seg(OFF_GV, IN_WIDTH),
    ], axis=1).astype(BF16)


def _pair_gain(g):
    return jnp.concatenate([g[0::2], g[0::2], g[1::2], g[1::2]]).reshape(1, LANES)


def kernel(x, c, ctx, c_ctx, w_mod, b_mod, g_pre_mix, g_post_mix, g_pre_ffn, g_post_ffn,
           w_in, w_pool_grp, pool_scale, lambda_q1, lambda_k1, lambda_q2, lambda_k2,
           diff_subln, gqa_q_norm, gqa_k_norm, w_pool_out, w_diff_out, w_gqa_out, w_o,
           w_ff1, w_ff2):
    batch, seq, d = x.shape
    ctx_len = ctx.shape[1]
    depth = w_mod.shape[0]
    assert d == D_MODEL and batch < MOD_ROWS and w_in.shape[-1] == IN_WIDTH
    assert GQA_KV_HEADS == 2 and ctx_len % MXU_DIM == 0

    lat_tm, ctx_tm = 512, ctx_len
    rope_tabs = _rope_tables(seq)
    lat_counts = _pool_inverse_counts(seq, lat_tm)
    ctx_counts = _pool_inverse_counts(ctx_len, ctx_tm)
    c_all = jnp.zeros((MOD_ROWS, d), F32).at[:batch].set(c).at[batch].set(c_ctx)
    mod = _modulation(c_all, w_mod, b_mod)
    lat_row = lambda s: s
    ctx_row = lambda s: batch

    xl = x.reshape(batch * seq, d)
    xc = ctx.reshape(batch * ctx_len, d)
    row = lambda a: a.reshape(1, -1)

    for l in range(depth):
        last = l == depth - 1
        lam_init = 0.8 - 0.6 * math.exp(-0.3 * l)
        w_in_b = _in_proj_weights(w_in[l])
        qg2, kg2 = _pair_gain(gqa_q_norm[l]), _pair_gain(gqa_k_norm[l])
        lam_params = jnp.stack([lambda_q1[l], lambda_k1[l], lambda_q2[l], lambda_k2[l]])
        sub_gain = diff_subln[l].reshape(-1, 1)
        mix_w = (w_pool_grp[l].astype(BF16), row(pool_scale[l]), w_pool_out[l].astype(BF16),
                 w_diff_out[l].astype(BF16), w_gqa_out[l].astype(BF16), w_o[l].astype(BF16),
                 row(g_post_mix[l]), row(g_pre_ffn[l]),
                 w_ff1[l].astype(BF16), w_ff2[l].astype(BF16), row(g_post_ffn[l]))

        cp = _in_proj(xc, mod, l, ctx_row, row(g_pre_mix[l]), w_in_b, qg2, kg2, None,
                      seg_len=ctx_len, tm=ctx_tm, q_side=not last)
        c_dk, c_dvt, c_gk, c_gvt = cp[-4:]
        pool_in, dqt, gqt, gates, l_dk, l_dvt, l_gk, l_gvt = _in_proj(
            xl, mod, l, lat_row, row(g_pre_mix[l]), w_in_b, qg2, kg2, rope_tabs,
            seg_len=seq, tm=lat_tm, q_side=True)

        diff_out = _diff_attn(dqt, (c_dk, l_dk), (c_dvt, l_dvt), lam_params, sub_gain,
                              lam_init, pw=PANEL)
        gqa_out = _gqa_attn(gqt, (c_gk, l_gk), (c_gvt, l_gvt), pw=PANEL // 2)
        xl_next = _mix_ffn(xl, pool_in, diff_out, gqa_out, gates, mod, l, lat_row, lat_counts,
                           *mix_w, seg_len=seq, tm=lat_tm)

        if not last:
            c_pool, c_dqt, c_gqt, c_gates = cp[:4]
            c_diff = _diff_attn(c_dqt, (c_dk,), (c_dvt,), lam_params, sub_gain, lam_init,
                                pw=ctx_len)
            c_gqa = _gqa_attn(c_gqt, (c_gk,), (c_gvt,), pw=ctx_len)
            xc = _mix_ffn(xc, c_pool, c_diff, c_gqa, c_gates, mod, l, ctx_row, ctx_counts,
                          *mix_w, seg_len=ctx_len, tm=ctx_tm)
        xl = xl_next
    return xl.reshape(batch, seq, d)
```

```python
import functools
import math

import jax
import jax.numpy as jnp
from jax import lax
from jax.experimental import pallas as pl
from jax.experimental.pallas import tpu as pltpu

F32 = jnp.float32
BF16 = jnp.bfloat16

D_MODEL = 1024
GRID_W = 64
HEAD_DIM = 64
HALF_HEAD = HEAD_DIM // 2
ATTN_SCALE = HEAD_DIM ** -0.5
LOG2E = math.log2(math.e)
ROPE_BASE = 10000.0
ROPE_AXIS_DIM = HEAD_DIM // 2
ROPE_PAIRS_PER_AXIS = ROPE_AXIS_DIM // 2
EPS = 1e-6

POOL_WINDOWS = (2, 4, 8, 16)
POOL_WIDTH = D_MODEL // 2
POOL_GROUP_DIM = POOL_WIDTH // len(POOL_WINDOWS)
POOL_HALO = max(POOL_WINDOWS) // 2

DIFF_HEADS = D_MODEL // 256
DIFF_WIDTH = DIFF_HEADS * 2 * HEAD_DIM
GQA_HEADS = D_MODEL // 128
GQA_KV_HEADS = GQA_HEADS // 4
GQA_GROUP = GQA_HEADS // GQA_KV_HEADS
GQA_WIDTH = GQA_HEADS * HEAD_DIM
GQA_KV_WIDTH = GQA_KV_HEADS * HEAD_DIM
N_BRANCH = 3
D_FF = 4 * D_MODEL

OFF_POOL = 0
OFF_DQ = OFF_POOL + POOL_WIDTH
OFF_GQ = OFF_DQ + DIFF_WIDTH
OFF_GATE = OFF_GQ + GQA_WIDTH
OFF_DK = OFF_GATE + N_BRANCH * D_MODEL
OFF_DV = OFF_DK + DIFF_WIDTH
OFF_GK = OFF_DV + DIFF_WIDTH
OFF_GV = OFF_GK + GQA_KV_WIDTH
IN_WIDTH = OFF_GV + GQA_KV_WIDTH

LANES = 128
MXU_DIM = 256
PANEL = 4 * MXU_DIM
BF16_ROWS = 16
PROJ_CHUNK = 512
SUB_ROWS = 256
VMEM_LIMIT = 56 * 1024 * 1024

MOD_ROWS = 8


def _cparams(sem):
    return pltpu.CompilerParams(dimension_semantics=sem, vmem_limit_bytes=VMEM_LIMIT)


def _resident(shape):
    nd = len(shape)
    return pl.BlockSpec(shape, lambda *_: (0,) * nd, pipeline_mode=pl.Buffered(1))


def _rms(x, g):
    ms = jnp.mean(x * x, axis=-1, keepdims=True)
    return x * lax.rsqrt(ms + EPS) * g


def _staggered(stages, row_starts):
    row_starts = list(row_starts)
    carry = {}
    for step in range(len(row_starts) + len(stages) - 1):
        for k in reversed(range(len(stages))):
            i = step - k
            if 0 <= i < len(row_starts):
                carry[i] = stages[k](row_starts[i], carry.get(i))


def _mod_kernel(c_ref, w_ref, b_ref, o_ref):
    c = c_ref[...]
    a = c * (1.0 / (1.0 + jnp.exp(-c)))
    o_ref[0] = jnp.dot(a.astype(BF16), w_ref[0].astype(BF16),
                       preferred_element_type=F32) + b_ref[0]


def _modulation(c_all, w_mod, b_mod):
    depth = w_mod.shape[0]
    nblk = w_mod.shape[2] // D_MODEL
    out = pl.pallas_call(
        _mod_kernel,
        grid=(depth, nblk),
        in_specs=[
            pl.BlockSpec((MOD_ROWS, D_MODEL), lambda l, j: (0, 0)),
            pl.BlockSpec((1, D_MODEL, D_MODEL), lambda l, j: (l, 0, j)),
            pl.BlockSpec((1, 1, D_MODEL), lambda l, j: (l, 0, j)),
        ],
        out_specs=pl.BlockSpec((1, MOD_ROWS, D_MODEL), lambda l, j: (l, 0, j)),
        out_shape=jax.ShapeDtypeStruct((depth, MOD_ROWS, nblk * D_MODEL), F32),
        compiler_params=_cparams(("arbitrary", "arbitrary")),
        name="modulation",
    )(c_all, w_mod, b_mod.reshape(depth, 1, nblk * D_MODEL))
    return out.reshape(depth, MOD_ROWS, nblk, D_MODEL)


def _rope(x, cos, sin_signed):
    return x * cos + pltpu.roll(x, HEAD_DIM, 1) * sin_signed


def _head_norm(x, gain2):
    lane = lax.broadcasted_iota(jnp.int32, x.shape, 1)
    first = (lane & HALF_HEAD) == 0
    x2 = x * x
    ms_a = jnp.sum(jnp.where(first, x2, 0.0), axis=-1, keepdims=True) * (1.0 / HEAD_DIM)
    ms_b = jnp.sum(jnp.where(first, 0.0, x2), axis=-1, keepdims=True) * (1.0 / HEAD_DIM)
    inv = jnp.where(first, lax.rsqrt(ms_a + EPS), lax.rsqrt(ms_b + EPS))
    return x * inv * gain2


def _in_proj_kernel(*refs, rope, q_side):
    it = iter(refs)
    x_ref, mod_ref, gpre_ref, w_ref, qg_ref, kg_ref = (next(it) for _ in range(6))
    cos_ref = next(it) if rope else None
    sin_ref = next(it) if rope else None
    if q_side:
        pool_ref, dqt_ref, gqt_ref, gate_ref = (next(it) for _ in range(4))
    dk_ref, dvt_ref, gk_ref, gvt_ref = (next(it) for _ in range(4))
    hb_scr = next(it)

    tm = x_ref.shape[0]
    sub = min(tm, SUB_ROWS)
    mod = mod_ref[0, 0]
    q_scale = ATTN_SCALE * LOG2E

    def pre_norm(r0, _):
        rs = slice(r0, r0 + sub)
        h = _rms(x_ref[rs, :], gpre_ref[...]) * (1.0 + mod[1:2]) + mod[0:1]
        hb_scr[rs, :] = h.astype(BF16)

    def project(r0, _):
        rs = slice(r0, r0 + sub)
        hb = hb_scr[rs, :]
        if rope:
            cos, sin = cos_ref[rs, :], sin_ref[rs, :]

        def proj(off, width):
            return jnp.dot(hb, w_ref[:, off:off + width], preferred_element_type=F32)

        def blocks(off, width):
            y = proj(off, width)
            return [y[:, j * LANES:(j + 1) * LANES] for j in range(width // LANES)]

        def store_heads_t(ref, idx_a, idx_b, y):
            yt = y.T
            row = lax.broadcasted_iota(jnp.int32, yt.shape, 0)
            first = (row & HALF_HEAD) == 0
            ref[idx_a + (slice(None), rs)] = jnp.where(first, yt, 0.0).astype(BF16)
            ref[idx_b + (slice(None), rs)] = jnp.where(first, 0.0, yt).astype(BF16)

        def diff_keys():
            for hh, y in enumerate(blocks(OFF_DK, DIFF_WIDTH)):
                if rope:
                    y = _rope(y, cos, sin)
                dk_ref[rs, hh * LANES:(hh + 1) * LANES] = y.astype(BF16)

        def gqa_keys():
            (y,) = blocks(OFF_GK, GQA_KV_WIDTH)
            y = _head_norm(y, kg_ref[...])
            if rope:
                y = _rope(y, cos, sin)
            gk_ref[rs, :] = y.astype(BF16)

        def diff_queries():
            for hh, y in enumerate(blocks(OFF_DQ, DIFF_WIDTH)):
                if rope:
                    y = _rope(y, cos, sin)
                store_heads_t(dqt_ref, (0, hh, 0), (0, hh, 1), y * q_scale)

        def gqa_queries():
            for jj, y in enumerate(blocks(OFF_GQ, GQA_WIDTH)):
                y = _head_norm(y, qg_ref[...])
                if rope:
                    y = _rope(y, cos, sin)
                store_heads_t(gqt_ref, (0, 0, jj), (0, 1, jj), y * q_scale)

        def values():
            for hh, y in enumerate(blocks(OFF_DV, DIFF_WIDTH)):
                dvt_ref[0, hh, :, rs] = y.T.astype(BF16)
            (y,) = blocks(OFF_GV, GQA_KV_WIDTH)
            gvt_ref[0, :, rs] = y.T.astype(BF16)

        def gates(j):
            gl = proj(OFF_GATE + j * PROJ_CHUNK, PROJ_CHUNK)
            gate_ref[rs, j * PROJ_CHUNK:(j + 1) * PROJ_CHUNK] = (
                1.0 / (1.0 + jnp.exp(-gl))).astype(BF16)

        if q_side:
            n_gate = N_BRANCH * D_MODEL // PROJ_CHUNK
            order = [diff_keys, gqa_keys, 0, 1, diff_queries, 2, 3, gqa_queries, 4, 5, values]
            assert sorted(s for s in order if isinstance(s, int)) == list(range(n_gate))
            for step in order:
                if isinstance(step, int):
                    gates(step)
                else:
                    step()
            pool_ref[rs, :] = proj(OFF_POOL, POOL_WIDTH)
        else:
            diff_keys()
            gqa_keys()
            values()

    _staggered([pre_norm, project], range(0, tm, sub))


def _in_proj(x, mod, layer, mod_row_fn, g_pre, w_in, q_gain2, k_gain2, rope_tabs, *,
             seg_len, tm, q_side):
    n = x.shape[0]
    n_seg = n // seg_len
    tps = seg_len // tm
    rope = rope_tabs is not None
    seg = lambda i: i // tps
    pos = lambda i: i % tps
    row = lambda i: (i, 0)

    in_specs = [
        pl.BlockSpec((tm, D_MODEL), row),
        pl.BlockSpec((1, 1, 6, D_MODEL), lambda i: (layer, mod_row_fn(seg(i)), 0, 0)),
        _resident((1, D_MODEL)),
        _resident(w_in.shape),
        _resident((1, LANES)),
        _resident((1, LANES)),
    ]
    args = [x, mod, g_pre, w_in, q_gain2, k_gain2]
    if rope:
        in_specs += [pl.BlockSpec((tm, LANES), lambda i: (pos(i), 0))] * 2
        args += list(rope_tabs)

    out_shape, out_specs = [], []
    if q_side:
        out_shape += [
            jax.ShapeDtypeStruct((n, POOL_WIDTH), F32),
            jax.ShapeDtypeStruct((n_seg, DIFF_HEADS, 2, LANES, seg_len), BF16),
            jax.ShapeDtypeStruct((n_seg, GQA_KV_HEADS, GQA_GROUP, LANES, seg_len), BF16),
            jax.ShapeDtypeStruct((n, N_BRANCH * D_MODEL), BF16),
        ]
        out_specs += [
            pl.BlockSpec((tm, POOL_WIDTH), row),
            pl.BlockSpec((1, DIFF_HEADS, 2, LANES, tm), lambda i: (seg(i), 0, 0, 0, pos(i))),
            pl.BlockSpec((1, GQA_KV_HEADS, GQA_GROUP, LANES, tm),
                         lambda i: (seg(i), 0, 0, 0, pos(i))),
            pl.BlockSpec((tm, N_BRANCH * D_MODEL), row),
        ]
    out_shape += [
        jax.ShapeDtypeStruct((n, DIFF_WIDTH), BF16),
        jax.ShapeDtypeStruct((n_seg, DIFF_HEADS, 2 * HEAD_DIM, seg_len), BF16),
        jax.ShapeDtypeStruct((n, GQA_KV_WIDTH), BF16),
        jax.ShapeDtypeStruct((n_seg, GQA_KV_WIDTH, seg_len), BF16),
    ]
    out_specs += [
        pl.BlockSpec((tm, DIFF_WIDTH), row),
        pl.BlockSpec((1, DIFF_HEADS, 2 * HEAD_DIM, tm), lambda i: (seg(i), 0, 0, pos(i))),
        pl.BlockSpec((tm, GQA_KV_WIDTH), row),
        pl.BlockSpec((1, GQA_KV_WIDTH, tm), lambda i: (seg(i), 0, pos(i))),
    ]
    return pl.pallas_call(
        functools.partial(_in_proj_kernel, rope=rope, q_side=q_side),
        grid=(n // tm,),
        in_specs=in_specs,
        out_specs=out_specs,
        out_shape=out_shape,
        scratch_shapes=[pltpu.VMEM((tm, D_MODEL), BF16)],
        compiler_params=_cparams(("arbitrary",)),
        name="in_proj" + ("_rope" if rope else "") + ("" if q_side else "_kv"),
    )(*args)


class _Panels:
    def __init__(self, n_b, n_h, lq, pw, per_out):
        assert per_out % 2 == 0
        self.n_h, self.per_out, self.tiles = n_h, per_out, lq // pw
        self.n_sub = per_out
        self.per_head = per_out * self.tiles
        self.total = n_b * n_h * self.per_head

    def split(self, q):
        bh, loc = q // self.per_head, q % self.per_head
        return bh // self.n_h, bh % self.n_h, loc % self.per_out, loc // self.per_out

    @property
    def steps(self):
        return self.total // self.n_sub + 1

    def loading(self, step):
        b, h, _, n = self.split(jnp.minimum(self.n_sub * step, self.total - self.n_sub))
        return b, h, n

    def emitting(self, step):
        b, h, _, n = self.split(jnp.clip(self.n_sub * (step - 1), 0, self.total - 1))
        return b, h, n


def _attn_scratch(nk, pw, v_rows, per_out):
    return [
        pltpu.VMEM((2, nk, pw), F32),
        pltpu.VMEM((2, 8, pw), F32),
        pltpu.VMEM((per_out, v_rows, pw), F32),
        pltpu.VMEM((2, v_rows + BF16_ROWS, nk), BF16),
    ]


def _attn_pipeline(qt_ref, k_refs, vt_refs, s_scr, m_scr, on_scr, vt_scr, *, panels, emit):
    step = pl.program_id(0)
    total, per_head, per_out, n_sub = panels.total, panels.per_head, panels.per_out, panels.n_sub
    half = total // n_sub
    v_rows = on_scr.shape[1]

    @pl.when((step < half) & ((n_sub * step) % per_head == 0))
    def _():
        dst = vt_scr.at[((n_sub * step) // per_head) & 1]
        c = 0
        for vt_ref in vt_refs:
            vt = vt_ref[...]
            dst[0:v_rows, c:c + vt.shape[-1]] = vt.reshape(vt.shape[-2:])
            c += vt.shape[-1]
        dst[v_rows:, :] = jnp.ones((dst.shape[0] - v_rows, dst.shape[1]), BF16)

    key_chunks = [(k_ref, c0) for k_ref in k_refs for c0 in range(0, k_ref.shape[0], MXU_DIM)]

    def phase(sub, score, value):
        score_buf = sub % 2 if score else None
        value_buf = 1 - sub % 2 if value else None
        q_done = n_sub * step + sub - 1
        if score_buf is not None:
            qt = qt_ref[0, 0, sub]
        if value_buf is not None:
            m_prev = m_scr[value_buf][0:1]
            vt = vt_scr.at[(q_done // per_head) & 1]
        m, acc = None, None
        for c, (k_ref, c0) in enumerate(key_chunks):
            rows = slice(c * MXU_DIM, (c + 1) * MXU_DIM)
            if value_buf is not None:
                pt = jnp.exp2(s_scr[value_buf, rows, :] - m_prev).astype(BF16)
                d = jnp.dot(vt[:, rows], pt, preferred_element_type=F32)
                acc = d if acc is None else acc + d
            if score_buf is not None:
                s = jnp.dot(k_ref[c0:c0 + MXU_DIM, :], qt, preferred_element_type=F32)
                s_scr[score_buf, rows, :] = s
                mc = jnp.max(s, axis=0, keepdims=True)
                m = mc if m is None else jnp.maximum(m, mc)
        if score_buf is not None:
            m_scr[score_buf] = jnp.broadcast_to(m, m_scr.shape[1:])
        if value_buf is not None:
            on_scr[(sub - 1) % per_out] = acc[0:v_rows] / acc[v_rows:v_rows + 1]

    @pl.when(step == 0)
    def _():
        on_scr[...] = jnp.zeros(on_scr.shape, F32)
        phase(0, True, False)

    @pl.when((step > 0) & (step < half))
    def _():
        phase(0, True, True)

    @pl.when(step == half)
    def _():
        phase(0, False, True)

    @pl.when(step < half)
    def _():
        emit()
        phase(1, True, True)

    @pl.when(step == half)
    def _():
        emit()

    for sub in range(2, n_sub):
        @pl.when(step < half)
        def _():
            phase(sub, True, True)


def _diff_attn_kernel(*refs, n_seg, panels, lam_init):
    it = iter(refs)
    lam_ref, subg_ref, qt_ref = next(it), next(it), next(it)
    k_refs = [next(it) for _ in range(n_seg)]
    vt_refs = [next(it) for _ in range(n_seg)]
    o_ref, s_scr, m_scr, on_scr, vt_scr = (next(it) for _ in range(5))

    def emit():
        lp = lam_ref[...]
        lam = (jnp.exp(jnp.sum(lp[0:1] * lp[1:2], axis=-1, keepdims=True))
               - jnp.exp(jnp.sum(lp[2:3] * lp[3:4], axis=-1, keepdims=True)) + lam_init)
        o = on_scr[0] - lam * on_scr[1]
        ms = jnp.mean(o * o, axis=0, keepdims=True)
        o = o * lax.rsqrt(ms + EPS) * subg_ref[...] * (1.0 - lam_init)
        o_ref[...] = o.T.astype(BF16)

    _attn_pipeline(qt_ref, k_refs, vt_refs, s_scr, m_scr, on_scr, vt_scr,
                   panels=panels, emit=emit)


def _diff_attn(dqt, ks, vts, lam_params, sub_gain_col, lam_init, *, pw):
    n_b, n_h, per_out, _, lq = dqt.shape
    lks = tuple(vt.shape[-1] for vt in vts)
    vw = 2 * HEAD_DIM
    panels = _Panels(n_b, n_h, lq, pw, per_out)

    def qt_map(g):
        b, h, n = panels.loading(g)
        return b, h, 0, 0, n

    def k_map(g):
        b, h, _ = panels.loading(g)
        return b, h

    def vt_map(g):
        b, h, _ = panels.loading(g)
        return b, h, 0, 0

    def out_map(g):
        b, h, n = panels.emitting(g)
        return b * panels.tiles + n, h

    in_specs = [
        _resident(lam_params.shape),
        _resident(sub_gain_col.shape),
        pl.BlockSpec((1, 1, per_out, LANES, pw), qt_map),
    ]
    in_specs += [pl.BlockSpec((lk, LANES), k_map) for lk in lks]
    in_specs += [pl.BlockSpec((1, 1, vw, lk), vt_map) for lk in lks]
    return pl.pallas_call(
        functools.partial(_diff_attn_kernel, n_seg=len(lks), panels=panels, lam_init=lam_init),
        grid=(panels.steps,),
        in_specs=in_specs,
        out_specs=pl.BlockSpec((pw, vw), out_map),
        out_shape=jax.ShapeDtypeStruct((n_b * lq, DIFF_WIDTH), BF16),
        scratch_shapes=_attn_scratch(sum(lks), pw, vw, per_out),
        compiler_params=_cparams(("arbitrary",)),
        name="diff_attn",
    )(lam_params, sub_gain_col, dqt, *ks, *vts)


def _gqa_attn_kernel(*refs, n_seg, panels):
    it = iter(refs)
    qt_ref = next(it)
    k_refs = [next(it) for _ in range(n_seg)]
    vt_refs = [next(it) for _ in range(n_seg)]
    o_ref, s_scr, m_scr, on_scr, vt_scr = (next(it) for _ in range(5))

    def emit():
        pairs = [jnp.concatenate([on_scr[2 * i], on_scr[2 * i + 1]], axis=0).T
                 for i in range(GQA_GROUP // 2)]
        o_ref[...] = jnp.concatenate(pairs, axis=1).astype(BF16)

    _attn_pipeline(qt_ref, k_refs, vt_refs, s_scr, m_scr, on_scr, vt_scr,
                   panels=panels, emit=emit)


def _gqa_attn(gqt, ks, vts, *, pw):
    n_b, n_h, per_out, _, lq = gqt.shape
    lks = tuple(vt.shape[-1] for vt in vts)
    panels = _Panels(n_b, n_h, lq, pw, per_out)

    def qt_map(g):
        b, h, n = panels.loading(g)
        return b, h, 0, 0, n

    def k_map(g):
        return panels.loading(g)[0], 0

    def vt_map(g):
        b, h, _ = panels.loading(g)
        return b, h, 0

    def out_map(g):
        b, h, n = panels.emitting(g)
        return b * panels.tiles + n, h

    in_specs = [pl.BlockSpec((1, 1, per_out, LANES, pw), qt_map)]
    in_specs += [pl.BlockSpec((lk, LANES), k_map) for lk in lks]
    in_specs += [pl.BlockSpec((1, HEAD_DIM, lk), vt_map) for lk in lks]
    return pl.pallas_call(
        functools.partial(_gqa_attn_kernel, n_seg=len(lks), panels=panels),
        grid=(panels.steps,),
        in_specs=in_specs,
        out_specs=pl.BlockSpec((pw, GQA_GROUP * HEAD_DIM), out_map),
        out_shape=jax.ShapeDtypeStruct((n_b * lq, GQA_WIDTH), BF16),
        scratch_shapes=_attn_scratch(sum(lks), pw, HEAD_DIM, per_out),
        compiler_params=_cparams(("arbitrary",)),
        name="gqa_attn",
    )(gqt, *ks, *vts)


def _mix_ffn_kernel(x_ref, zp_ref, z_ref, zn_ref, icnt_ref, do_ref, go_ref, gate_ref, mod_ref,
                    wgrp_ref, pscale_ref, wpo_ref, wdo_ref, wgo_ref, wo_ref,
                    gpost_ref, gpre_ref, w1_ref, w2_ref, gpost_ffn_ref,
                    xo_ref, zext_scr, xn_scr, hf_scr, h_scr, *, seg_len, tm):
    i = pl.program_id(0)
    tps = seg_len // tm
    t = i % tps
    z = z_ref[...]
    zext_scr[0:POOL_HALO] = jnp.where(t > 0, zp_ref[...], 0.0)
    zext_scr[POOL_HALO:POOL_HALO + tm] = z
    zext_scr[POOL_HALO + tm:] = jnp.where(t < tps - 1, zn_ref[...], 0.0)
    mod = mod_ref[0, 0]
    sub = min(tm, SUB_ROWS)

    def pool_mixer(r0, _):
        mixed = []
        for g, w in enumerate(POOL_WINDOWS):
            cs = slice(g * POOL_GROUP_DIM, (g + 1) * POOL_GROUP_DIM)
            ssum = None
            for d in range(-(w // 2), w // 2):
                zz = zext_scr[POOL_HALO + d + r0:POOL_HALO + d + r0 + sub, cs]
                ssum = zz if ssum is None else ssum + zz
            pooled = ssum * icnt_ref[r0:r0 + sub, cs] - z_ref[r0:r0 + sub, cs]
            mixed.append(jnp.dot(pooled.astype(BF16), wgrp_ref[g], preferred_element_type=F32))
        return (jnp.concatenate(mixed, axis=1) * pscale_ref[...]).astype(BF16)

    def gated_branches(r0, pool_out):
        rs = slice(r0, r0 + sub)

        def gate(k):
            return gate_ref[rs, k * D_MODEL:(k + 1) * D_MODEL].astype(F32)

        merged = gate(0) * jnp.dot(pool_out, wpo_ref[...], preferred_element_type=F32)
        merged += gate(1) * jnp.dot(do_ref[rs, :], wdo_ref[...], preferred_element_type=F32)
        merged += gate(2) * jnp.dot(go_ref[rs, :], wgo_ref[...], preferred_element_type=F32)
        return merged.astype(BF16)

    def out_proj(r0, merged):
        return jnp.dot(merged, wo_ref[...], preferred_element_type=F32)

    def residual(r0, y):
        rs = slice(r0, r0 + sub)
        xn = x_ref[rs, :] + mod[2:3] * _rms(y, gpost_ref[...])
        xn_scr[rs, :] = xn
        hf_scr[rs, :] = (_rms(xn, gpre_ref[...]) * (1.0 + mod[4:5]) + mod[3:4]).astype(BF16)

    def ffn_up(r0, _):
        rs = slice(r0, r0 + sub)
        hf = hf_scr[rs, :]
        for j in range(D_FF // PROJ_CHUNK):
            cols = slice(j * PROJ_CHUNK, (j + 1) * PROJ_CHUNK)
            h = jnp.maximum(jnp.dot(hf, w1_ref[:, cols], preferred_element_type=F32), 0.0)
            h_scr[rs, cols] = (h * h).astype(BF16)

    def ffn_down(r0, _):
        rs = slice(r0, r0 + sub)
        y = jnp.dot(h_scr[rs, :], w2_ref[...], preferred_element_type=F32)
        xo_ref[rs, :] = xn_scr[rs, :] + mod[5:6] * _rms(y, gpost_ffn_ref[...])

    _staggered([pool_mixer, gated_branches, out_proj, residual, ffn_up, ffn_down],
               range(0, tm, sub))


def _pool_inverse_counts(seg_len, tm):
    tps = seg_len // tm
    t = jnp.arange(seg_len)
    if tps > 2:
        t = jnp.concatenate([t[:2 * tm], t[-tm:]])
    cols = []
    for w in POOL_WINDOWS:
        cnt = jnp.minimum(t + w // 2, seg_len) - jnp.maximum(t - w // 2, 0)
        cols.append(jnp.broadcast_to((1.0 / cnt.astype(F32))[:, None], (t.shape[0], POOL_GROUP_DIM)))
    table = jnp.concatenate(cols, axis=1)
    n_blocks = table.shape[0] // tm
    block = lambda pos: jnp.where(pos == tps - 1, n_blocks - 1, jnp.minimum(pos, 1))
    return table, block


def _mix_ffn(x, pool_in, diff_out, gqa_out, gates, mod, layer, mod_row_fn, inv_counts, w_grp,
             pool_scale, w_pool_out, w_diff_out, w_gqa_out, w_o, g_post, g_pre_ffn, w1, w2,
             g_post_ffn, *, seg_len, tm):
    n = x.shape[0]
    tps = seg_len // tm
    hb = tm // POOL_HALO
    n_hblk = n // POOL_HALO
    row = lambda i: (i, 0)
    icnt, icnt_block = inv_counts
    weights = (w_grp, pool_scale, w_pool_out, w_diff_out, w_gqa_out, w_o, g_post, g_pre_ffn,
               w1, w2, g_post_ffn)
    in_specs = [
        pl.BlockSpec((tm, D_MODEL), row),
        pl.BlockSpec((POOL_HALO, POOL_WIDTH), lambda i: (jnp.maximum(i * hb - 1, 0), 0)),
        pl.BlockSpec((tm, POOL_WIDTH), row),
        pl.BlockSpec((POOL_HALO, POOL_WIDTH), lambda i: (jnp.minimum((i + 1) * hb, n_hblk - 1), 0)),
        pl.BlockSpec((tm, POOL_WIDTH), lambda i: (icnt_block(i % tps), 0)),
        pl.BlockSpec((tm, DIFF_WIDTH), row),
        pl.BlockSpec((tm, GQA_WIDTH), row),
        pl.BlockSpec((tm, N_BRANCH * D_MODEL), row),
        pl.BlockSpec((1, 1, 6, D_MODEL), lambda i: (layer, mod_row_fn(i // tps), 0, 0)),
    ] + [_resident(w.shape) for w in weights]
    return pl.pallas_call(
        functools.partial(_mix_ffn_kernel, seg_len=seg_len, tm=tm),
        grid=(n // tm,),
        in_specs=in_specs,
        out_specs=pl.BlockSpec((tm, D_MODEL), row),
        out_shape=jax.ShapeDtypeStruct((n, D_MODEL), F32),
        scratch_shapes=[
            pltpu.VMEM((tm + 2 * POOL_HALO, POOL_WIDTH), F32),
            pltpu.VMEM((tm, D_MODEL), F32),
            pltpu.VMEM((tm, D_MODEL), BF16),
            pltpu.VMEM((tm, D_FF), BF16),
        ],
        compiler_params=_cparams(("arbitrary",)),
        name="mix_ffn",
    )(x, pool_in, pool_in, pool_in, icnt, diff_out, gqa_out, gates, mod, *weights)


def _rope_tables(seq):
    rows = seq // GRID_W
    row = jnp.repeat(jnp.arange(rows), GRID_W).astype(F32)
    col = jnp.tile(jnp.arange(GRID_W), rows).astype(F32)
    inv = 1.0 / (ROPE_BASE ** (jnp.arange(ROPE_PAIRS_PER_AXIS, dtype=F32) * 2.0 / ROPE_AXIS_DIM))
    ang = jnp.concatenate([row[:, None] * inv, col[:, None] * inv], axis=-1)
    cos, sin = jnp.cos(ang), jnp.sin(ang)
    return (jnp.tile(cos, (1, 4)),
            jnp.concatenate([-sin, -sin, sin, sin], axis=-1))


def _pair_layout(w, member_major):
    d = w.shape[0]
    if member_major:
        w5, perm = w.reshape(d, 2, -1, HALF_HEAD, 2), (0, 2, 4, 1, 3)
    else:
        w5, perm = w.reshape(d, -1, 2, HALF_HEAD, 2), (0, 1, 4, 2, 3)
    return w5.transpose(perm).reshape(d, -1)


def _in_proj_weights(w):
    seg = lambda a, b: w[:, a:b]
    return jnp.concatenate([
        seg(OFF_POOL, OFF_DQ),
        _pair_layout(seg(OFF_DQ, OFF_GQ), False),
        _pair_layout(seg(OFF_GQ, OFF_GATE), True),
        seg(OFF_GATE, OFF_DK),
        _pair_layout(seg(OFF_DK, OFF_DV), False),
        seg(OFF_DV, OFF_GK),
        _pair_layout(seg(OFF_GK, OFF_GV), False),
        seg(OFF_GV, IN_WIDTH),
    ], axis=1).astype(BF16)


def _pair_gain(g):
    return jnp.concatenate([g[0::2], g[0::2], g[1::2], g[1::2]]).reshape(1, LANES)


def kernel(x, c, ctx, c_ctx, w_mod, b_mod, g_pre_mix, g_post_mix, g_pre_ffn, g_post_ffn,
           w_in, w_pool_grp, pool_scale, lambda_q1, lambda_k1, lambda_q2, lambda_k2,
           diff_subln, gqa_q_norm, gqa_k_norm, w_pool_out, w_diff_out, w_gqa_out, w_o,
           w_ff1, w_ff2):
    batch, seq, d = x.shape
    ctx_len = ctx.shape[1]
    depth = w_mod.shape[0]
    assert d == D_MODEL and batch < MOD_ROWS and w_in.shape[-1] == IN_WIDTH
    assert GQA_KV_HEADS == 2 and ctx_len % MXU_DIM == 0

    lat_tm, ctx_tm = 512, ctx_len
    rope_tabs = _rope_tables(seq)
    lat_counts = _pool_inverse_counts(seq, lat_tm)
    ctx_counts = _pool_inverse_counts(ctx_len, ctx_tm)
    c_all = jnp.zeros((MOD_ROWS, d), F32).at[:batch].set(c).at[batch].set(c_ctx)
    mod = _modulation(c_all, w_mod, b_mod)
    lat_row = lambda s: s
    ctx_row = lambda s: batch

    xl = x.reshape(batch * seq, d)
    xc = ctx.reshape(batch * ctx_len, d)
    row = lambda a: a.reshape(1, -1)

    for l in range(depth):
        last = l == depth - 1
        lam_init = 0.8 - 0.6 * math.exp(-0.3 * l)
        w_in_b = _in_proj_weights(w_in[l])
        qg2, kg2 = _pair_gain(gqa_q_norm[l]), _pair_gain(gqa_k_norm[l])
        lam_params = jnp.stack([lambda_q1[l], lambda_k1[l], lambda_q2[l], lambda_k2[l]])
        sub_gain = diff_subln[l].reshape(-1, 1)
        mix_w = (w_pool_grp[l].astype(BF16), row(pool_scale[l]), w_pool_out[l].astype(BF16),
                 w_diff_out[l].astype(BF16), w_gqa_out[l].astype(BF16), w_o[l].astype(BF16),
                 row(g_post_mix[l]), row(g_pre_ffn[l]),
                 w_ff1[l].astype(BF16), w_ff2[l].astype(BF16), row(g_post_ffn[l]))

        cp = _in_proj(xc, mod, l, ctx_row, row(g_pre_mix[l]), w_in_b, qg2, kg2, None,
                      seg_len=ctx_len, tm=ctx_tm, q_side=not last)
        c_dk, c_dvt, c_gk, c_gvt = cp[-4:]
        pool_in, dqt, gqt, gates, l_dk, l_dvt, l_gk, l_gvt = _in_proj(
            xl, mod, l, lat_row, row(g_pre_mix[l]), w_in_b, qg2, kg2, rope_tabs,
            seg_len=seq, tm=lat_tm, q_side=True)

        diff_out = _diff_attn(dqt, (c_dk, l_dk), (c_dvt, l_dvt), lam_params, sub_gain,
                              lam_init, pw=PANEL)
        gqa_out = _gqa_attn(gqt, (c_gk, l_gk), (c_gvt, l_gvt), pw=PANEL // 2)
        xl_next = _mix_ffn(xl, pool_in, diff_out, gqa_out, gates, mod, l, lat_row, lat_counts,
                           *mix_w, seg_len=seq, tm=lat_tm)

        if not last:
            c_pool, c_dqt, c_gqt, c_gates = cp[:4]
            c_diff = _diff_attn(c_dqt, (c_dk,), (c_dvt,), lam_params, sub_gain, lam_init,
                                pw=ctx_len)
            c_gqa = _gqa_attn(c_gqt, (c_gk,), (c_gvt,), pw=ctx_len)
            xc = _mix_ffn(xc, c_pool, c_diff, c_gqa, c_gates, mod, l, ctx_row, ctx_counts,
                          *mix_w, seg_len=ctx_len, tm=ctx_tm)
        xl = xl_next
    return xl.reshape(batch, seq, d)
```

```python
import functools
import math

import jax
import jax.numpy as jnp
from jax import lax
from jax.experimental import pallas as pl
from jax.experimental.pallas import tpu as pltpu

F32 = jnp.float32
BF16 = jnp.bfloat16

D_MODEL = 1024
GRID_W = 64
HEAD_DIM = 64
HALF_HEAD = HEAD_DIM // 2
ATTN_SCALE = HEAD_DIM ** -0.5
LOG2E = math.log2(math.e)
ROPE_BASE = 10000.0
ROPE_AXIS_DIM = HEAD_DIM // 2
ROPE_PAIRS_PER_AXIS = ROPE_AXIS_DIM // 2
EPS = 1e-6

POOL_WINDOWS = (2, 4, 8, 16)
POOL_WIDTH = D_MODEL // 2
POOL_GROUP_DIM = POOL_WIDTH // len(POOL_WINDOWS)
POOL_HALO = max(POOL_WINDOWS) // 2

DIFF_HEADS = D_MODEL // 256
DIFF_WIDTH = DIFF_HEADS * 2 * HEAD_DIM
GQA_HEADS = D_MODEL // 128
GQA_KV_HEADS = GQA_HEADS // 4
GQA_GROUP = GQA_HEADS // GQA_KV_HEADS
GQA_WIDTH = GQA_HEADS * HEAD_DIM
GQA_KV_WIDTH = GQA_KV_HEADS * HEAD_DIM
N_BRANCH = 3
D_FF = 4 * D_MODEL

OFF_POOL = 0
OFF_DQ = OFF_POOL + POOL_WIDTH
OFF_GQ = OFF_DQ + DIFF_WIDTH
OFF_GATE = OFF_GQ + GQA_WIDTH
OFF_DK = OFF_GATE + N_BRANCH * D_MODEL
OFF_DV = OFF_DK + DIFF_WIDTH
OFF_GK = OFF_DV + DIFF_WIDTH
OFF_GV = OFF_GK + GQA_KV_WIDTH
IN_WIDTH = OFF_GV + GQA_KV_WIDTH

LANES = 128
MXU_DIM = 256
PANEL = 4 * MXU_DIM
BF16_ROWS = 16
PROJ_CHUNK = 512
SUB_ROWS = 256
VMEM_LIMIT = 56 * 1024 * 1024

MOD_ROWS = 8


def _cparams(sem):
    return pltpu.CompilerParams(dimension_semantics=sem, vmem_limit_bytes=VMEM_LIMIT)


def _resident(shape):
    nd = len(shape)
    return pl.BlockSpec(shape, lambda *_: (0,) * nd, pipeline_mode=pl.Buffered(1))


def _rms(x, g):
    ms = jnp.mean(x * x, axis=-1, keepdims=True)
    return x * lax.rsqrt(ms + EPS) * g


def _staggered(stages, row_starts):
    row_starts = list(row_starts)
    carry = {}
    for step in range(len(row_starts) + len(stages) - 1):
        for k in reversed(range(len(stages))):
            i = step - k
            if 0 <= i < len(row_starts):
                carry[i] = stages[k](row_starts[i], carry.get(i))


def _mod_kernel(c_ref, w_ref, b_ref, o_ref):
    c = c_ref[...]
    a = c * (1.0 / (1.0 + jnp.exp(-c)))
    o_ref[0] = jnp.dot(a.astype(BF16), w_ref[0].astype(BF16),
                       preferred_element_type=F32) + b_ref[0]


def _modulation(c_all, w_mod, b_mod):
    depth = w_mod.shape[0]
    nblk = w_mod.shape[2] // D_MODEL
    out = pl.pallas_call(
        _mod_kernel,
        grid=(depth, nblk),
        in_specs=[
            pl.BlockSpec((MOD_ROWS, D_MODEL), lambda l, j: (0, 0)),
            pl.BlockSpec((1, D_MODEL, D_MODEL), lambda l, j: (l, 0, j)),
            pl.BlockSpec((1, 1, D_MODEL), lambda l, j: (l, 0, j)),
        ],
        out_specs=pl.BlockSpec((1, MOD_ROWS, D_MODEL), lambda l, j: (l, 0, j)),
        out_shape=jax.ShapeDtypeStruct((depth, MOD_ROWS, nblk * D_MODEL), F32),
        compiler_params=_cparams(("arbitrary", "arbitrary")),
        name="modulation",
    )(c_all, w_mod, b_mod.reshape(depth, 1, nblk * D_MODEL))
    return out.reshape(depth, MOD_ROWS, nblk, D_MODEL)


def _rope(x, cos, sin_signed):
    return x * cos + pltpu.roll(x, HEAD_DIM, 1) * sin_signed


def _head_norm(x, gain2):
    lane = lax.broadcasted_iota(jnp.int32, x.shape, 1)
    first = (lane & HALF_HEAD) == 0
    x2 = x * x
    ms_a = jnp.sum(jnp.where(first, x2, 0.0), axis=-1, keepdims=True) * (1.0 / HEAD_DIM)
    ms_b = jnp.sum(jnp.where(first, 0.0, x2), axis=-1, keepdims=True) * (1.0 / HEAD_DIM)
    inv = jnp.where(first, lax.rsqrt(ms_a + EPS), lax.rsqrt(ms_b + EPS))
    return x * inv * gain2


def _in_proj_kernel(*refs, rope, q_side):
    it = iter(refs)
    x_ref, mod_ref, gpre_ref, w_ref, qg_ref, kg_ref = (next(it) for _ in range(6))
    cos_ref = next(it) if rope else None
    sin_ref = next(it) if rope else None
    if q_side:
        pool_ref, dqt_ref, gqt_ref, gate_ref = (next(it) for _ in range(4))
    dk_ref, dvt_ref, gk_ref, gvt_ref = (next(it) for _ in range(4))
    hb_scr = next(it)

    tm = x_ref.shape[0]
    sub = min(tm, SUB_ROWS)
    mod = mod_ref[0, 0]
    q_scale = ATTN_SCALE * LOG2E

    def pre_norm(r0, _):
        rs = slice(r0, r0 + sub)
        h = _rms(x_ref[rs, :], gpre_ref[...]) * (1.0 + mod[1:2]) + mod[0:1]
        hb_scr[rs, :] = h.astype(BF16)

    def project(r0, _):
        rs = slice(r0, r0 + sub)
        hb = hb_scr[rs, :]
        if rope:
            cos, sin = cos_ref[rs, :], sin_ref[rs, :]

        def proj(off, width):
            return jnp.dot(hb, w_ref[:, off:off + width], preferred_element_type=F32)

        def blocks(off, width):
            y = proj(off, width)
            return [y[:, j * LANES:(j + 1) * LANES] for j in range(width // LANES)]

        def store_heads_t(ref, idx_a, idx_b, y):
            yt = y.T
            row = lax.broadcasted_iota(jnp.int32, yt.shape, 0)
            first = (row & HALF_HEAD) == 0
            ref[idx_a + (slice(None), rs)] = jnp.where(first, yt, 0.0).astype(BF16)
            ref[idx_b + (slice(None), rs)] = jnp.where(first, 0.0, yt).astype(BF16)

        def diff_keys():
            for hh, y in enumerate(blocks(OFF_DK, DIFF_WIDTH)):
                if rope:
                    y = _rope(y, cos, sin)
                dk_ref[rs, hh * LANES:(hh + 1) * LANES] = y.astype(BF16)

        def gqa_keys():
            (y,) = blocks(OFF_GK, GQA_KV_WIDTH)
            y = _head_norm(y, kg_ref[...])
            if rope:
                y = _rope(y, cos, sin)
            gk_ref[rs, :] = y.astype(BF16)

        def diff_queries():
            for hh, y in enumerate(blocks(OFF_DQ, DIFF_WIDTH)):
                if rope:
                    y = _rope(y, cos, sin)
                store_heads_t(dqt_ref, (0, hh, 0), (0, hh, 1), y * q_scale)

        def gqa_queries():
            for jj, y in enumerate(blocks(OFF_GQ, GQA_WIDTH)):
                y = _head_norm(y, qg_ref[...])
                if rope:
                    y = _rope(y, cos, sin)
                store_heads_t(gqt_ref, (0, 0, jj), (0, 1, jj), y * q_scale)

        def values():
            for hh, y in enumerate(blocks(OFF_DV, DIFF_WIDTH)):
                dvt_ref[0, hh, :, rs] = y.T.astype(BF16)
            (y,) = blocks(OFF_GV, GQA_KV_WIDTH)
            gvt_ref[0, :, rs] = y.T.astype(BF16)

        def gates(j):
            gl = proj(OFF_GATE + j * PROJ_CHUNK, PROJ_CHUNK)
            gate_ref[rs, j * PROJ_CHUNK:(j + 1) * PROJ_CHUNK] = (
                1.0 / (1.0 + jnp.exp(-gl))).astype(BF16)

        if q_side:
            n_gate = N_BRANCH * D_MODEL // PROJ_CHUNK
            order = [diff_keys, gqa_keys, 0, 1, diff_queries, 2, 3, gqa_queries, 4, 5, values]
            assert sorted(s for s in order if isinstance(s, int)) == list(range(n_gate))
            for step in order:
                if isinstance(step, int):
                    gates(step)
                else:
                    step()
            pool_ref[rs, :] = proj(OFF_POOL, POOL_WIDTH)
        else:
            diff_keys()
            gqa_keys()
            values()

    _staggered([pre_norm, project], range(0, tm, sub))


def _in_proj(x, mod, layer, mod_row_fn, g_pre, w_in, q_gain2, k_gain2, rope_tabs, *,
             seg_len, tm, q_side):
    n = x.shape[0]
    n_seg = n // seg_len
    tps = seg_len // tm
    rope = rope_tabs is not None
    seg = lambda i: i // tps
    pos = lambda i: i % tps
    row = lambda i: (i, 0)

    in_specs = [
        pl.BlockSpec((tm, D_MODEL), row),
        pl.BlockSpec((1, 1, 6, D_MODEL), lambda i: (layer, mod_row_fn(seg(i)), 0, 0)),
        _resident((1, D_MODEL)),
        _resident(w_in.shape),
        _resident((1, LANES)),
        _resident((1, LANES)),
    ]
    args = [x, mod, g_pre, w_in, q_gain2, k_gain2]
    if rope:
        in_specs += [pl.BlockSpec((tm, LANES), lambda i: (pos(i), 0))] * 2
        args += list(rope_tabs)

    out_shape, out_specs = [], []
    if q_side:
        out_shape += [
            jax.ShapeDtypeStruct((n, POOL_WIDTH), F32),
            jax.ShapeDtypeStruct((n_seg, DIFF_HEADS, 2, LANES, seg_len), BF16),
            jax.ShapeDtypeStruct((n_seg, GQA_KV_HEADS, GQA_GROUP, LANES, seg_len), BF16),
            jax.ShapeDtypeStruct((n, N_BRANCH * D_MODEL), BF16),
        ]
        out_specs += [
            pl.BlockSpec((tm, POOL_WIDTH), row),
            pl.BlockSpec((1, DIFF_HEADS, 2, LANES, tm), lambda i: (seg(i), 0, 0, 0, pos(i))),
            pl.BlockSpec((1, GQA_KV_HEADS, GQA_GROUP, LANES, tm),
                         lambda i: (seg(i), 0, 0, 0, pos(i))),
            pl.BlockSpec((tm, N_BRANCH * D_MODEL), row),
        ]
    out_shape += [
        jax.ShapeDtypeStruct((n, DIFF_WIDTH), BF16),
        jax.ShapeDtypeStruct((n_seg, DIFF_HEADS, 2 * HEAD_DIM, seg_len), BF16),
        jax.ShapeDtypeStruct((n, GQA_KV_WIDTH), BF16),
        jax.ShapeDtypeStruct((n_seg, GQA_KV_WIDTH, seg_len), BF16),
    ]
    out_specs += [
        pl.BlockSpec((tm, DIFF_WIDTH), row),
        pl.BlockSpec((1, DIFF_HEADS, 2 * HEAD_DIM, tm), lambda i: (seg(i), 0, 0, pos(i))),
        pl.BlockSpec((tm, GQA_KV_WIDTH), row),
        pl.BlockSpec((1, GQA_KV_WIDTH, tm), lambda i: (seg(i), 0, pos(i))),
    ]
    return pl.pallas_call(
        functools.partial(_in_proj_kernel, rope=rope, q_side=q_side),
        grid=(n // tm,),
        in_specs=in_specs,
        out_specs=out_specs,
        out_shape=out_shape,
        scratch_shapes=[pltpu.VMEM((tm, D_MODEL), BF16)],
        compiler_params=_cparams(("arbitrary",)),
        name="in_proj" + ("_rope" if rope else "") + ("" if q_side else "_kv"),
    )(*args)


class _Panels:
    def __init__(self, n_b, n_h, lq, pw, per_out):
        assert per_out % 2 == 0
        self.n_h, self.per_out, self.tiles = n_h, per_out, lq // pw
        self.n_sub = per_out
        self.per_head = per_out * self.tiles
        self.total = n_b * n_h * self.per_head

    def split(self, q):
        bh, loc = q // self.per_head, q % self.per_head
        return bh // self.n_h, bh % self.n_h, loc % self.per_out, loc // self.per_out

    @property
    def steps(self):
        return self.total // self.n_sub + 1

    def loading(self, step):
        b, h, _, n = self.split(jnp.minimum(self.n_sub * step, self.total - self.n_sub))
        return b, h, n

    def emitting(self, step):
        b, h, _, n = self.split(jnp.clip(self.n_sub * (step - 1), 0, self.total - 1))
        return b, h, n


def _attn_scratch(nk, pw, v_rows, per_out):
    return [
        pltpu.VMEM((2, nk, pw), F32),
        pltpu.VMEM((2, 8, pw), F32),
        pltpu.VMEM((per_out, v_rows, pw), F32),
        pltpu.VMEM((2, v_rows + BF16_ROWS, nk), BF16),
    ]


def _attn_pipeline(qt_ref, k_refs, vt_refs, s_scr, m_scr, on_scr, vt_scr, *, panels, emit,
                   score_first=False):
    step = pl.program_id(0)
    total, per_head, per_out, n_sub = panels.total, panels.per_head, panels.per_out, panels.n_sub
    half = total // n_sub
    v_rows = on_scr.shape[1]

    @pl.when((step < half) & ((n_sub * step) % per_head == 0))
    def _():
        dst = vt_scr.at[((n_sub * step) // per_head) & 1]
        c = 0
        for vt_ref in vt_refs:
            vt = vt_ref[...]
            dst[0:v_rows, c:c + vt.shape[-1]] = vt.reshape(vt.shape[-2:])
            c += vt.shape[-1]
        dst[v_rows:, :] = jnp.ones((dst.shape[0] - v_rows, dst.shape[1]), BF16)

    key_chunks = [(k_ref, c0) for k_ref in k_refs for c0 in range(0, k_ref.shape[0], MXU_DIM)]

    def phase(sub, score, value):
        score_buf = sub % 2 if score else None
        value_buf = 1 - sub % 2 if value else None
        q_done = n_sub * step + sub - 1
        if score_buf is not None:
            qt = qt_ref[0, 0, sub]
        if value_buf is not None:
            m_prev = m_scr[value_buf][0:1]
            vt = vt_scr.at[(q_done // per_head) & 1]
        m, acc = None, None
        for c, (k_ref, c0) in enumerate(key_chunks):
            rows = slice(c * MXU_DIM, (c + 1) * MXU_DIM)
            for part in (("score", "value") if score_first else ("value", "score")):
                if part == "value" and value_buf is not None:
                    pt = jnp.exp2(s_scr[value_buf, rows, :] - m_prev).astype(BF16)
                    d = jnp.dot(vt[:, rows], pt, preferred_element_type=F32)
                    acc = d if acc is None else acc + d
                if part == "score" and score_buf is not None:
                    s = jnp.dot(k_ref[c0:c0 + MXU_DIM, :], qt, preferred_element_type=F32)
                    s_scr[score_buf, rows, :] = s
                    mc = jnp.max(s, axis=0, keepdims=True)
                    m = mc if m is None else jnp.maximum(m, mc)
        if score_buf is not None:
            m_scr[score_buf] = jnp.broadcast_to(m, m_scr.shape[1:])
        if value_buf is not None:
            on_scr[(sub - 1) % per_out] = acc[0:v_rows] / acc[v_rows:v_rows + 1]

    @pl.when(step == 0)
    def _():
        on_scr[...] = jnp.zeros(on_scr.shape, F32)
        phase(0, True, False)

    @pl.when((step > 0) & (step < half))
    def _():
        phase(0, True, True)

    @pl.when(step == half)
    def _():
        phase(0, False, True)

    @pl.when(step < half)
    def _():
        emit()
        phase(1, True, True)

    @pl.when(step == half)
    def _():
        emit()

    for sub in range(2, n_sub):
        @pl.when(step < half)
        def _():
            phase(sub, True, True)


def _diff_attn_kernel(*refs, n_seg, panels, lam_init):
    it = iter(refs)
    lam_ref, subg_ref, qt_ref = next(it), next(it), next(it)
    k_refs = [next(it) for _ in range(n_seg)]
    vt_refs = [next(it) for _ in range(n_seg)]
    o_ref, s_scr, m_scr, on_scr, vt_scr = (next(it) for _ in range(5))

    def emit():
        lp = lam_ref[...]
        lam = (jnp.exp(jnp.sum(lp[0:1] * lp[1:2], axis=-1, keepdims=True))
               - jnp.exp(jnp.sum(lp[2:3] * lp[3:4], axis=-1, keepdims=True)) + lam_init)
        o = on_scr[0] - lam * on_scr[1]
        ms = jnp.mean(o * o, axis=0, keepdims=True)
        o = o * lax.rsqrt(ms + EPS) * subg_ref[...] * (1.0 - lam_init)
        o_ref[...] = o.T.astype(BF16)

    _attn_pipeline(qt_ref, k_refs, vt_refs, s_scr, m_scr, on_scr, vt_scr,
                   panels=panels, emit=emit)


def _diff_attn(dqt, ks, vts, lam_params, sub_gain_col, lam_init, *, pw):
    n_b, n_h, per_out, _, lq = dqt.shape
    lks = tuple(vt.shape[-1] for vt in vts)
    vw = 2 * HEAD_DIM
    panels = _Panels(n_b, n_h, lq, pw, per_out)

    def qt_map(g):
        b, h, n = panels.loading(g)
        return b, h, 0, 0, n

    def k_map(g):
        b, h, _ = panels.loading(g)
        return b, h

    def vt_map(g):
        b, h, _ = panels.loading(g)
        return b, h, 0, 0

    def out_map(g):
        b, h, n = panels.emitting(g)
        return b * panels.tiles + n, h

    in_specs = [
        _resident(lam_params.shape),
        _resident(sub_gain_col.shape),
        pl.BlockSpec((1, 1, per_out, LANES, pw), qt_map),
    ]
    in_specs += [pl.BlockSpec((lk, LANES), k_map) for lk in lks]
    in_specs += [pl.BlockSpec((1, 1, vw, lk), vt_map) for lk in lks]
    return pl.pallas_call(
        functools.partial(_diff_attn_kernel, n_seg=len(lks), panels=panels, lam_init=lam_init),
        grid=(panels.steps,),
        in_specs=in_specs,
        out_specs=pl.BlockSpec((pw, vw), out_map),
        out_shape=jax.ShapeDtypeStruct((n_b * lq, DIFF_WIDTH), BF16),
        scratch_shapes=_attn_scratch(sum(lks), pw, vw, per_out),
        compiler_params=_cparams(("arbitrary",)),
        name="diff_attn",
    )(lam_params, sub_gain_col, dqt, *ks, *vts)


def _gqa_attn_kernel(*refs, n_seg, panels):
    it = iter(refs)
    qt_ref = next(it)
    k_refs = [next(it) for _ in range(n_seg)]
    vt_refs = [next(it) for _ in range(n_seg)]
    o_ref, s_scr, m_scr, on_scr, vt_scr = (next(it) for _ in range(5))

    def emit():
        pairs = [jnp.concatenate([on_scr[2 * i], on_scr[2 * i + 1]], axis=0).T
                 for i in range(GQA_GROUP // 2)]
        o_ref[...] = jnp.concatenate(pairs, axis=1).astype(BF16)

    _attn_pipeline(qt_ref, k_refs, vt_refs, s_scr, m_scr, on_scr, vt_scr,
                   panels=panels, emit=emit, score_first=True)


def _gqa_attn(gqt, ks, vts, *, pw):
    n_b, n_h, per_out, _, lq = gqt.shape
    lks = tuple(vt.shape[-1] for vt in vts)
    panels = _Panels(n_b, n_h, lq, pw, per_out)

    def qt_map(g):
        b, h, n = panels.loading(g)
        return b, h, 0, 0, n

    def k_map(g):
        return panels.loading(g)[0], 0

    def vt_map(g):
        b, h, _ = panels.loading(g)
        return b, h, 0

    def out_map(g):
        b, h, n = panels.emitting(g)
        return b * panels.tiles + n, h

    in_specs = [pl.BlockSpec((1, 1, per_out, LANES, pw), qt_map)]
    in_specs += [pl.BlockSpec((lk, LANES), k_map) for lk in lks]
    in_specs += [pl.BlockSpec((1, HEAD_DIM, lk), vt_map) for lk in lks]
    return pl.pallas_call(
        functools.partial(_gqa_attn_kernel, n_seg=len(lks), panels=panels),
        grid=(panels.steps,),
        in_specs=in_specs,
        out_specs=pl.BlockSpec((pw, GQA_GROUP * HEAD_DIM), out_map),
        out_shape=jax.ShapeDtypeStruct((n_b * lq, GQA_WIDTH), BF16),
        scratch_shapes=_attn_scratch(sum(lks), pw, HEAD_DIM, per_out),
        compiler_params=_cparams(("arbitrary",)),
        name="gqa_attn",
    )(gqt, *ks, *vts)


def _mix_ffn_kernel(x_ref, zp_ref, z_ref, zn_ref, icnt_ref, do_ref, go_ref, gate_ref, mod_ref,
                    wgrp_ref, pscale_ref, wpo_ref, wdo_ref, wgo_ref, wo_ref,
                    gpost_ref, gpre_ref, w1_ref, w2_ref, gpost_ffn_ref,
                    xo_ref, zext_scr, xn_scr, hf_scr, h_scr, *, seg_len, tm):
    i = pl.program_id(0)
    tps = seg_len // tm
    t = i % tps
    z = z_ref[...]
    zext_scr[0:POOL_HALO] = jnp.where(t > 0, zp_ref[...], 0.0)
    zext_scr[POOL_HALO:POOL_HALO + tm] = z
    zext_scr[POOL_HALO + tm:] = jnp.where(t < tps - 1, zn_ref[...], 0.0)
    mod = mod_ref[0, 0]
    sub = min(tm, SUB_ROWS)

    def pool_mixer(r0, _):
        mixed = []
        for g, w in enumerate(POOL_WINDOWS):
            cs = slice(g * POOL_GROUP_DIM, (g + 1) * POOL_GROUP_DIM)
            ssum = None
            for d in range(-(w // 2), w // 2):
                zz = zext_scr[POOL_HALO + d + r0:POOL_HALO + d + r0 + sub, cs]
                ssum = zz if ssum is None else ssum + zz
            pooled = ssum * icnt_ref[r0:r0 + sub, cs] - z_ref[r0:r0 + sub, cs]
            mixed.append(jnp.dot(pooled.astype(BF16), wgrp_ref[g], preferred_element_type=F32))
        return (jnp.concatenate(mixed, axis=1) * pscale_ref[...]).astype(BF16)

    def gated_branches(r0, pool_out):
        rs = slice(r0, r0 + sub)

        def gate(k):
            return gate_ref[rs, k * D_MODEL:(k + 1) * D_MODEL].astype(F32)

        merged = gate(0) * jnp.dot(pool_out, wpo_ref[...], preferred_element_type=F32)
        merged += gate(1) * jnp.dot(do_ref[rs, :], wdo_ref[...], preferred_element_type=F32)
        merged += gate(2) * jnp.dot(go_ref[rs, :], wgo_ref[...], preferred_element_type=F32)
        return merged.astype(BF16)

    def out_proj(r0, merged):
        return jnp.dot(merged, wo_ref[...], preferred_element_type=F32)

    def residual(r0, y):
        rs = slice(r0, r0 + sub)
        xn = x_ref[rs, :] + mod[2:3] * _rms(y, gpost_ref[...])
        xn_scr[rs, :] = xn
        hf_scr[rs, :] = (_rms(xn, gpre_ref[...]) * (1.0 + mod[4:5]) + mod[3:4]).astype(BF16)

    def ffn_up(r0, _):
        rs = slice(r0, r0 + sub)
        hf = hf_scr[rs, :]
        for j in range(D_FF // PROJ_CHUNK):
            cols = slice(j * PROJ_CHUNK, (j + 1) * PROJ_CHUNK)
            h = jnp.maximum(jnp.dot(hf, w1_ref[:, cols], preferred_element_type=F32), 0.0)
            h_scr[rs, cols] = (h * h).astype(BF16)

    def ffn_down(r0, _):
        rs = slice(r0, r0 + sub)
        y = jnp.dot(h_scr[rs, :], w2_ref[...], preferred_element_type=F32)
        xo_ref[rs, :] = xn_scr[rs, :] + mod[5:6] * _rms(y, gpost_ffn_ref[...])

    _staggered([pool_mixer, gated_branches, out_proj, residual, ffn_up, ffn_down],
               range(0, tm, sub))


def _pool_inverse_counts(seg_len, tm):
    tps = seg_len // tm
    t = jnp.arange(seg_len)
    if tps > 2:
        t = jnp.concatenate([t[:2 * tm], t[-tm:]])
    cols = []
    for w in POOL_WINDOWS:
        cnt = jnp.minimum(t + w // 2, seg_len) - jnp.maximum(t - w // 2, 0)
        cols.append(jnp.broadcast_to((1.0 / cnt.astype(F32))[:, None], (t.shape[0], POOL_GROUP_DIM)))
    table = jnp.concatenate(cols, axis=1)
    n_blocks = table.shape[0] // tm
    block = lambda pos: jnp.where(pos == tps - 1, n_blocks - 1, jnp.minimum(pos, 1))
    return table, block


def _mix_ffn(x, pool_in, diff_out, gqa_out, gates, mod, layer, mod_row_fn, inv_counts, w_grp,
             pool_scale, w_pool_out, w_diff_out, w_gqa_out, w_o, g_post, g_pre_ffn, w1, w2,
             g_post_ffn, *, seg_len, tm):
    n = x.shape[0]
    tps = seg_len // tm
    hb = tm // POOL_HALO
    n_hblk = n // POOL_HALO
    row = lambda i: (i, 0)
    icnt, icnt_block = inv_counts
    weights = (w_grp, pool_scale, w_pool_out, w_diff_out, w_gqa_out, w_o, g_post, g_pre_ffn,
               w1, w2, g_post_ffn)
    in_specs = [
        pl.BlockSpec((tm, D_MODEL), row),
        pl.BlockSpec((POOL_HALO, POOL_WIDTH), lambda i: (jnp.maximum(i * hb - 1, 0), 0)),
        pl.BlockSpec((tm, POOL_WIDTH), row),
        pl.BlockSpec((POOL_HALO, POOL_WIDTH), lambda i: (jnp.minimum((i + 1) * hb, n_hblk - 1), 0)),
        pl.BlockSpec((tm, POOL_WIDTH), lambda i: (icnt_block(i % tps), 0)),
        pl.BlockSpec((tm, DIFF_WIDTH), row),
        pl.BlockSpec((tm, GQA_WIDTH), row),
        pl.BlockSpec((tm, N_BRANCH * D_MODEL), row),
        pl.BlockSpec((1, 1, 6, D_MODEL), lambda i: (layer, mod_row_fn(i // tps), 0, 0)),
    ] + [_resident(w.shape) for w in weights]
    return pl.pallas_call(
        functools.partial(_mix_ffn_kernel, seg_len=seg_len, tm=tm),
        grid=(n // tm,),
        in_specs=in_specs,
        out_specs=pl.BlockSpec((tm, D_MODEL), row),
        out_shape=jax.ShapeDtypeStruct((n, D_MODEL), F32),
        scratch_shapes=[
            pltpu.VMEM((tm + 2 * POOL_HALO, POOL_WIDTH), F32),
            pltpu.VMEM((tm, D_MODEL), F32),
            pltpu.VMEM((tm, D_MODEL), BF16),
            pltpu.VMEM((tm, D_FF), BF16),
        ],
        compiler_params=_cparams(("arbitrary",)),
        name="mix_ffn",
    )(x, pool_in, pool_in, pool_in, icnt, diff_out, gqa_out, gates, mod, *weights)


def _rope_tables(seq):
    rows = seq // GRID_W
    row = jnp.repeat(jnp.arange(rows), GRID_W).astype(F32)
    col = jnp.tile(jnp.arange(GRID_W), rows).astype(F32)
    inv = 1.0 / (ROPE_BASE ** (jnp.arange(ROPE_PAIRS_PER_AXIS, dtype=F32) * 2.0 / ROPE_AXIS_DIM))
    ang = jnp.concatenate([row[:, None] * inv, col[:, None] * inv], axis=-1)
    cos, sin = jnp.cos(ang), jnp.sin(ang)
    return (jnp.tile(cos, (1, 4)),
            jnp.concatenate([-sin, -sin, sin, sin], axis=-1))


def _pair_layout(w, member_major):
    d = w.shape[0]
    if member_major:
        w5, perm = w.reshape(d, 2, -1, HALF_HEAD, 2), (0, 2, 4, 1, 3)
    else:
        w5, perm = w.reshape(d, -1, 2, HALF_HEAD, 2), (0, 1, 4, 2, 3)
    return w5.transpose(perm).reshape(d, -1)


def _in_proj_weights(w):
    seg = lambda a, b: w[:, a:b]
    return jnp.concatenate([
        seg(OFF_POOL, OFF_DQ),
        _pair_layout(seg(OFF_DQ, OFF_GQ), False),
        _pair_layout(seg(OFF_GQ, OFF_GATE), True),
        seg(OFF_GATE, OFF_DK),
        _pair_layout(seg(OFF_DK, OFF_DV), False),
        seg(OFF_DV, OFF_GK),
        _pair_layout(seg(OFF_GK, OFF_GV), False),
        seg(OFF_GV, IN_WIDTH),
    ], axis=1).astype(BF16)


def _pair_gain(g):
    return jnp.concatenate([g[0::2], g[0::2], g[1::2], g[1::2]]).reshape(1, LANES)


def kernel(x, c, ctx, c_ctx, w_mod, b_mod, g_pre_mix, g_post_mix, g_pre_ffn, g_post_ffn,
           w_in, w_pool_grp, pool_scale, lambda_q1, lambda_k1, lambda_q2, lambda_k2,
           diff_subln, gqa_q_norm, gqa_k_norm, w_pool_out, w_diff_out, w_gqa_out, w_o,
           w_ff1, w_ff2):
    batch, seq, d = x.shape
    ctx_len = ctx.shape[1]
    depth = w_mod.shape[0]
    assert d == D_MODEL and batch < MOD_ROWS and w_in.shape[-1] == IN_WIDTH
    assert GQA_KV_HEADS == 2 and ctx_len % MXU_DIM == 0

    lat_tm, ctx_tm = 512, ctx_len
    rope_tabs = _rope_tables(seq)
    lat_counts = _pool_inverse_counts(seq, lat_tm)
    ctx_counts = _pool_inverse_counts(ctx_len, ctx_tm)
    c_all = jnp.zeros((MOD_ROWS, d), F32).at[:batch].set(c).at[batch].set(c_ctx)
    mod = _modulation(c_all, w_mod, b_mod)
    lat_row = lambda s: s
    ctx_row = lambda s: batch

    xl = x.reshape(batch * seq, d)
    xc = ctx.reshape(batch * ctx_len, d)
    row = lambda a: a.reshape(1, -1)

    for l in range(depth):
        last = l == depth - 1
        lam_init = 0.8 - 0.6 * math.exp(-0.3 * l)
        w_in_b = _in_proj_weights(w_in[l])
        qg2, kg2 = _pair_gain(gqa_q_norm[l]), _pair_gain(gqa_k_norm[l])
        lam_params = jnp.stack([lambda_q1[l], lambda_k1[l], lambda_q2[l], lambda_k2[l]])
        sub_gain = diff_subln[l].reshape(-1, 1)
        mix_w = (w_pool_grp[l].astype(BF16), row(pool_scale[l]), w_pool_out[l].astype(BF16),
                 w_diff_out[l].astype(BF16), w_gqa_out[l].astype(BF16), w_o[l].astype(BF16),
                 row(g_post_mix[l]), row(g_pre_ffn[l]),
                 w_ff1[l].astype(BF16), w_ff2[l].astype(BF16), row(g_post_ffn[l]))

        cp = _in_proj(xc, mod, l, ctx_row, row(g_pre_mix[l]), w_in_b, qg2, kg2, None,
                      seg_len=ctx_len, tm=ctx_tm, q_side=not last)
        c_dk, c_dvt, c_gk, c_gvt = cp[-4:]
        pool_in, dqt, gqt, gates, l_dk, l_dvt, l_gk, l_gvt = _in_proj(
            xl, mod, l, lat_row, row(g_pre_mix[l]), w_in_b, qg2, kg2, rope_tabs,
            seg_len=seq, tm=lat_tm, q_side=True)

        diff_out = _diff_attn(dqt, (c_dk, l_dk), (c_dvt, l_dvt), lam_params, sub_gain,
                              lam_init, pw=PANEL)
        gqa_out = _gqa_attn(gqt, (c_gk, l_gk), (c_gvt, l_gvt), pw=PANEL // 2)
        xl_next = _mix_ffn(xl, pool_in, diff_out, gqa_out, gates, mod, l, lat_row, lat_counts,
                           *mix_w, seg_len=seq, tm=lat_tm)

        if not last:
            c_pool, c_dqt, c_gqt, c_gates = cp[:4]
            c_diff = _diff_attn(c_dqt, (c_dk,), (c_dvt,), lam_params, sub_gain, lam_init,
                                pw=ctx_len)
            c_gqa = _gqa_attn(c_gqt, (c_gk,), (c_gvt,), pw=ctx_len)
            xc = _mix_ffn(xc, c_pool, c_diff, c_gqa, c_gates, mod, l, ctx_row, ctx_counts,
                          *mix_w, seg_len=ctx_len, tm=ctx_tm)
        xl = xl_next
    return xl.reshape(batch, seq, d)
```
